```python
import math
import jax
import jax.numpy as jnp
from jax import lax
import numpy as np

D_MODEL = 1024
BATCH = 16
SEQ = 256
DEPTH = 4
DEC_BATCH = 8
DEC_SEQ = 2048
PAST_LEN = 512

GRID_W = 64
N_MIXERS = 3
N_RET = (DEPTH + 2) // 3
N_GLA = (DEPTH + 1) // 3
N_S5 = DEPTH // 3
RET_HEADS = 4
RET_DK = D_MODEL // RET_HEADS
RET_DV = 2 * D_MODEL // RET_HEADS
RET_CHUNK = 128
ROPE_BASE = 10000.0
GLA_HEADS = 4
GLA_KEY = D_MODEL // 2
GLA_DK = GLA_KEY // GLA_HEADS
GLA_DV = D_MODEL // GLA_HEADS
GLA_RANK = 16
GLA_TAU = 16.0
GLA_CHUNK = 16
S5_GROUP = 16
S5_GROUPS = D_MODEL // S5_GROUP
S5_STATE = 64
N_EXPERTS = 32
TOP_K = 4
D_FF = D_MODEL
SWIGLU_ALPHA = 1.702
SWIGLU_LIMIT = 7.0
MOE_BLOCK = 128
DN_ALPHA = (2 * DEPTH) ** 0.25
DN_BETA = (8 * DEPTH) ** -0.25
LN_EPS = 1e-5
GN_EPS = 1e-5

kernel_name = 'hybrid_ret_gla_s5_moe_diffusion_step'


def layer_norm(x, g, b):
    xf = x.astype(jnp.float32)
    mu = jnp.mean(xf, axis=-1, keepdims=True)
    var = jnp.mean(jnp.square(xf - mu), axis=-1, keepdims=True)
    return ((xf - mu) * lax.rsqrt(var + LN_EPS) * g.astype(jnp.float32) + b.astype(jnp.float32)).astype(x.dtype)


def head_norm(o):
    of = o.astype(jnp.float32)
    mu = jnp.mean(of, axis=-1, keepdims=True)
    var = jnp.mean(jnp.square(of - mu), axis=-1, keepdims=True)
    return ((of - mu) * lax.rsqrt(var + GN_EPS)).astype(o.dtype)


def grid_rotary(x):
    L = x.shape[1]
    rows = L // GRID_W
    row = jnp.repeat(jnp.arange(rows, dtype=jnp.float32), GRID_W)
    col = jnp.tile(jnp.arange(GRID_W, dtype=jnp.float32), rows)
    n_freq = x.shape[-1] // 4
    inv = ROPE_BASE ** (-jnp.arange(n_freq, dtype=jnp.float32) / n_freq)
    ang = jnp.concatenate([row[:, None] * inv, col[:, None] * inv], axis=-1)
    cos = jnp.cos(ang)[None, :, None, :].astype(x.dtype)
    sin = jnp.sin(ang)[None, :, None, :].astype(x.dtype)
    half = x.shape[-1] // 2
    x1, x2 = x[..., :half], x[..., half:]
    return jnp.concatenate([x1 * cos - x2 * sin, x1 * sin + x2 * cos], axis=-1)


def _to_chunks(t, chunk):
    bsz, L = t.shape[:2]
    return t.reshape(bsz, L // chunk, chunk, *t.shape[2:]).transpose(1, 0, 3, 2, 4)


def _from_chunks(t):
    n, bsz, H, chunk, d = t.shape
    return t.transpose(1, 0, 3, 2, 4).reshape(bsz, n * chunk, H, d)


def retention_scan(q, k, v, log_gamma, s0):
    dt = q.dtype
    idx = jnp.arange(RET_CHUNK, dtype=jnp.float32)
    rel = idx[:, None] - idx[None, :]
    lg = log_gamma[:, None, None]
    dmat = jnp.where(rel >= 0, jnp.exp(lg * jnp.maximum(rel, 0.0)), 0.0).astype(dt)
    xi = jnp.exp(log_gamma[:, None] * (idx + 1.0))[..., None].astype(dt)
    zeta = jnp.exp(log_gamma[:, None] * (RET_CHUNK - 1.0 - idx))[..., None].astype(dt)
    g_chunk = jnp.exp(log_gamma * RET_CHUNK)[:, None, None].astype(dt)

    def step(s, inp):
        qc, kc, vc = inp
        scores = jnp.einsum('bhnd,bhmd->bhnm', qc, kc) * dmat
        o = jnp.einsum('bhnm,bhme->bhne', scores, vc) + jnp.einsum('bhnd,bhde->bhne', qc, s) * xi
        s = s * g_chunk + jnp.einsum('bhmd,bhme->bhde', kc * zeta, vc)
        return s, o

    s, o = lax.scan(step, s0.astype(dt), (_to_chunks(q, RET_CHUNK), _to_chunks(k, RET_CHUNK), _to_chunks(v, RET_CHUNK)))
    return _from_chunks(o), s


def retention_mixer(h, w_in, decay_logit, w_out, s0, grid):
    bsz, L, _ = h.shape
    q, k, v, g = jnp.split(h @ w_in, [D_MODEL, 2 * D_MODEL, 4 * D_MODEL], axis=-1)
    q = q.reshape(bsz, L, RET_HEADS, RET_DK) * (RET_DK ** -0.5)
    k = k.reshape(bsz, L, RET_HEADS, RET_DK)
    v = v.reshape(bsz, L, RET_HEADS, RET_DV)
    if grid:
        q, k = grid_rotary(q), grid_rotary(k)
    log_gamma = jax.nn.log_sigmoid(decay_logit.astype(jnp.float32))
    o_f, s_f = retention_scan(q, k, v, log_gamma[0], s0[:, 0])
    o_b, s_b = retention_scan(q[:, ::-1], k[:, ::-1], v[:, ::-1], log_gamma[1], s0[:, 1])
    o = head_norm(o_f + o_b[:, ::-1]).reshape(bsz, L, 2 * D_MODEL)
    return (jax.nn.silu(g) * o) @ w_out, jnp.stack([s_f, s_b], axis=1)


def gla_scan(q, k, v, log_a, s0):
    mask = jnp.tril(jnp.ones((GLA_CHUNK, GLA_CHUNK), dtype=bool))[:, :, None]

    def step(s, inp):
        qc, kc, vc, gc = inp
        b = jnp.cumsum(gc, axis=2)
        diff = b[:, :, :, None, :] - b[:, :, None, :, :]
        decay = jnp.where(mask, jnp.exp(jnp.minimum(diff, 0.0)), 0.0).astype(qc.dtype)
        scores = jnp.einsum('bhnd,bhmd,bhnmd->bhnm', qc, kc, decay)
        b_last = b[:, :, -1:, :]
        o = (jnp.einsum('bhnm,bhme->bhne', scores, vc)
             + jnp.einsum('bhnd,bhde->bhne', qc * jnp.exp(b).astype(qc.dtype), s))
        s = (jnp.exp(b_last[:, :, 0, :, None]).astype(s.dtype) * s
             + jnp.einsum('bhmd,bhme->bhde', kc * jnp.exp(b_last - b).astype(kc.dtype), vc))
        return s, o

    s, o = lax.scan(step, s0.astype(q.dtype),
                    (_to_chunks(q, GLA_CHUNK), _to_chunks(k, GLA_CHUNK), _to_chunks(v, GLA_CHUNK), _to_chunks(log_a, GLA_CHUNK)))
    return _from_chunks(o), s


def gla_mixer(h, w_in, w_a1, w_a2, b_a, w_out, s0):
    bsz, L, _ = h.shape
    q, k, v, r = jnp.split(h @ w_in, [GLA_KEY, 2 * GLA_KEY, 2 * GLA_KEY + D_MODEL], axis=-1)
    q = q.reshape(bsz, L, GLA_HEADS, GLA_DK) * (GLA_DK ** -0.5)
    k = k.reshape(bsz, L, GLA_HEADS, GLA_DK)
    v = v.reshape(bsz, L, GLA_HEADS, GLA_DV)

    def log_gate(d):
        z = (h @ w_a1[d]) @ w_a2[d] + b_a[d]
        return (jax.nn.log_sigmoid(z.astype(jnp.float32)) / GLA_TAU).reshape(bsz, L, GLA_HEADS, GLA_DK)

    la_f, la_b = log_gate(0), log_gate(1)
    o_f, s_f = gla_scan(q, k, v, la_f, s0[:, 0])
    o_b, s_b = gla_scan(q[:, ::-1], k[:, ::-1], v[:, ::-1], la_b[:, ::-1], s0[:, 1])
    o = head_norm(o_f + o_b[:, ::-1]).reshape(bsz, L, D_MODEL)
    return (jax.nn.silu(r) * o) @ w_out, jnp.stack([s_f, s_b], axis=1)


def _complex_affine_combine(e1, e2):
    a1r, a1i, b1r, b1i = e1
    a2r, a2i, b2r, b2i = e2
    return (a2r * a1r - a2i * a1i,
            a2r * a1i + a2i * a1r,
            a2r * b1r - a2i * b1i + b2r,
            a2r * b1i + a2i * b1r + b2i)


def s5_scan(u, a_re, a_im, log_step, b_re, b_im, c_re, c_im, x0_re, x0_im):
    f32 = jnp.float32
    dt = u.dtype
    step = jnp.exp(log_step.astype(f32))[:, None]
    ar, ai = a_re.astype(f32), a_im.astype(f32)
    mag = jnp.exp(ar * step)
    ab_re, ab_im = mag * jnp.cos(ai * step), mag * jnp.sin(ai * step)
    den = ar * ar + ai * ai
    f_re = ((ab_re - 1.0) * ar + ab_im * ai) / den
    f_im = (ab_im * ar - (ab_re - 1.0) * ai) / den
    br, bi = b_re.astype(f32), b_im.astype(f32)
    bb_re = (f_re[..., None] * br - f_im[..., None] * bi).astype(dt)
    bb_im = (f_re[..., None] * bi + f_im[..., None] * br).astype(dt)
    ab_re, ab_im = ab_re.astype(dt), ab_im.astype(dt)
    bu_re = jnp.einsum('blgc,gpc->blgp', u, bb_re)
    bu_im = jnp.einsum('blgc,gpc->blgp', u, bb_im)
    x0r, x0i = x0_re.astype(dt), x0_im.astype(dt)
    bu_re = bu_re.at[:, 0].add(ab_re * x0r - ab_im * x0i)
    bu_im = bu_im.at[:, 0].add(ab_re * x0i + ab_im * x0r)
    L = u.shape[1]
    a_seq_re = jnp.broadcast_to(ab_re, (1, L) + ab_re.shape)
    a_seq_im = jnp.broadcast_to(ab_im, (1, L) + ab_im.shape)
    _, _, x_re, x_im = lax.associative_scan(_complex_affine_combine, (a_seq_re, a_seq_im, bu_re, bu_im), axis=1)
    y = jnp.einsum('blgp,gcp->blgc', x_re, c_re) - jnp.einsum('blgp,gcp->blgc', x_im, c_im)
    return y, x_re[:, -1], x_im[:, -1]


def s5_mixer(h, a_re, a_im, log_step, b_re, b_im, c_re, c_im, d_skip, w_glu, s0_re, s0_im):
    bsz, L, _ = h.shape
    u = h.reshape(bsz, L, S5_GROUPS, S5_GROUP)
    y_f, xf_re, xf_im = s5_scan(u, a_re[0], a_im[0], log_step[0], b_re[0], b_im[0], c_re[0], c_im[0], s0_re[:, 0], s0_im[:, 0])
    ub = u[:, ::-1]
    y_b, xb_re, xb_im = s5_scan(ub, a_re[1], a_im[1], log_step[1], b_re[1], b_im[1], c_re[1], c_im[1], s0_re[:, 1], s0_im[:, 1])
    y = (y_f + y_b[:, ::-1]).reshape(bsz, L, D_MODEL) + d_skip * h
    gv = jax.nn.gelu(y) @ w_glu
    out = gv[..., :D_MODEL] * jax.nn.sigmoid(gv[..., D_MODEL:])
    return out, jnp.stack([xf_re, xb_re], axis=1), jnp.stack([xf_im, xb_im], axis=1)


def moe(h, w_router, b_router, w_gu, b_gu, w_down, b_down):
    bsz, L, D = h.shape
    x = h.reshape(-1, D)
    n_tok = x.shape[0]
    logits = (x @ w_router + b_router).astype(jnp.float32)
    top_val, top_idx = lax.top_k(logits, TOP_K)
    gates = jax.nn.softmax(top_val, axis=-1)
    n_assign = n_tok * TOP_K
    flat_e = top_idx.reshape(-1)
    flat_tok = jnp.arange(n_assign, dtype=jnp.int32) // TOP_K
    order = jnp.argsort(flat_e)
    sorted_e = flat_e[order]
    counts = jnp.bincount(flat_e, length=N_EXPERTS)
    padded = (counts + MOE_BLOCK - 1) // MOE_BLOCK * MOE_BLOCK
    start = jnp.cumsum(counts) - counts
    pend = jnp.cumsum(padded)
    pstart = pend - padded
    dest = pstart[sorted_e] + jnp.arange(n_assign) - start[sorted_e]
    n_blocks = -(-n_assign // MOE_BLOCK) + N_EXPERTS
    n_rows = n_blocks * MOE_BLOCK
    row_tok = jnp.full((n_rows,), n_tok, jnp.int32).at[dest].set(flat_tok[order])
    row_gate = jnp.zeros((n_rows,), jnp.float32).at[dest].set(gates.reshape(-1)[order])
    block_e = jnp.minimum(jnp.searchsorted(pend, jnp.arange(n_blocks) * MOE_BLOCK, side='right'), N_EXPERTS - 1)
    x_pad = jnp.concatenate([x, jnp.zeros((1, D), x.dtype)], axis=0)
    xb = x_pad[row_tok].reshape(n_blocks, MOE_BLOCK, D)

    def expert_block(args):
        xblk, e = args
        gu = xblk @ w_gu[e] + b_gu[e]
        gate = jnp.minimum(gu[:, :D_FF], SWIGLU_LIMIT)
        lin = jnp.clip(gu[:, D_FF:], -SWIGLU_LIMIT, SWIGLU_LIMIT)
        act = gate * jax.nn.sigmoid(SWIGLU_ALPHA * gate) * (lin + 1.0)
        return act @ w_down[e] + b_down[e]

    yb = lax.map(expert_block, (xb, block_e)).reshape(n_rows, D)
    y = jax.ops.segment_sum(yb * row_gate[:, None].astype(yb.dtype), row_tok, num_segments=n_tok + 1)[:n_tok]
    return y.reshape(bsz, L, D)


def trunk(x, cond, ret_s0, gla_s0, s5_s0_re, s5_s0_im, p, grid):
    ret_i = gla_i = s5_i = 0
    new_ret, new_gla, new_s5_re, new_s5_im = [], [], [], []
    for layer in range(DEPTH):
        mod = jax.nn.silu(cond) @ p['w_mod'][layer] + p['b_mod'][layer]
        sh1, sc1, g1, sh2, sc2, g2 = jnp.split(mod[:, None, :], 6, axis=-1)
        h = x * (1.0 + sc1) + sh1
        kind = layer % N_MIXERS
        if kind == 0:
            y, st = retention_mixer(h, p['ret_w_in'][ret_i], p['ret_decay'][ret_i], p['ret_w_out'][ret_i], ret_s0[:, ret_i], grid)
            new_ret.append(st)
            ret_i += 1
        elif kind == 1:
            y, st = gla_mixer(h, p['gla_w_in'][gla_i], p['gla_w_a1'][gla_i], p['gla_w_a2'][gla_i], p['gla_b_a'][gla_i],
                              p['gla_w_out'][gla_i], gla_s0[:, gla_i])
            new_gla.append(st)
            gla_i += 1
        else:
            y, st_re, st_im = s5_mixer(h, p['s5_a_re'][s5_i], p['s5_a_im'][s5_i], p['s5_log_step'][s5_i],
                                       p['s5_b_re'][s5_i], p['s5_b_im'][s5_i], p['s5_c_re'][s5_i], p['s5_c_im'][s5_i],
                                       p['s5_d'][s5_i], p['s5_w_glu'][s5_i], s5_s0_re[:, s5_i], s5_s0_im[:, s5_i])
            new_s5_re.append(st_re)
            new_s5_im.append(st_im)
            s5_i += 1
        x = layer_norm(DN_ALPHA * x + g1 * y, p['ln_g'][layer, 0], p['ln_b'][layer, 0])
        h = x * (1.0 + sc2) + sh2
        y = moe(h, p['moe_w_router'][layer], p['moe_b_router'][layer], p['moe_w_gu'][layer], p['moe_b_gu'][layer],
                p['moe_w_down'][layer], p['moe_b_down'][layer])
        x = layer_norm(DN_ALPHA * x + g2 * y, p['ln_g'][layer, 1], p['ln_b'][layer, 1])
    return x, jnp.stack(new_ret, axis=1), jnp.stack(new_gla, axis=1), jnp.stack(new_s5_re, axis=1), jnp.stack(new_s5_im, axis=1)


def setup_inputs(seed: int = 0) -> dict:
    key = jax.random.key(seed)
    ks = iter(jax.random.split(key, 48))
    f32 = jnp.float32

    def nrm(shape, scale):
        return jax.random.normal(next(ks), shape, f32) * scale

    D = D_MODEL
    G, P = S5_GROUPS, S5_STATE
    ret_logit0 = jnp.log(2.0 ** (5.0 + jnp.arange(RET_HEADS, dtype=f32)) - 1.0)
    s5_log_step = jax.random.uniform(next(ks), (N_S5, 2, G), f32, minval=math.log(1e-3), maxval=math.log(1e-1))
    glu = nrm((N_S5, D, 2 * D), D ** -0.5)
    glu = glu.at[:, :, :D].multiply(DN_BETA)
    return {
        'x_prompt': nrm((BATCH, SEQ, D), 1.0),
        'x_sample': nrm((DEC_BATCH, DEC_SEQ, D), 1.0),
        'state_ret': nrm((DEC_BATCH, N_RET, 2, RET_HEADS, RET_DK, RET_DV), 1.0),
        'state_gla': nrm((DEC_BATCH, N_GLA, 2, GLA_HEADS, GLA_DK, GLA_DV), 1.0),
        'state_s5_re': nrm((DEC_BATCH, N_S5, 2, G, P), 0.1),
        'state_s5_im': nrm((DEC_BATCH, N_S5, 2, G, P), 0.1),
        'c': nrm((DEC_BATCH, D), 1.0),
        'c_ctx': nrm((D,), 1.0),
        'w_mod': nrm((DEPTH, D, 6 * D), D ** -0.5),
        'b_mod': nrm((DEPTH, 6 * D), 0.01),
        'ln_g': 1.0 + nrm((DEPTH, 2, D), 0.01),
        'ln_b': nrm((DEPTH, 2, D), 0.01),
        'ret_w_in': nrm((N_RET, D, 6 * D), D ** -0.5),
        'ret_decay': ret_logit0 + nrm((N_RET, 2, RET_HEADS), 0.01),
        'ret_w_out': nrm((N_RET, 2 * D, D), (2 * D) ** -0.5 * DN_BETA),
        'gla_w_in': nrm((N_GLA, D, 2 * GLA_KEY + 2 * D), D ** -0.5),
        'gla_w_a1': nrm((N_GLA, 2, D, GLA_RANK), D ** -0.5),
        'gla_w_a2': nrm((N_GLA, 2, GLA_RANK, GLA_KEY), GLA_RANK ** -0.5),
        'gla_b_a': nrm((N_GLA, 2, GLA_KEY), 0.01),
        'gla_w_out': nrm((N_GLA, D, D), D ** -0.5 * DN_BETA),
        's5_a_re': -0.5 + nrm((N_S5, 2, G, P), 0.01),
        's5_a_im': jnp.pi * jnp.arange(P, dtype=f32) + nrm((N_S5, 2, G, P), 0.01),
        's5_log_step': s5_log_step,
        's5_b_re': nrm((N_S5, 2, G, P, S5_GROUP), (2 * S5_GROUP) ** -0.5),
        's5_b_im': nrm((N_S5, 2, G, P, S5_GROUP), (2 * S5_GROUP) ** -0.5),
        's5_c_re': nrm((N_S5, 2, G, S5_GROUP, P), (2 * P) ** -0.5),
        's5_c_im': nrm((N_S5, 2, G, S5_GROUP, P), (2 * P) ** -0.5),
        's5_d': nrm((N_S5, D), 1.0),
        's5_w_glu': glu,
        'moe_w_router': nrm((DEPTH, D, N_EXPERTS), D ** -0.5),
        'moe_b_router': nrm((DEPTH, N_EXPERTS), 0.01),
        'moe_w_gu': nrm((DEPTH, N_EXPERTS, D, 2 * D_FF), D ** -0.5),
        'moe_b_gu': nrm((DEPTH, N_EXPERTS, 2 * D_FF), 0.01),
        'moe_w_down': nrm((DEPTH, N_EXPERTS, D_FF, D), D_FF ** -0.5 * DN_BETA),
        'moe_b_down': nrm((DEPTH, N_EXPERTS, D), 0.01),
    }


def reference(x_prompt, x_sample, state_ret, state_gla, state_s5_re, state_s5_im, c, c_ctx,
              w_mod, b_mod, ln_g, ln_b, ret_w_in, ret_decay, ret_w_out,
              gla_w_in, gla_w_a1, gla_w_a2, gla_b_a, gla_w_out,
              s5_a_re, s5_a_im, s5_log_step, s5_b_re, s5_b_im, s5_c_re, s5_c_im, s5_d, s5_w_glu,
              moe_w_router, moe_b_router, moe_w_gu, moe_b_gu, moe_w_down, moe_b_down):
    p = dict(w_mod=w_mod, b_mod=b_mod, ln_g=ln_g, ln_b=ln_b,
             ret_w_in=ret_w_in, ret_decay=ret_decay, ret_w_out=ret_w_out,
             gla_w_in=gla_w_in, gla_w_a1=gla_w_a1, gla_w_a2=gla_w_a2, gla_b_a=gla_b_a, gla_w_out=gla_w_out,
             s5_a_re=s5_a_re, s5_a_im=s5_a_im, s5_log_step=s5_log_step, s5_b_re=s5_b_re, s5_b_im=s5_b_im,
             s5_c_re=s5_c_re, s5_c_im=s5_c_im, s5_d=s5_d, s5_w_glu=s5_w_glu,
             moe_w_router=moe_w_router, moe_b_router=moe_b_router, moe_w_gu=moe_w_gu, moe_b_gu=moe_b_gu,
             moe_w_down=moe_w_down, moe_b_down=moe_b_down)
    nb = x_prompt.shape[0]
    dt = x_prompt.dtype
    z_ret = jnp.zeros((nb, N_RET, 2, RET_HEADS, RET_DK, RET_DV), dt)
    z_gla = jnp.zeros((nb, N_GLA, 2, GLA_HEADS, GLA_DK, GLA_DV), dt)
    z_s5 = jnp.zeros((nb, N_S5, 2, S5_GROUPS, S5_STATE), dt)
    y_prompt, new_ret, new_gla, new_s5_re, new_s5_im = trunk(x_prompt, c_ctx[None, :], z_ret, z_gla, z_s5, z_s5, p, False)
    y_sample, _, _, _, _ = trunk(x_sample, c, state_ret, state_gla, state_s5_re, state_s5_im, p, True)
    return (y_prompt, y_sample, new_ret, new_gla, new_s5_re, new_s5_im)
```

```python
import functools
import math

import numpy as np
import jax
import jax.numpy as jnp
from jax import lax
from jax.experimental import pallas as pl
from jax.experimental.pallas import tpu as pltpu

F32 = jnp.float32
BF16 = jnp.bfloat16

RET_HEADS = 4
GLA_HEADS = 4
GLA_TAU = 16.0
GRID_W = 64
ROPE_BASE = 10000.0
S5_GROUP = 16
TOP_K = 4
SWIGLU_ALPHA = 1.702
SWIGLU_LIMIT = 7.0
LN_EPS = 1e-5
GN_EPS = 1e-5

ROW_BLOCK = 256
GLA_CHUNK = 64
S5_CHUNK = 16
MOE_BLOCK = 256
ROUTER_PAD = 128
VMEM_LIMIT = 56 * 1024 * 1024

M_START, M_END, M_SAMPLE, M_SIDX, M_PIDX, M_POS, M_SEQ = range(7)
M_ROWS = 7

NT = (((1,), (1,)), ((), ()))
TN = (((0,), (0,)), ((), ()))


def _cparams(sem):
    return pltpu.CompilerParams(dimension_semantics=sem, vmem_limit_bytes=VMEM_LIMIT)


def _silu(x):
    return x * jax.nn.sigmoid(x)


def _dot(a, b):
    return jnp.dot(a, b, preferred_element_type=F32)


def _mod_kernel(c_ref, w_ref, b_ref, o_ref):
    c = c_ref[...]
    s = _silu(c).astype(BF16)
    o_ref[0] = _dot(s, w_ref[0].astype(BF16)) + b_ref[0]


def _mod_all(conds, w_mod, b_mod):
    depth, d, n6 = w_mod.shape
    nc = conds.shape[0]
    tn = 1536
    return pl.pallas_call(
        _mod_kernel,
        out_shape=jax.ShapeDtypeStruct((depth, nc, n6), F32),
        grid=(depth, n6 // tn),
        in_specs=[pl.BlockSpec((nc, d), lambda l, j: (0, 0)),
                  pl.BlockSpec((1, d, tn), lambda l, j: (l, 0, j)),
                  pl.BlockSpec((1, 1, tn), lambda l, j: (l, 0, j))],
        out_specs=pl.BlockSpec((1, nc, tn), lambda l, j: (l, 0, j)),
        compiler_params=_cparams(("parallel", "parallel")),
    )(conds, w_mod, b_mod.reshape(depth, 1, n6))


def _proj_kernel(x_ref, sc_ref, sh_ref, w_ref, o_ref, h_scr):
    @pl.when(pl.program_id(1) == 0)
    def _():
        h_scr[...] = (x_ref[...] * (1.0 + sc_ref[...]) + sh_ref[...]).astype(BF16)

    o_ref[...] = _dot(h_scr[...], w_ref[...]).astype(o_ref.dtype)


def _proj(x, sc, sh, w, tm, tn, out_dtype=BF16):
    n, d = x.shape
    nout = w.shape[1]
    kb = tm // ROW_BLOCK
    return pl.pallas_call(
        _proj_kernel,
        out_shape=jax.ShapeDtypeStruct((n, nout), out_dtype),
        grid=(n // tm, nout // tn),
        in_specs=[pl.BlockSpec((tm, d), lambda i, j: (i, 0)),
                  pl.BlockSpec((None, 1, d), lambda i, j: (i * kb, 0, 0)),
                  pl.BlockSpec((None, 1, d), lambda i, j: (i * kb, 0, 0)),
                  pl.BlockSpec((d, tn), lambda i, j: (0, j))],
        out_specs=pl.BlockSpec((tm, tn), lambda i, j: (i, j)),
        scratch_shapes=[pltpu.VMEM((tm, d), BF16)],
        compiler_params=_cparams(("parallel", "arbitrary")),
    )(x, sc, sh, w)


def _layer_norm(z, g, b):
    mu = jnp.mean(z, axis=-1, keepdims=True)
    zc = z - mu
    var = jnp.mean(zc * zc, axis=-1, keepdims=True)
    return zc * lax.rsqrt(var + LN_EPS) * g + b


def _top4(logits):
    tm, width = logits.shape
    col = lax.broadcasted_iota(jnp.int32, (tm, width), 1)
    colf = col.astype(F32)
    l = logits
    vals, idxs = [], []
    for _ in range(TOP_K):
        m = jnp.max(l, axis=-1, keepdims=True)
        ix = jnp.min(jnp.where(l == m, colf, float(width)), axis=-1, keepdims=True)
        vals.append(m)
        idxs.append(ix.astype(jnp.int32))
        l = jnp.where(colf == ix, -jnp.inf, l)
    es = [jnp.exp(v - vals[0]) for v in vals]
    den = es[0] + es[1] + es[2] + es[3]
    idx_t = jnp.zeros((tm, width), jnp.int32)
    gate_t = jnp.zeros((tm, width), F32)
    for k in range(TOP_K):
        idx_t = jnp.where(col == k, idxs[k], idx_t)
        gate_t = jnp.where(col == k, es[k] / den, gate_t)
    return idx_t, gate_t


def _post_sublayer(alpha, x, y, g, lng, lnb):
    return _layer_norm(alpha * x + g * y, lng, lnb)


def _mixer_out_kernel(a_ref, w_ref, x_ref, g_ref, lng_ref, lnb_ref, sc2_ref, sh2_ref,
                      wr_ref, br_ref, xo_ref, h2_ref, idx_ref, gate_ref, *, alpha):
    y = _dot(a_ref[...], w_ref[...])
    xn = _post_sublayer(alpha, x_ref[...], y, g_ref[...], lng_ref[...], lnb_ref[...])
    xo_ref[...] = xn
    h2 = (xn * (1.0 + sc2_ref[...]) + sh2_ref[...]).astype(BF16)
    h2_ref[...] = h2
    idx_t, gate_t = _top4(_dot(h2, wr_ref[...]) + br_ref[...])
    idx_ref[...] = idx_t
    gate_ref[...] = gate_t


def _glu_out_kernel(y_ref, w1_ref, w2_ref, d_ref, sc1_ref, sh1_ref, x_ref, g_ref, lng_ref, lnb_ref,
                    sc2_ref, sh2_ref, wr_ref, br_ref, xo_ref, h2_ref, idx_ref, gate_ref, *, alpha):
    x = x_ref[...]
    h = x * (1.0 + sc1_ref[...]) + sh1_ref[...]
    a = jax.nn.gelu(y_ref[...] + d_ref[...] * h).astype(BF16)
    y = _dot(a, w1_ref[...]) * jax.nn.sigmoid(_dot(a, w2_ref[...]))
    xn = _post_sublayer(alpha, x, y, g_ref[...], lng_ref[...], lnb_ref[...])
    xo_ref[...] = xn
    h2 = (xn * (1.0 + sc2_ref[...]) + sh2_ref[...]).astype(BF16)
    h2_ref[...] = h2
    idx_t, gate_t = _top4(_dot(h2, wr_ref[...]) + br_ref[...])
    idx_ref[...] = idx_t
    gate_ref[...] = gate_t


def _row_spec(tm, width):
    return pl.BlockSpec((tm, width), lambda i: (i, 0))


def _blk_vec_spec(tm, d):
    kb = tm // ROW_BLOCK
    return pl.BlockSpec((None, 1, d), lambda i: (i * kb, 0, 0))


def _const_spec(shape):
    return pl.BlockSpec(shape, lambda i: tuple(0 for _ in shape))


def _post_out_shapes(n, d):
    return (jax.ShapeDtypeStruct((n, d), F32), jax.ShapeDtypeStruct((n, d), BF16),
            jax.ShapeDtypeStruct((n, ROUTER_PAD), jnp.int32), jax.ShapeDtypeStruct((n, ROUTER_PAD), F32))


def _post_out_specs(tm, d):
    return (_row_spec(tm, d), _row_spec(tm, d), _row_spec(tm, ROUTER_PAD), _row_spec(tm, ROUTER_PAD))


def _mixer_out(a, w, x, g1, lng, lnb, sc2, sh2, wr, br, alpha, tm=512):
    n, d = x.shape
    kin = a.shape[1]
    return pl.pallas_call(
        functools.partial(_mixer_out_kernel, alpha=alpha),
        out_shape=_post_out_shapes(n, d),
        grid=(n // tm,),
        in_specs=[_row_spec(tm, kin), _const_spec((kin, d)), _row_spec(tm, d), _blk_vec_spec(tm, d),
                  _const_spec((1, d)), _const_spec((1, d)), _blk_vec_spec(tm, d), _blk_vec_spec(tm, d),
                  _const_spec((d, ROUTER_PAD)), _const_spec((1, ROUTER_PAD))],
        out_specs=_post_out_specs(tm, d),
        compiler_params=_cparams(("parallel",)),
    )(a, w, x, g1, lng, lnb, sc2, sh2, wr, br)


def _glu_out(y, w1, w2, dskip, sc1, sh1, x, g1, lng, lnb, sc2, sh2, wr, br, alpha, tm=512):
    n, d = x.shape
    return pl.pallas_call(
        functools.partial(_glu_out_kernel, alpha=alpha),
        out_shape=_post_out_shapes(n, d),
        grid=(n // tm,),
        in_specs=[_row_spec(tm, d), _const_spec((d, d)), _const_spec((d, d)), _const_spec((1, d)),
                  _blk_vec_spec(tm, d), _blk_vec_spec(tm, d), _row_spec(tm, d), _blk_vec_spec(tm, d),
                  _const_spec((1, d)), _const_spec((1, d)), _blk_vec_spec(tm, d), _blk_vec_spec(tm, d),
                  _const_spec((d, ROUTER_PAD)), _const_spec((1, ROUTER_PAD))],
        out_specs=_post_out_specs(tm, d),
        compiler_params=_cparams(("parallel",)),
    )(y, w1, w2, dskip, sc1, sh1, x, g1, lng, lnb, sc2, sh2, wr, br)


def _res_ln_kernel(x_ref, y_ref, g_ref, lng_ref, lnb_ref, xo_ref, *, alpha):
    xo_ref[...] = _post_sublayer(alpha, x_ref[...], y_ref[...], g_ref[...], lng_ref[...], lnb_ref[...])


def _res_ln(x, y, g2, lng, lnb, alpha, tm=512):
    n, d = x.shape
    return pl.pallas_call(
        functools.partial(_res_ln_kernel, alpha=alpha),
        out_shape=jax.ShapeDtypeStruct((n, d), F32),
        grid=(n // tm,),
        in_specs=[_row_spec(tm, d), _row_spec(tm, d), _blk_vec_spec(tm, d), _const_spec((1, d)),
                  _const_spec((1, d))],
        out_specs=_row_spec(tm, d),
        compiler_params=_cparams(("parallel",)),
    )(x, y, g2, lng, lnb)


def _rotary(x, cos, sin):
    half = x.shape[-1] // 2
    x1, x2 = x[:, :half], x[:, half:]
    return jnp.concatenate([x1 * cos - x2 * sin, x1 * sin + x2 * cos], axis=-1)


def _head_norm(o):
    mu = jnp.mean(o, axis=-1, keepdims=True)
    oc = o - mu
    var = jnp.mean(oc * oc, axis=-1, keepdims=True)
    return oc * lax.rsqrt(var + GN_EPS)


def _ret_fwd_kernel(meta_ref, lg_ref, q_ref, k_ref, v_ref, cos_ref, sin_ref, s0_ref,
                    o_ref, st_ref, s_scr, *, nb, heads):
    h = pl.program_id(0)
    i = pl.program_id(1)
    start = meta_ref[M_START * nb + i]
    end = meta_ref[M_END * nb + i]
    sample = meta_ref[M_SAMPLE * nb + i]
    c = q_ref.shape[0]
    dk = q_ref.shape[1]

    @pl.when(jnp.logical_and(start == 1, sample == 1))
    def _():
        s_scr[...] = s0_ref[...]

    @pl.when(jnp.logical_and(start == 1, sample == 0))
    def _():
        s_scr[...] = jnp.zeros_like(s_scr)

    lgf = lg_ref[h]
    lgb = lg_ref[heads + h]
    cos = cos_ref[...]
    sin = sin_ref[...]
    q = _rotary(q_ref[...].astype(F32), cos, sin) * (dk ** -0.5)
    k = _rotary(k_ref[...].astype(F32), cos, sin)
    qb = q.astype(BF16)
    v = v_ref[...]

    rn = lax.broadcasted_iota(jnp.int32, (c, c), 0)
    rm = lax.broadcasted_iota(jnp.int32, (c, c), 1)
    rel = (rn - rm).astype(F32)
    dmat = (jnp.where(rel >= 0, jnp.exp(lgf * jnp.maximum(rel, 0.0)), 0.0)
            + jnp.where(rel <= 0, jnp.exp(lgb * jnp.maximum(-rel, 0.0)), 0.0))
    scores = lax.dot_general(qb, k.astype(BF16), NT, preferred_element_type=F32) * dmat
    idx = lax.broadcasted_iota(jnp.int32, (c, 1), 0).astype(F32)
    xi = jnp.exp(lgf * (idx + 1.0))
    zeta = jnp.exp(lgf * (c - 1.0 - idx))
    s_prev = s_scr[...]
    o_ref[...] = _dot(scores.astype(BF16), v) + _dot(qb, s_prev.astype(BF16)) * xi
    g_chunk = jnp.exp(lgf * jnp.full((1, 1), float(c), F32))
    s_new = s_prev * g_chunk + lax.dot_general((k * zeta).astype(BF16), v, TN, preferred_element_type=F32)
    s_scr[...] = s_new

    @pl.when(jnp.logical_and(end == 1, sample == 0))
    def _():
        st_ref[...] = s_new


def _ret_bwd_kernel(meta_ref, lg_ref, q_ref, k_ref, v_ref, g_ref, cos_ref, sin_ref, s0_ref, of_ref,
                    o_ref, st_ref, s_scr, *, nb, heads):
    h = pl.program_id(0)
    i = nb - 1 - pl.program_id(1)
    start = meta_ref[M_START * nb + i]
    end = meta_ref[M_END * nb + i]
    sample = meta_ref[M_SAMPLE * nb + i]
    c = q_ref.shape[0]
    dk = q_ref.shape[1]

    @pl.when(jnp.logical_and(end == 1, sample == 1))
    def _():
        s_scr[...] = s0_ref[...]

    @pl.when(jnp.logical_and(end == 1, sample == 0))
    def _():
        s_scr[...] = jnp.zeros_like(s_scr)

    lgb = lg_ref[heads + h]
    cos = cos_ref[...]
    sin = sin_ref[...]
    q = _rotary(q_ref[...].astype(F32), cos, sin) * (dk ** -0.5)
    k = _rotary(k_ref[...].astype(F32), cos, sin)
    v = v_ref[...]
    idx = lax.broadcasted_iota(jnp.int32, (c, 1), 0).astype(F32)
    xi = jnp.exp(lgb * (c - idx))
    zeta = jnp.exp(lgb * idx)
    s_prev = s_scr[...]
    o = of_ref[...] + _dot(q.astype(BF16), s_prev.astype(BF16)) * xi
    g_chunk = jnp.exp(lgb * jnp.full((1, 1), float(c), F32))
    s_new = s_prev * g_chunk + lax.dot_general((k * zeta).astype(BF16), v, TN, preferred_element_type=F32)
    s_scr[...] = s_new
    o_ref[...] = (_silu(g_ref[...].astype(F32)) * _head_norm(o)).astype(o_ref.dtype)

    @pl.when(jnp.logical_and(start == 1, sample == 0))
    def _():
        st_ref[...] = s_new


def _retention(qkvg, meta, lg, cos_t, sin_t, state, li, nb, n_prompt):
    n = qkvg.shape[0]
    heads = RET_HEADS
    d = qkvg.shape[1] // 6
    dk, dv = d // heads, 2 * d // heads
    c = ROW_BLOCK
    kw = dict(nb=nb, heads=heads)
    s0_spec = lambda d_: pl.BlockSpec(
        (None, None, None, None, dk, dv), lambda h, j, m: (m[M_SIDX * nb + (j if d_ == 0 else nb - 1 - j)], li, d_, h, 0, 0))
    pos = lambda j, m: m[M_POS * nb + j]
    of, st_f = pl.pallas_call(
        functools.partial(_ret_fwd_kernel, **kw),
        out_shape=(jax.ShapeDtypeStruct((n, heads * dv), F32), jax.ShapeDtypeStruct((n_prompt, heads, dk, dv), F32)),
        grid_spec=pltpu.PrefetchScalarGridSpec(
            num_scalar_prefetch=1, grid=(heads, nb),
            in_specs=[pl.BlockSpec(memory_space=pltpu.SMEM),
                      pl.BlockSpec((c, dk), lambda h, j, m: (j, h)),
                      pl.BlockSpec((c, dk), lambda h, j, m: (j, heads + h)),
                      pl.BlockSpec((c, dv), lambda h, j, m: (j, heads + h)),
                      pl.BlockSpec((c, dk // 2), lambda h, j, m: (pos(j, m), 0)),
                      pl.BlockSpec((c, dk // 2), lambda h, j, m: (pos(j, m), 0)),
                      s0_spec(0)],
            out_specs=(pl.BlockSpec((c, dv), lambda h, j, m: (j, h)),
                       pl.BlockSpec((None, None, dk, dv), lambda h, j, m: (m[M_PIDX * nb + j], h, 0, 0))),
            scratch_shapes=[pltpu.VMEM((dk, dv), F32)]),
        compiler_params=_cparams(("parallel", "arbitrary")),
    )(meta, lg, qkvg, qkvg, qkvg, cos_t, sin_t, state)
    r = lambda j: nb - 1 - j
    out, st_b = pl.pallas_call(
        functools.partial(_ret_bwd_kernel, **kw),
        out_shape=(jax.ShapeDtypeStruct((n, heads * dv), BF16), jax.ShapeDtypeStruct((n_prompt, heads, dk, dv), F32)),
        grid_spec=pltpu.PrefetchScalarGridSpec(
            num_scalar_prefetch=1, grid=(heads, nb),
            in_specs=[pl.BlockSpec(memory_space=pltpu.SMEM),
                      pl.BlockSpec((c, dk), lambda h, j, m: (r(j), h)),
                      pl.BlockSpec((c, dk), lambda h, j, m: (r(j), heads + h)),
                      pl.BlockSpec((c, dv), lambda h, j, m: (r(j), heads + h)),
                      pl.BlockSpec((c, dv), lambda h, j, m: (r(j), 2 * heads + h)),
                      pl.BlockSpec((c, dk // 2), lambda h, j, m: (pos(r(j), m), 0)),
                      pl.BlockSpec((c, dk // 2), lambda h, j, m: (pos(r(j), m), 0)),
                      s0_spec(1),
                      pl.BlockSpec((c, dv), lambda h, j, m: (r(j), h))],
            out_specs=(pl.BlockSpec((c, dv), lambda h, j, m: (r(j), h)),
                       pl.BlockSpec((None, None, dk, dv), lambda h, j, m: (m[M_PIDX * nb + r(j)], h, 0, 0))),
            scratch_shapes=[pltpu.VMEM((dk, dv), F32)]),
        compiler_params=_cparams(("parallel", "arbitrary")),
    )(meta, lg, qkvg, qkvg, qkvg, qkvg, cos_t, sin_t, state, of)
    return out, jnp.stack([st_f, st_b], axis=1)


def _gla_gate_kernel(x_ref, sc_ref, sh_ref, w1_ref, w2_ref, b_ref, o_ref):
    h = (x_ref[...] * (1.0 + sc_ref[...]) + sh_ref[...]).astype(BF16)
    u = _dot(h, w1_ref[...]).astype(BF16)
    z = _dot(u, w2_ref[...]) + b_ref[...]
    log_sig = jnp.minimum(z, 0.0) - jnp.log(1.0 + jnp.exp(-jnp.abs(z)))
    o_ref[...] = log_sig / GLA_TAU


def _gla_gate(x, sc, sh, w1, w2, b, tm=512):
    n, d = x.shape
    r2 = w1.shape[1]
    nk = w2.shape[1]
    return pl.pallas_call(
        _gla_gate_kernel,
        out_shape=jax.ShapeDtypeStruct((n, nk), F32),
        grid=(n // tm,),
        in_specs=[_row_spec(tm, d), _blk_vec_spec(tm, d), _blk_vec_spec(tm, d), _const_spec((d, r2)),
                  _const_spec((r2, nk)), _const_spec((1, nk))],
        out_specs=_row_spec(tm, nk),
        compiler_params=_cparams(("parallel",)),
    )(x, sc, sh, w1, w2, b)


def _cumsum_rows(tri, x):
    hi = x.astype(BF16)
    r1 = x - hi.astype(F32)
    mid = r1.astype(BF16)
    lo = (r1 - mid.astype(F32)).astype(BF16)
    return _dot(tri, hi) + _dot(tri, mid) + _dot(tri, lo)


def _gla_chunk(q, k, v, la, s_t, tri, mask, mid_row, last_row):
    b = _cumsum_rows(tri, la)
    r = b[mid_row:mid_row + 1, :]
    bl = b[last_row:last_row + 1, :]
    qe = (q * jnp.exp(b - r)).astype(BF16)
    ke = (k * jnp.exp(r - b)).astype(BF16)
    qi = (q * jnp.exp(b)).astype(BF16)
    kd = (k * jnp.exp(bl - b)).astype(BF16)
    sc = lax.dot_general(qe, ke, NT, preferred_element_type=F32)
    sc = jnp.where(mask, sc, 0.0).astype(BF16)
    o = _dot(sc, v) + lax.dot_general(qi, s_t.astype(BF16), NT, preferred_element_type=F32)
    s_new = s_t * jnp.exp(bl) + lax.dot_general(v, kd, TN, preferred_element_type=F32)
    return o, s_new


def _gla_fwd_kernel(meta_ref, q_ref, k_ref, v_ref, la_ref, s0_ref, o_ref, st_ref, s_scr, *, nb):
    i = pl.program_id(1)
    start = meta_ref[M_START * nb + i]
    end = meta_ref[M_END * nb + i]
    sample = meta_ref[M_SAMPLE * nb + i]
    dk = q_ref.shape[1]
    cc = GLA_CHUNK

    @pl.when(jnp.logical_and(start == 1, sample == 1))
    def _():
        s_scr[...] = s0_ref[...].T

    @pl.when(jnp.logical_and(start == 1, sample == 0))
    def _():
        s_scr[...] = jnp.zeros_like(s_scr)

    rn = lax.broadcasted_iota(jnp.int32, (cc, cc), 0)
    rm = lax.broadcasted_iota(jnp.int32, (cc, cc), 1)
    mask = rn >= rm
    tri = jnp.where(mask, 1.0, 0.0).astype(BF16)
    s_t = s_scr[...]
    for ci in range(q_ref.shape[0] // cc):
        sl = slice(ci * cc, (ci + 1) * cc)
        q = q_ref[sl, :].astype(F32) * (dk ** -0.5)
        o, s_t = _gla_chunk(q, k_ref[sl, :].astype(F32), v_ref[sl, :], la_ref[sl, :], s_t, tri, mask,
                            cc // 2 - 1, cc - 1)
        o_ref[sl, :] = o
    s_scr[...] = s_t

    @pl.when(jnp.logical_and(end == 1, sample == 0))
    def _():
        st_ref[...] = s_t.T


def _gla_bwd_kernel(meta_ref, q_ref, k_ref, v_ref, r_ref, la_ref, s0_ref, of_ref, o_ref, st_ref, s_scr, *, nb):
    i = nb - 1 - pl.program_id(1)
    start = meta_ref[M_START * nb + i]
    end = meta_ref[M_END * nb + i]
    sample = meta_ref[M_SAMPLE * nb + i]
    dk = q_ref.shape[1]
    cc = GLA_CHUNK

    @pl.when(jnp.logical_and(end == 1, sample == 1))
    def _():
        s_scr[...] = s0_ref[...].T

    @pl.when(jnp.logical_and(end == 1, sample == 0))
    def _():
        s_scr[...] = jnp.zeros_like(s_scr)

    rn = lax.broadcasted_iota(jnp.int32, (cc, cc), 0)
    rm = lax.broadcasted_iota(jnp.int32, (cc, cc), 1)
    mask = rn <= rm
    tri = jnp.where(mask, 1.0, 0.0).astype(BF16)
    s_t = s_scr[...]
    for ci in reversed(range(q_ref.shape[0] // cc)):
        sl = slice(ci * cc, (ci + 1) * cc)
        q = q_ref[sl, :].astype(F32) * (dk ** -0.5)
        ob, s_t = _gla_chunk(q, k_ref[sl, :].astype(F32), v_ref[sl, :], la_ref[sl, :], s_t, tri, mask,
                             cc // 2, 0)
        o = of_ref[sl, :] + ob
        o_ref[sl, :] = (_silu(r_ref[sl, :].astype(F32)) * _head_norm(o)).astype(o_ref.dtype)
    s_scr[...] = s_t

    @pl.when(jnp.logical_and(start == 1, sample == 0))
    def _():
        st_ref[...] = s_t.T


def _gla(qkvr, la, meta, state, li, nb, n_prompt):
    n = qkvr.shape[0]
    heads = GLA_HEADS
    d = qkvr.shape[1] // 3
    dk, dv = d // 2 // heads, d // heads
    c = ROW_BLOCK
    s0_spec = lambda d_: pl.BlockSpec(
        (None, None, None, None, dk, dv), lambda h, j, m: (m[M_SIDX * nb + (j if d_ == 0 else nb - 1 - j)], li, d_, h, 0, 0))
    of, st_f = pl.pallas_call(
        functools.partial(_gla_fwd_kernel, nb=nb),
        out_shape=(jax.ShapeDtypeStruct((n, d), F32), jax.ShapeDtypeStruct((n_prompt, heads, dk, dv), F32)),
        grid_spec=pltpu.PrefetchScalarGridSpec(
            num_scalar_prefetch=1, grid=(heads, nb),
            in_specs=[pl.BlockSpec((c, dk), lambda h, j, m: (j, h)),
                      pl.BlockSpec((c, dk), lambda h, j, m: (j, heads + h)),
                      pl.BlockSpec((c, dv), lambda h, j, m: (j, heads + h)),
                      pl.BlockSpec((c, dk), lambda h, j, m: (j, h)),
                      s0_spec(0)],
            out_specs=(pl.BlockSpec((c, dv), lambda h, j, m: (j, h)),
                       pl.BlockSpec((None, None, dk, dv), lambda h, j, m: (m[M_PIDX * nb + j], h, 0, 0))),
            scratch_shapes=[pltpu.VMEM((dv, dk), F32)]),
        compiler_params=_cparams(("parallel", "arbitrary")),
    )(meta, qkvr, qkvr, qkvr, la, state)
    r = lambda j: nb - 1 - j
    out, st_b = pl.pallas_call(
        functools.partial(_gla_bwd_kernel, nb=nb),
        out_shape=(jax.ShapeDtypeStruct((n, d), BF16), jax.ShapeDtypeStruct((n_prompt, heads, dk, dv), F32)),
        grid_spec=pltpu.PrefetchScalarGridSpec(
            num_scalar_prefetch=1, grid=(heads, nb),
            in_specs=[pl.BlockSpec((c, dk), lambda h, j, m: (r(j), h)),
                      pl.BlockSpec((c, dk), lambda h, j, m: (r(j), heads + h)),
                      pl.BlockSpec((c, dv), lambda h, j, m: (r(j), heads + h)),
                      pl.BlockSpec((c, dv), lambda h, j, m: (r(j), 2 * heads + h)),
                      pl.BlockSpec((c, dk), lambda h, j, m: (r(j), heads + h)),
                      s0_spec(1),
                      pl.BlockSpec((c, dv), lambda h, j, m: (r(j), h))],
            out_specs=(pl.BlockSpec((c, dv), lambda h, j, m: (r(j), h)),
                       pl.BlockSpec((None, None, dk, dv), lambda h, j, m: (m[M_PIDX * nb + r(j)], h, 0, 0))),
            scratch_shapes=[pltpu.VMEM((dv, dk), F32)]),
        compiler_params=_cparams(("parallel", "arbitrary")),
    )(meta, qkvr, qkvr, qkvr, qkvr, la, state, of)
    return out, jnp.stack([st_f, st_b], axis=1)


def _modulate_kernel(x_ref, sc_ref, sh_ref, o_ref):
    o_ref[...] = (x_ref[...] * (1.0 + sc_ref[...]) + sh_ref[...]).astype(o_ref.dtype)


def _modulate(x, sc, sh, tm=512):
    n, d = x.shape
    return pl.pallas_call(
        _modulate_kernel,
        out_shape=jax.ShapeDtypeStruct((n, d), BF16),
        grid=(n // tm,),
        in_specs=[_row_spec(tm, d), _blk_vec_spec(tm, d), _blk_vec_spec(tm, d)],
        out_specs=_row_spec(tm, d),
        compiler_params=_cparams(("parallel",)),
    )(x, sc, sh)


def _s5_in_kernel(u_ref, w_ref, wf_ref, wb_ref):
    z = _dot(u_ref[...], w_ref[...])
    half = z.shape[1] // 2
    wf_ref[...] = z[:, :half]
    wb_ref[...] = z[:, half:]


def _s5_in(u, w_in, tr):
    g, r, kc = u.shape
    p2 = w_in.shape[2] // 2
    return pl.pallas_call(
        _s5_in_kernel,
        out_shape=(jax.ShapeDtypeStruct((r, g * p2), F32), jax.ShapeDtypeStruct((r, g * p2), F32)),
        grid=(g, r // tr),
        in_specs=[pl.BlockSpec((None, tr, kc), lambda gi, ri: (gi, ri, 0)),
                  pl.BlockSpec((None, kc, 2 * p2), lambda gi, ri: (gi, 0, 0))],
        out_specs=(pl.BlockSpec((tr, p2), lambda gi, ri: (ri, gi)),
                   pl.BlockSpec((tr, p2), lambda gi, ri: (ri, gi))),
        compiler_params=_cparams(("parallel", "parallel")),
    )(u, w_in)


def _s5_scan_kernel(meta_ref, w_ref, a1_ref, a2_ref, s0_ref, xp_ref, st_ref, x_scr, *, nb, reverse):
    j = pl.program_id(0)
    i = nb - 1 - j if reverse else j
    first = meta_ref[(M_END if reverse else M_START) * nb + i]
    last = meta_ref[(M_START if reverse else M_END) * nb + i]

    @pl.when(first == 1)
    def _():
        x_scr[...] = s0_ref[...]

    a1 = a1_ref[...]
    a2 = a2_ref[...]
    x = x_scr[...]
    half = x.shape[1] // 2
    nrow = w_ref.shape[0]
    order = range(nrow - 1, -1, -1) if reverse else range(nrow)
    for c in order:
        xp_ref[c] = x
        x = x * a1 + pltpu.roll(x, half, axis=1) * a2 + w_ref[c]
    x_scr[...] = x

    @pl.when(last == 1)
    def _():
        st_ref[...] = x


def _s5_scan(w, a1, a2, s0, meta, nb, nseq, reverse):
    r, g, p2 = w.shape
    cr = r // nb
    blk = (lambda j: nb - 1 - j) if reverse else (lambda j: j)
    return pl.pallas_call(
        functools.partial(_s5_scan_kernel, nb=nb, reverse=reverse),
        out_shape=(jax.ShapeDtypeStruct((r, g, p2), F32), jax.ShapeDtypeStruct((nseq, g, p2), F32)),
        grid_spec=pltpu.PrefetchScalarGridSpec(
            num_scalar_prefetch=1, grid=(nb,),
            in_specs=[pl.BlockSpec((cr, g, p2), lambda j, m: (blk(j), 0, 0)),
                      pl.BlockSpec((g, p2), lambda j, m: (0, 0)),
                      pl.BlockSpec((g, p2), lambda j, m: (0, 0)),
                      pl.BlockSpec((None, g, p2), lambda j, m: (m[M_SEQ * nb + blk(j)], 0, 0))],
            out_specs=(pl.BlockSpec((cr, g, p2), lambda j, m: (blk(j), 0, 0)),
                       pl.BlockSpec((None, g, p2), lambda j, m: (m[M_SEQ * nb + blk(j)], 0, 0))),
            scratch_shapes=[pltpu.VMEM((g, p2), F32)]),
        compiler_params=_cparams(("arbitrary",)),
    )(meta, w, a1, a2, s0)


def _s5_out_kernel(u_ref, m_ref, xf_ref, cf_ref, xb_ref, cb_ref, y_ref):
    y_ref[...] = (_dot(u_ref[...], m_ref[...]) + _dot(xf_ref[...].astype(BF16), cf_ref[...])
                  + _dot(xb_ref[...].astype(BF16), cb_ref[...]))


def _s5_out(u, m_intra, xf, cf, xb, cb, tr):
    g, r, kc = u.shape
    p2 = cf.shape[1]
    return pl.pallas_call(
        _s5_out_kernel,
        out_shape=jax.ShapeDtypeStruct((g, r, kc), F32),
        grid=(g, r // tr),
        in_specs=[pl.BlockSpec((None, tr, kc), lambda gi, ri: (gi, ri, 0)),
                  pl.BlockSpec((None, kc, kc), lambda gi, ri: (gi, 0, 0)),
                  pl.BlockSpec((tr, p2), lambda gi, ri: (ri, gi)),
                  pl.BlockSpec((None, p2, kc), lambda gi, ri: (gi, 0, 0)),
                  pl.BlockSpec((tr, p2), lambda gi, ri: (ri, gi)),
                  pl.BlockSpec((None, p2, kc), lambda gi, ri: (gi, 0, 0))],
        out_specs=pl.BlockSpec((None, tr, kc), lambda gi, ri: (gi, ri, 0)),
        compiler_params=_cparams(("parallel", "parallel")),
    )(u, m_intra, xf, cf, xb, cb)


def _s5_tables(a_re, a_im, log_step, b_re, b_im, c_re, c_im):
    t = S5_CHUNK
    step = jnp.exp(log_step.astype(F32))[..., None]
    ar, ai = a_re.astype(F32), a_im.astype(F32)
    mag = jnp.exp(ar * step)
    ab_re, ab_im = mag * jnp.cos(ai * step), mag * jnp.sin(ai * step)
    den = ar * ar + ai * ai
    f_re = ((ab_re - 1.0) * ar + ab_im * ai) / den
    f_im = (ab_im * ar - (ab_re - 1.0) * ai) / den
    br, bi = b_re.astype(F32), b_im.astype(F32)
    bb_re = f_re[..., None] * br - f_im[..., None] * bi
    bb_im = f_re[..., None] * bi + f_im[..., None] * br
    cr, ci = c_re.astype(F32), c_im.astype(F32)
    tau = jnp.arange(t + 1, dtype=F32)[:, None, None, None]
    pmag = jnp.exp(ar[None] * step[None] * tau)
    pw_re = pmag * jnp.cos(ai[None] * step[None] * tau)
    pw_im = pmag * jnp.sin(ai[None] * step[None] * tau)
    g, p, ch = br.shape[1], br.shape[2], br.shape[3]

    def cmul(xr, xi, yr, yi):
        return xr * yr - xi * yi, xr * yi + xi * yr

    def inject(d, powers):
        pr = pw_re[powers, d][:, :, :, None]
        pi = pw_im[powers, d][:, :, :, None]
        wr, wi = cmul(pr, pi, bb_re[d][None], bb_im[d][None])
        w = jnp.concatenate([wr, wi], axis=2)
        return w.transpose(1, 0, 3, 2).reshape(g, t * ch, 2 * p)
    w_in = jnp.concatenate([inject(0, jnp.arange(t - 1, -1, -1)), inject(1, jnp.arange(t))], axis=-1)

    def readout(d, powers):
        pr = pw_re[powers, d][:, :, None, :]
        pi = pw_im[powers, d][:, :, None, :]
        kr, ki = cmul(cr[d][None], ci[d][None], pr, pi)
        m = jnp.concatenate([kr, -ki], axis=3)
        return m.transpose(1, 3, 0, 2).reshape(g, 2 * p, t * ch)
    c_f = readout(0, jnp.arange(1, t + 1))
    c_b = readout(1, jnp.arange(t, 0, -1))

    def lag_kernels(d):
        pr = pw_re[:t, d][:, :, :, None]
        pi = pw_im[:t, d][:, :, :, None]
        wr, wi = cmul(pr, pi, bb_re[d][None], bb_im[d][None])
        return (jnp.einsum('gop,tgpc->tgoc', cr[d], wr) - jnp.einsum('gop,tgpc->tgoc', ci[d], wi))
    kf, kb = lag_kernels(0), lag_kernels(1)
    ii = jnp.arange(t)
    lag = ii[None, :] - ii[:, None]
    mf = jnp.where((lag >= 0)[:, :, None, None, None], kf[jnp.clip(lag, 0, t - 1)], 0.0)
    mb = jnp.where((lag <= 0)[:, :, None, None, None], kb[jnp.clip(-lag, 0, t - 1)], 0.0)
    m = (mf + mb).transpose(2, 0, 4, 1, 3).reshape(g, t * ch, t * ch)

    at_re, at_im = pw_re[t], pw_im[t]
    a1 = jnp.concatenate([at_re, at_re], axis=-1)
    a2 = jnp.concatenate([-at_im, at_im], axis=-1)
    return w_in.astype(BF16), m.astype(BF16), c_f.astype(BF16), c_b.astype(BF16), a1, a2


def _ffn_kernel(be_ref, x_ref, wgu_ref, bgu_ref, wd_ref, bd_ref, o_ref):
    dff = wd_ref.shape[0]
    gu = _dot(x_ref[...], wgu_ref[...]) + bgu_ref[...]
    gate = jnp.minimum(gu[:, :dff], SWIGLU_LIMIT)
    lin = jnp.clip(gu[:, dff:], -SWIGLU_LIMIT, SWIGLU_LIMIT)
    act = gate * jax.nn.sigmoid(SWIGLU_ALPHA * gate) * (lin + 1.0)
    o_ref[...] = _dot(act.astype(BF16), wd_ref[...]) + bd_ref[...]


def _ffn(block_e, xb, wgu, bgu, wd, bd):
    n_rows, d = xb.shape
    bm = MOE_BLOCK
    dff = wd.shape[1]
    return pl.pallas_call(
        _ffn_kernel,
        out_shape=jax.ShapeDtypeStruct((n_rows, d), F32),
        grid_spec=pltpu.PrefetchScalarGridSpec(
            num_scalar_prefetch=1, grid=(n_rows // bm,),
            in_specs=[pl.BlockSpec((bm, d), lambda b, be: (b, 0)),
                      pl.BlockSpec((None, d, 2 * dff), lambda b, be: (be[b], 0, 0)),
                      pl.BlockSpec((None, 1, 2 * dff), lambda b, be: (be[b], 0, 0)),
                      pl.BlockSpec((None, dff, d), lambda b, be: (be[b], 0, 0)),
                      pl.BlockSpec((None, 1, d), lambda b, be: (be[b], 0, 0))],
            out_specs=pl.BlockSpec((bm, d), lambda b, be: (b, 0))),
        compiler_params=_cparams(("arbitrary",)),
    )(block_e, xb, wgu, bgu, wd, bd)


def _moe(h2, top_idx, gates, wgu, bgu, wd, bd):
    n_tok, d = h2.shape
    n_exp = wgu.shape[0]
    bm = MOE_BLOCK
    n_assign = n_tok * TOP_K
    flat_e = top_idx.reshape(-1)
    flat_tok = jnp.arange(n_assign, dtype=jnp.int32) // TOP_K
    order = jnp.argsort(flat_e)
    sorted_e = flat_e[order]
    counts = jnp.bincount(flat_e, length=n_exp)
    padded = (counts + bm - 1) // bm * bm
    start = jnp.cumsum(counts) - counts
    pend = jnp.cumsum(padded)
    pstart = pend - padded
    dest = pstart[sorted_e] + jnp.arange(n_assign) - start[sorted_e]
    n_blocks = -(-n_assign // bm) + n_exp
    n_rows = n_blocks * bm
    row_tok = jnp.zeros((n_rows,), jnp.int32).at[dest].set(flat_tok[order])
    row_gate = jnp.zeros((n_rows,), F32).at[dest].set(gates.reshape(-1)[order])
    row_seg = jnp.full((n_rows,), n_tok, jnp.int32).at[dest].set(flat_tok[order])
    block_e = jnp.minimum(jnp.searchsorted(pend, jnp.arange(n_blocks) * bm, side='right'), n_exp - 1).astype(jnp.int32)
    xb = h2[row_tok]
    yb = _ffn(block_e, xb, wgu, bgu, wd, bd)
    return jax.ops.segment_sum(yb * row_gate[:, None], row_seg, num_segments=n_tok + 1)[:n_tok]


def _block_meta(n_p, l_p, n_s, l_s):
    rb = ROW_BLOCK
    assert l_p % rb == 0 and l_s % rb == 0
    rows = []
    for s in range(n_p):
        for j in range(l_p // rb):
            rows.append((int(j == 0), int(j == l_p // rb - 1), 0, 0, s, 0, s))
    for s in range(n_s):
        for j in range(l_s // rb):
            rows.append((int(j == 0), int(j == l_s // rb - 1), 1, s, n_p - 1, 1 + j, n_p + s))
    meta = np.asarray(rows, np.int32).T
    cond = np.asarray([0] * (n_p * l_p // rb) + [1 + s for s in range(n_s) for _ in range(l_s // rb)], np.int32)
    return meta, cond


def _rotary_tables(l_s, dk):
    rows = l_s // GRID_W
    row = jnp.repeat(jnp.arange(rows, dtype=F32), GRID_W)
    col = jnp.tile(jnp.arange(GRID_W, dtype=F32), rows)
    n_freq = dk // 4
    inv = ROPE_BASE ** (-jnp.arange(n_freq, dtype=F32) / n_freq)
    ang = jnp.concatenate([row[:, None] * inv, col[:, None] * inv], axis=-1)
    ident = jnp.zeros((ROW_BLOCK, dk // 2), F32)
    return (jnp.concatenate([ident + 1.0, jnp.cos(ang)], axis=0),
            jnp.concatenate([ident, jnp.sin(ang)], axis=0))


def kernel(x_prompt, x_sample, state_ret, state_gla, state_s5_re, state_s5_im, c, c_ctx, w_mod, b_mod, ln_g, ln_b, ret_w_in, ret_decay, ret_w_out, gla_w_in, gla_w_a1, gla_w_a2, gla_b_a, gla_w_out, s5_a_re, s5_a_im, s5_log_step, s5_b_re, s5_b_im, s5_c_re, s5_c_im, s5_d, s5_w_glu, moe_w_router, moe_b_router, moe_w_gu, moe_b_gu, moe_w_down, moe_b_down):
    n_p, l_p, d = x_prompt.shape
    n_s, l_s, _ = x_sample.shape
    depth = w_mod.shape[0]
    n_exp = moe_w_router.shape[-1]
    alpha = (2 * depth) ** 0.25
    rb = ROW_BLOCK
    n = n_p * l_p + n_s * l_s
    nb = n // rb
    nseq = n_p + n_s
    meta_np, cond_np = _block_meta(n_p, l_p, n_s, l_s)
    meta = jnp.asarray(meta_np.reshape(-1))

    x = jnp.concatenate([x_prompt.reshape(n_p * l_p, d), x_sample.reshape(n_s * l_s, d)], axis=0)

    ncond = 16
    conds = jnp.zeros((ncond, d), F32).at[0].set(c_ctx).at[1:1 + n_s].set(c)
    mod = _mod_all(conds, w_mod, b_mod)
    mod_b = mod[:, cond_np].reshape(depth, nb, 6, 1, d)

    cos_t, sin_t = _rotary_tables(l_s, d // RET_HEADS)
    lg_ret = jax.nn.log_sigmoid(ret_decay.astype(F32))

    wr_pad = jnp.zeros((depth, d, ROUTER_PAD), BF16).at[:, :, :n_exp].set(moe_w_router.astype(BF16))
    br_pad = jnp.full((depth, 1, ROUTER_PAD), -1e30, F32).at[:, 0, :n_exp].set(moe_b_router)

    new_ret, new_gla, new_s5_re, new_s5_im = [], [], [], []
    ret_i = gla_i = s5_i = 0
    for layer in range(depth):
        sh1, sc1, g1, sh2, sc2, g2 = [mod_b[layer, :, t] for t in range(6)]
        lng1, lnb1 = ln_g[layer, 0][None], ln_b[layer, 0][None]
        lng2, lnb2 = ln_g[layer, 1][None], ln_b[layer, 1][None]
        wr, br = wr_pad[layer], br_pad[layer]
        kind = layer % 3
        if kind == 0:
            qkvg = _proj(x, sc1, sh1, ret_w_in[ret_i].astype(BF16), tm=1024, tn=1536)
            a, st = _retention(qkvg, meta, lg_ret[ret_i].reshape(-1), cos_t, sin_t, state_ret, ret_i, nb, n_p)
            new_ret.append(st)
            x, h2, idx_t, gate_t = _mixer_out(a, ret_w_out[ret_i].astype(BF16), x, g1, lng1, lnb1, sc2, sh2,
                                              wr, br, alpha)
            ret_i += 1
        elif kind == 1:
            qkvr = _proj(x, sc1, sh1, gla_w_in[gla_i].astype(BF16), tm=1024, tn=1536)
            rank = gla_w_a1.shape[-1]
            key = gla_w_a2.shape[-1]
            w1 = jnp.concatenate([gla_w_a1[gla_i, 0], gla_w_a1[gla_i, 1]], axis=-1).astype(BF16)
            w2 = jnp.zeros((2 * rank, 2 * key), F32)
            w2 = w2.at[:rank, :key].set(gla_w_a2[gla_i, 0]).at[rank:, key:].set(gla_w_a2[gla_i, 1]).astype(BF16)
            la = _gla_gate(x, sc1, sh1, w1, w2, gla_b_a[gla_i].reshape(1, 2 * key))
            a, st = _gla(qkvr, la, meta, state_gla, gla_i, nb, n_p)
            new_gla.append(st)
            x, h2, idx_t, gate_t = _mixer_out(a, gla_w_out[gla_i].astype(BF16), x, g1, lng1, lnb1, sc2, sh2,
                                              wr, br, alpha)
            gla_i += 1
        else:
            t = S5_CHUNK
            g = d // S5_GROUP
            p = s5_a_re.shape[-1]
            r = n // t
            w_in, m_intra, c_f, c_b, a1, a2 = _s5_tables(
                s5_a_re[s5_i], s5_a_im[s5_i], s5_log_step[s5_i], s5_b_re[s5_i], s5_b_im[s5_i],
                s5_c_re[s5_i], s5_c_im[s5_i])
            hmod = _modulate(x, sc1, sh1)
            u = hmod.reshape(r, t, g, S5_GROUP).transpose(2, 0, 1, 3).reshape(g, r, t * S5_GROUP)
            wf, wb = _s5_in(u, w_in, tr=r // 2)
            s0 = jnp.concatenate([state_s5_re[:, s5_i], state_s5_im[:, s5_i]], axis=-1)
            s0 = jnp.concatenate([jnp.zeros((n_p,) + s0.shape[1:], F32), s0], axis=0)
            xf, stf = _s5_scan(wf.reshape(r, g, 2 * p), a1[0], a2[0], s0[:, 0], meta, nb, nseq, reverse=False)
            xb, stb = _s5_scan(wb.reshape(r, g, 2 * p), a1[1], a2[1], s0[:, 1], meta, nb, nseq, reverse=True)
            yg = _s5_out(u, m_intra, xf.reshape(r, g * 2 * p), c_f, xb.reshape(r, g * 2 * p), c_b, tr=r // 2)
            y = yg.reshape(g, r, t, S5_GROUP).transpose(1, 2, 0, 3).reshape(n, d)
            st = jnp.stack([stf[:n_p], stb[:n_p]], axis=1)
            new_s5_re.append(st[..., :p])
            new_s5_im.append(st[..., p:])
            wg = s5_w_glu[s5_i].astype(BF16)
            x, h2, idx_t, gate_t = _glu_out(y, wg[:, :d], wg[:, d:], s5_d[s5_i][None], sc1, sh1, x, g1,
                                            lng1, lnb1, sc2, sh2, wr, br, alpha)
            s5_i += 1
        y = _moe(h2, idx_t[:, :TOP_K], gate_t[:, :TOP_K], moe_w_gu[layer].astype(BF16),
                 moe_b_gu[layer][:, None, :], moe_w_down[layer].astype(BF16), moe_b_down[layer][:, None, :])
        x = _res_ln(x, y, g2, lng2, lnb2, alpha)

    y_prompt = x[:n_p * l_p].reshape(n_p, l_p, d)
    y_sample = x[n_p * l_p:].reshape(n_s, l_s, d)
    return (y_prompt, y_sample, jnp.stack(new_ret, axis=1), jnp.stack(new_gla, axis=1),
            jnp.stack(new_s5_re, axis=1), jnp.stack(new_s5_im, axis=1))
```

```python
import functools
import math

import numpy as np
import jax
import jax.numpy as jnp
from jax import lax
from jax.experimental import pallas as pl
from jax.experimental.pallas import tpu as pltpu

F32 = jnp.float32
BF16 = jnp.bfloat16

RET_HEADS = 4
GLA_HEADS = 4
GLA_TAU = 16.0
GRID_W = 64
ROPE_BASE = 10000.0
S5_GROUP = 16
TOP_K = 4
SWIGLU_ALPHA = 1.702
SWIGLU_LIMIT = 7.0
LN_EPS = 1e-5
GN_EPS = 1e-5

ROW_BLOCK = 256
GLA_CHUNK = 64
S5_CHUNK = 16
MOE_BLOCK = 256
ROUTER_PAD = 128
VMEM_LIMIT = 56 * 1024 * 1024

M_START, M_END, M_SAMPLE, M_SIDX, M_PIDX, M_POS, M_SEQ = range(7)
M_ROWS = 7

NT = (((1,), (1,)), ((), ()))
TN = (((0,), (0,)), ((), ()))


def _cparams(sem):
    return pltpu.CompilerParams(dimension_semantics=sem, vmem_limit_bytes=VMEM_LIMIT)


def _silu(x):
    return x * jax.nn.sigmoid(x)


def _dot(a, b):
    return jnp.dot(a, b, preferred_element_type=F32)


def _mod_kernel(c_ref, w_ref, b_ref, o_ref):
    c = c_ref[...]
    s = _silu(c).astype(BF16)
    o_ref[0] = _dot(s, w_ref[0].astype(BF16)) + b_ref[0]


def _mod_all(conds, w_mod, b_mod):
    depth, d, n6 = w_mod.shape
    nc = conds.shape[0]
    tn = 1536
    return pl.pallas_call(
        _mod_kernel,
        out_shape=jax.ShapeDtypeStruct((depth, nc, n6), F32),
        grid=(depth, n6 // tn),
        in_specs=[pl.BlockSpec((nc, d), lambda l, j: (0, 0)),
                  pl.BlockSpec((1, d, tn), lambda l, j: (l, 0, j)),
                  pl.BlockSpec((1, 1, tn), lambda l, j: (l, 0, j))],
        out_specs=pl.BlockSpec((1, nc, tn), lambda l, j: (l, 0, j)),
        compiler_params=_cparams(("parallel", "parallel")),
    )(conds, w_mod, b_mod.reshape(depth, 1, n6))


def _proj_kernel(x_ref, sc_ref, sh_ref, w_ref, o_ref, h_scr):
    @pl.when(pl.program_id(1) == 0)
    def _():
        h_scr[...] = (x_ref[...] * (1.0 + sc_ref[...]) + sh_ref[...]).astype(BF16)

    o_ref[...] = _dot(h_scr[...], w_ref[...]).astype(o_ref.dtype)


def _proj(x, sc, sh, w, tm, tn, out_dtype=BF16):
    n, d = x.shape
    nout = w.shape[1]
    kb = tm // ROW_BLOCK
    return pl.pallas_call(
        _proj_kernel,
        out_shape=jax.ShapeDtypeStruct((n, nout), out_dtype),
        grid=(n // tm, nout // tn),
        in_specs=[pl.BlockSpec((tm, d), lambda i, j: (i, 0)),
                  pl.BlockSpec((None, 1, d), lambda i, j: (i * kb, 0, 0)),
                  pl.BlockSpec((None, 1, d), lambda i, j: (i * kb, 0, 0)),
                  pl.BlockSpec((d, tn), lambda i, j: (0, j))],
        out_specs=pl.BlockSpec((tm, tn), lambda i, j: (i, j)),
        scratch_shapes=[pltpu.VMEM((tm, d), BF16)],
        compiler_params=_cparams(("parallel", "arbitrary")),
    )(x, sc, sh, w)


def _layer_norm(z, g, b):
    mu = jnp.mean(z, axis=-1, keepdims=True)
    zc = z - mu
    var = jnp.mean(zc * zc, axis=-1, keepdims=True)
    return zc * lax.rsqrt(var + LN_EPS) * g + b


def _route(h2, wr_ref, br_ref, cnt_scr, idx_ref, gate_ref, rank_ref, cnt_ref):
    logits = _dot(h2.astype(BF16), wr_ref[...]) + br_ref[...]
    tm, width = logits.shape
    col = lax.broadcasted_iota(jnp.int32, (tm, width), 1)
    colf = col.astype(F32)
    l = logits
    vals, idxs = [], []
    for _ in range(TOP_K):
        m = jnp.max(l, axis=-1, keepdims=True)
        ix = jnp.min(jnp.where(l == m, colf, float(width)), axis=-1, keepdims=True)
        vals.append(m)
        idxs.append(ix)
        l = jnp.where(colf == ix, -jnp.inf, l)
    es = [jnp.exp(v - vals[0]) for v in vals]
    den = es[0] + es[1] + es[2] + es[3]
    onehot = jnp.where(l == -jnp.inf, 1.0, 0.0)

    @pl.when(pl.program_id(0) == 0)
    def _():
        cnt_scr[...] = jnp.zeros_like(cnt_scr)

    rn = lax.broadcasted_iota(jnp.int32, (tm, tm), 0)
    rm = lax.broadcasted_iota(jnp.int32, (tm, tm), 1)
    earlier = jnp.where(rn > rm, 1.0, 0.0).astype(BF16)
    cum = _dot(earlier, onehot.astype(BF16)) + cnt_scr[...]
    idx_t = jnp.zeros((tm, width), jnp.int32)
    rank_t = jnp.zeros((tm, width), jnp.int32)
    gate_t = jnp.zeros((tm, width), F32)
    for k in range(TOP_K):
        rank_k = jnp.sum(jnp.where(colf == idxs[k], cum, 0.0), axis=-1, keepdims=True)
        idx_t = jnp.where(col == k, idxs[k].astype(jnp.int32), idx_t)
        rank_t = jnp.where(col == k, rank_k.astype(jnp.int32), rank_t)
        gate_t = jnp.where(col == k, es[k] / den, gate_t)
    idx_ref[...] = idx_t
    gate_ref[...] = gate_t
    rank_ref[...] = rank_t
    cnt_new = cnt_scr[...] + jnp.sum(onehot, axis=0, keepdims=True)
    cnt_scr[...] = cnt_new
    cnt_ref[...] = cnt_new


def _post_sublayer(alpha, x, y, g, lng, lnb):
    return _layer_norm(alpha * x + g * y, lng, lnb)


def _mixer_out_kernel(a_ref, w_ref, x_ref, g_ref, lng_ref, lnb_ref, sc2_ref, sh2_ref,
                      wr_ref, br_ref, xo_ref, h2_ref, idx_ref, gate_ref, rank_ref, cnt_ref, cnt_scr, *, alpha):
    y = _dot(a_ref[...], w_ref[...])
    xn = _post_sublayer(alpha, x_ref[...], y, g_ref[...], lng_ref[...], lnb_ref[...])
    xo_ref[...] = xn
    h2 = xn * (1.0 + sc2_ref[...]) + sh2_ref[...]
    h2_ref[...] = h2
    _route(h2, wr_ref, br_ref, cnt_scr, idx_ref, gate_ref, rank_ref, cnt_ref)


def _glu_out_kernel(y_ref, w1_ref, w2_ref, d_ref, sc1_ref, sh1_ref, x_ref, g_ref, lng_ref, lnb_ref,
                    sc2_ref, sh2_ref, wr_ref, br_ref, xo_ref, h2_ref, idx_ref, gate_ref, rank_ref, cnt_ref,
                    cnt_scr, *, alpha):
    x = x_ref[...]
    h = x * (1.0 + sc1_ref[...]) + sh1_ref[...]
    a = jax.nn.gelu(y_ref[...] + d_ref[...] * h).astype(BF16)
    y = _dot(a, w1_ref[...]) * jax.nn.sigmoid(_dot(a, w2_ref[...]))
    xn = _post_sublayer(alpha, x, y, g_ref[...], lng_ref[...], lnb_ref[...])
    xo_ref[...] = xn
    h2 = xn * (1.0 + sc2_ref[...]) + sh2_ref[...]
    h2_ref[...] = h2
    _route(h2, wr_ref, br_ref, cnt_scr, idx_ref, gate_ref, rank_ref, cnt_ref)


def _row_spec(tm, width):
    return pl.BlockSpec((tm, width), lambda i: (i, 0))


def _blk_vec_spec(tm, d):
    kb = tm // ROW_BLOCK
    return pl.BlockSpec((None, 1, d), lambda i: (i * kb, 0, 0))


def _const_spec(shape):
    return pl.BlockSpec(shape, lambda i: tuple(0 for _ in shape))


def _post_out_shapes(n, d):
    return (jax.ShapeDtypeStruct((n, d), F32), jax.ShapeDtypeStruct((n, d), F32),
            jax.ShapeDtypeStruct((n, ROUTER_PAD), jnp.int32), jax.ShapeDtypeStruct((n, ROUTER_PAD), F32),
            jax.ShapeDtypeStruct((n, ROUTER_PAD), jnp.int32), jax.ShapeDtypeStruct((1, ROUTER_PAD), F32))


def _post_out_specs(tm, d):
    return (_row_spec(tm, d), _row_spec(tm, d), _row_spec(tm, ROUTER_PAD), _row_spec(tm, ROUTER_PAD),
            _row_spec(tm, ROUTER_PAD), _const_spec((1, ROUTER_PAD)))


def _mixer_out(a, w, x, g1, lng, lnb, sc2, sh2, wr, br, alpha, tm=512):
    n, d = x.shape
    kin = a.shape[1]
    return pl.pallas_call(
        functools.partial(_mixer_out_kernel, alpha=alpha),
        out_shape=_post_out_shapes(n, d),
        grid=(n // tm,),
        in_specs=[_row_spec(tm, kin), _const_spec((kin, d)), _row_spec(tm, d), _blk_vec_spec(tm, d),
                  _const_spec((1, d)), _const_spec((1, d)), _blk_vec_spec(tm, d), _blk_vec_spec(tm, d),
                  _const_spec((d, ROUTER_PAD)), _const_spec((1, ROUTER_PAD))],
        out_specs=_post_out_specs(tm, d),
        scratch_shapes=[pltpu.VMEM((1, ROUTER_PAD), F32)],
        compiler_params=_cparams(("arbitrary",)),
    )(a, w, x, g1, lng, lnb, sc2, sh2, wr, br)


def _glu_out(y, w1, w2, dskip, sc1, sh1, x, g1, lng, lnb, sc2, sh2, wr, br, alpha, tm=512):
    n, d = x.shape
    return pl.pallas_call(
        functools.partial(_glu_out_kernel, alpha=alpha),
        out_shape=_post_out_shapes(n, d),
        grid=(n // tm,),
        in_specs=[_row_spec(tm, d), _const_spec((d, d)), _const_spec((d, d)), _const_spec((1, d)),
                  _blk_vec_spec(tm, d), _blk_vec_spec(tm, d), _row_spec(tm, d), _blk_vec_spec(tm, d),
                  _const_spec((1, d)), _const_spec((1, d)), _blk_vec_spec(tm, d), _blk_vec_spec(tm, d),
                  _const_spec((d, ROUTER_PAD)), _const_spec((1, ROUTER_PAD))],
        out_specs=_post_out_specs(tm, d),
        scratch_shapes=[pltpu.VMEM((1, ROUTER_PAD), F32)],
        compiler_params=_cparams(("arbitrary",)),
    )(y, w1, w2, dskip, sc1, sh1, x, g1, lng, lnb, sc2, sh2, wr, br)


def _rotary(x, cos, sin):
    half = x.shape[-1] // 2
    x1, x2 = x[:, :half], x[:, half:]
    return jnp.concatenate([x1 * cos - x2 * sin, x1 * sin + x2 * cos], axis=-1)


def _head_norm(o):
    mu = jnp.mean(o, axis=-1, keepdims=True)
    oc = o - mu
    var = jnp.mean(oc * oc, axis=-1, keepdims=True)
    return oc * lax.rsqrt(var + GN_EPS)


def _ret_fwd_kernel(meta_ref, lg_ref, q_ref, k_ref, v_ref, cos_ref, sin_ref, s0_ref,
                    o_ref, st_ref, s_scr, *, nb, heads):
    h = pl.program_id(0)
    i = pl.program_id(1)
    start = meta_ref[M_START * nb + i]
    end = meta_ref[M_END * nb + i]
    sample = meta_ref[M_SAMPLE * nb + i]
    c = q_ref.shape[0]
    dk = q_ref.shape[1]

    @pl.when(jnp.logical_and(start == 1, sample == 1))
    def _():
        s_scr[...] = s0_ref[...]

    @pl.when(jnp.logical_and(start == 1, sample == 0))
    def _():
        s_scr[...] = jnp.zeros_like(s_scr)

    lgf = lg_ref[h]
    lgb = lg_ref[heads + h]
    cos = cos_ref[...]
    sin = sin_ref[...]
    q = _rotary(q_ref[...].astype(F32), cos, sin) * (dk ** -0.5)
    k = _rotary(k_ref[...].astype(F32), cos, sin)
    qb = q.astype(BF16)
    v = v_ref[...]

    rn = lax.broadcasted_iota(jnp.int32, (c, c), 0)
    rm = lax.broadcasted_iota(jnp.int32, (c, c), 1)
    rel = (rn - rm).astype(F32)
    dmat = (jnp.where(rel >= 0, jnp.exp(lgf * jnp.maximum(rel, 0.0)), 0.0)
            + jnp.where(rel <= 0, jnp.exp(lgb * jnp.maximum(-rel, 0.0)), 0.0))
    scores = lax.dot_general(qb, k.astype(BF16), NT, preferred_element_type=F32) * dmat
    idx = lax.broadcasted_iota(jnp.int32, (c, 1), 0).astype(F32)
    xi = jnp.exp(lgf * (idx + 1.0))
    zeta = jnp.exp(lgf * (c - 1.0 - idx))
    s_prev = s_scr[...]
    o_ref[...] = _dot(scores.astype(BF16), v) + _dot(qb, s_prev.astype(BF16)) * xi
    g_chunk = jnp.exp(lgf * jnp.full((1, 1), float(c), F32))
    s_new = s_prev * g_chunk + lax.dot_general((k * zeta).astype(BF16), v, TN, preferred_element_type=F32)
    s_scr[...] = s_new

    @pl.when(jnp.logical_and(end == 1, sample == 0))
    def _():
        st_ref[...] = s_new


def _ret_bwd_kernel(meta_ref, lg_ref, q_ref, k_ref, v_ref, g_ref, cos_ref, sin_ref, s0_ref, of_ref,
                    o_ref, st_ref, s_scr, *, nb, heads):
    h = pl.program_id(0)
    i = nb - 1 - pl.program_id(1)
    start = meta_ref[M_START * nb + i]
    end = meta_ref[M_END * nb + i]
    sample = meta_ref[M_SAMPLE * nb + i]
    c = q_ref.shape[0]
    dk = q_ref.shape[1]

    @pl.when(jnp.logical_and(end == 1, sample == 1))
    def _():
        s_scr[...] = s0_ref[...]

    @pl.when(jnp.logical_and(end == 1, sample == 0))
    def _():
        s_scr[...] = jnp.zeros_like(s_scr)

    lgb = lg_ref[heads + h]
    cos = cos_ref[...]
    sin = sin_ref[...]
    q = _rotary(q_ref[...].astype(F32), cos, sin) * (dk ** -0.5)
    k = _rotary(k_ref[...].astype(F32), cos, sin)
    v = v_ref[...]
    idx = lax.broadcasted_iota(jnp.int32, (c, 1), 0).astype(F32)
    xi = jnp.exp(lgb * (c - idx))
    zeta = jnp.exp(lgb * idx)
    s_prev = s_scr[...]
    o = of_ref[...] + _dot(q.astype(BF16), s_prev.astype(BF16)) * xi
    g_chunk = jnp.exp(lgb * jnp.full((1, 1), float(c), F32))
    s_new = s_prev * g_chunk + lax.dot_general((k * zeta).astype(BF16), v, TN, preferred_element_type=F32)
    s_scr[...] = s_new
    o_ref[...] = (_silu(g_ref[...].astype(F32)) * _head_norm(o)).astype(o_ref.dtype)

    @pl.when(jnp.logical_and(start == 1, sample == 0))
    def _():
        st_ref[...] = s_new


def _retention(qkvg, meta, lg, cos_t, sin_t, state, li, nb, n_prompt):
    n = qkvg.shape[0]
    heads = RET_HEADS
    d = qkvg.shape[1] // 6
    dk, dv = d // heads, 2 * d // heads
    c = ROW_BLOCK
    kw = dict(nb=nb, heads=heads)
    s0_spec = lambda d_: pl.BlockSpec(
        (None, None, None, None, dk, dv), lambda h, j, m: (m[M_SIDX * nb + (j if d_ == 0 else nb - 1 - j)], li, d_, h, 0, 0))
    pos = lambda j, m: m[M_POS * nb + j]
    of, st_f = pl.pallas_call(
        functools.partial(_ret_fwd_kernel, **kw),
        out_shape=(jax.ShapeDtypeStruct((n, heads * dv), F32), jax.ShapeDtypeStruct((n_prompt, heads, dk, dv), F32)),
        grid_spec=pltpu.PrefetchScalarGridSpec(
            num_scalar_prefetch=1, grid=(heads, nb),
            in_specs=[pl.BlockSpec(memory_space=pltpu.SMEM),
                      pl.BlockSpec((c, dk), lambda h, j, m: (j, h)),
                      pl.BlockSpec((c, dk), lambda h, j, m: (j, heads + h)),
                      pl.BlockSpec((c, dv), lambda h, j, m: (j, heads + h)),
                      pl.BlockSpec((c, dk // 2), lambda h, j, m: (pos(j, m), 0)),
                      pl.BlockSpec((c, dk // 2), lambda h, j, m: (pos(j, m), 0)),
                      s0_spec(0)],
            out_specs=(pl.BlockSpec((c, dv), lambda h, j, m: (j, h)),
                       pl.BlockSpec((None, None, dk, dv), lambda h, j, m: (m[M_PIDX * nb + j], h, 0, 0))),
            scratch_shapes=[pltpu.VMEM((dk, dv), F32)]),
        compiler_params=_cparams(("parallel", "arbitrary")),
    )(meta, lg, qkvg, qkvg, qkvg, cos_t, sin_t, state)
    r = lambda j: nb - 1 - j
    out, st_b = pl.pallas_call(
        functools.partial(_ret_bwd_kernel, **kw),
        out_shape=(jax.ShapeDtypeStruct((n, heads * dv), BF16), jax.ShapeDtypeStruct((n_prompt, heads, dk, dv), F32)),
        grid_spec=pltpu.PrefetchScalarGridSpec(
            num_scalar_prefetch=1, grid=(heads, nb),
            in_specs=[pl.BlockSpec(memory_space=pltpu.SMEM),
                      pl.BlockSpec((c, dk), lambda h, j, m: (r(j), h)),
                      pl.BlockSpec((c, dk), lambda h, j, m: (r(j), heads + h)),
                      pl.BlockSpec((c, dv), lambda h, j, m: (r(j), heads + h)),
                      pl.BlockSpec((c, dv), lambda h, j, m: (r(j), 2 * heads + h)),
                      pl.BlockSpec((c, dk // 2), lambda h, j, m: (pos(r(j), m), 0)),
                      pl.BlockSpec((c, dk // 2), lambda h, j, m: (pos(r(j), m), 0)),
                      s0_spec(1),
                      pl.BlockSpec((c, dv), lambda h, j, m: (r(j), h))],
            out_specs=(pl.BlockSpec((c, dv), lambda h, j, m: (r(j), h)),
                       pl.BlockSpec((None, None, dk, dv), lambda h, j, m: (m[M_PIDX * nb + r(j)], h, 0, 0))),
            scratch_shapes=[pltpu.VMEM((dk, dv), F32)]),
        compiler_params=_cparams(("parallel", "arbitrary")),
    )(meta, lg, qkvg, qkvg, qkvg, qkvg, cos_t, sin_t, state, of)
    return out, jnp.stack([st_f, st_b], axis=1)


def _gla_gate_kernel(x_ref, sc_ref, sh_ref, w1_ref, w2_ref, b_ref, o_ref):
    h = (x_ref[...] * (1.0 + sc_ref[...]) + sh_ref[...]).astype(BF16)
    u = _dot(h, w1_ref[...]).astype(BF16)
    z = _dot(u, w2_ref[...]) + b_ref[...]
    log_sig = jnp.minimum(z, 0.0) - jnp.log(1.0 + jnp.exp(-jnp.abs(z)))
    o_ref[...] = log_sig / GLA_TAU


def _gla_gate(x, sc, sh, w1, w2, b, tm=512):
    n, d = x.shape
    r2 = w1.shape[1]
    nk = w2.shape[1]
    return pl.pallas_call(
        _gla_gate_kernel,
        out_shape=jax.ShapeDtypeStruct((n, nk), F32),
        grid=(n // tm,),
        in_specs=[_row_spec(tm, d), _blk_vec_spec(tm, d), _blk_vec_spec(tm, d), _const_spec((d, r2)),
                  _const_spec((r2, nk)), _const_spec((1, nk))],
        out_specs=_row_spec(tm, nk),
        compiler_params=_cparams(("parallel",)),
    )(x, sc, sh, w1, w2, b)


def _cumsum_rows(tri, x):
    hi = x.astype(BF16)
    r1 = x - hi.astype(F32)
    mid = r1.astype(BF16)
    lo = (r1 - mid.astype(F32)).astype(BF16)
    return _dot(tri, hi) + _dot(tri, mid) + _dot(tri, lo)


def _gla_chunk(q, k, v, la, s_t, tri, mask, mid_row, last_row):
    b = _cumsum_rows(tri, la)
    r = b[mid_row:mid_row + 1, :]
    bl = b[last_row:last_row + 1, :]
    qe = (q * jnp.exp(b - r)).astype(BF16)
    ke = (k * jnp.exp(r - b)).astype(BF16)
    qi = (q * jnp.exp(b)).astype(BF16)
    kd = (k * jnp.exp(bl - b)).astype(BF16)
    sc = lax.dot_general(qe, ke, NT, preferred_element_type=F32)
    sc = jnp.where(mask, sc, 0.0).astype(BF16)
    o = _dot(sc, v) + lax.dot_general(qi, s_t.astype(BF16), NT, preferred_element_type=F32)
    s_new = s_t * jnp.exp(bl) + lax.dot_general(v, kd, TN, preferred_element_type=F32)
    return o, s_new


def _gla_fwd_kernel(meta_ref, q_ref, k_ref, v_ref, la_ref, s0_ref, o_ref, st_ref, s_scr, *, nb):
    i = pl.program_id(1)
    start = meta_ref[M_START * nb + i]
    end = meta_ref[M_END * nb + i]
    sample = meta_ref[M_SAMPLE * nb + i]
    dk = q_ref.shape[1]
    cc = GLA_CHUNK

    @pl.when(jnp.logical_and(start == 1, sample == 1))
    def _():
        s_scr[...] = s0_ref[...].T

    @pl.when(jnp.logical_and(start == 1, sample == 0))
    def _():
        s_scr[...] = jnp.zeros_like(s_scr)

    rn = lax.broadcasted_iota(jnp.int32, (cc, cc), 0)
    rm = lax.broadcasted_iota(jnp.int32, (cc, cc), 1)
    mask = rn >= rm
    tri = jnp.where(mask, 1.0, 0.0).astype(BF16)
    s_t = s_scr[...]
    for ci in range(q_ref.shape[0] // cc):
        sl = slice(ci * cc, (ci + 1) * cc)
        q = q_ref[sl, :].astype(F32) * (dk ** -0.5)
        o, s_t = _gla_chunk(q, k_ref[sl, :].astype(F32), v_ref[sl, :], la_ref[sl, :], s_t, tri, mask,
                            cc // 2 - 1, cc - 1)
        o_ref[sl, :] = o
    s_scr[...] = s_t

    @pl.when(jnp.logical_and(end == 1, sample == 0))
    def _():
        st_ref[...] = s_t.T


def _gla_bwd_kernel(meta_ref, q_ref, k_ref, v_ref, r_ref, la_ref, s0_ref, of_ref, o_ref, st_ref, s_scr, *, nb):
    i = nb - 1 - pl.program_id(1)
    start = meta_ref[M_START * nb + i]
    end = meta_ref[M_END * nb + i]
    sample = meta_ref[M_SAMPLE * nb + i]
    dk = q_ref.shape[1]
    cc = GLA_CHUNK

    @pl.when(jnp.logical_and(end == 1, sample == 1))
    def _():
        s_scr[...] = s0_ref[...].T

    @pl.when(jnp.logical_and(end == 1, sample == 0))
    def _():
        s_scr[...] = jnp.zeros_like(s_scr)

    rn = lax.broadcasted_iota(jnp.int32, (cc, cc), 0)
    rm = lax.broadcasted_iota(jnp.int32, (cc, cc), 1)
    mask = rn <= rm
    tri = jnp.where(mask, 1.0, 0.0).astype(BF16)
    s_t = s_scr[...]
    for ci in reversed(range(q_ref.shape[0] // cc)):
        sl = slice(ci * cc, (ci + 1) * cc)
        q = q_ref[sl, :].astype(F32) * (dk ** -0.5)
        ob, s_t = _gla_chunk(q, k_ref[sl, :].astype(F32), v_ref[sl, :], la_ref[sl, :], s_t, tri, mask,
                             cc // 2, 0)
        o = of_ref[sl, :] + ob
        o_ref[sl, :] = (_silu(r_ref[sl, :].astype(F32)) * _head_norm(o)).astype(o_ref.dtype)
    s_scr[...] = s_t

    @pl.when(jnp.logical_and(start == 1, sample == 0))
    def _():
        st_ref[...] = s_t.T


def _gla(qkvr, la, meta, state, li, nb, n_prompt):
    n = qkvr.shape[0]
    heads = GLA_HEADS
    d = qkvr.shape[1] // 3
    dk, dv = d // 2 // heads, d // heads
    c = ROW_BLOCK
    s0_spec = lambda d_: pl.BlockSpec(
        (None, None, None, None, dk, dv), lambda h, j, m: (m[M_SIDX * nb + (j if d_ == 0 else nb - 1 - j)], li, d_, h, 0, 0))
    of, st_f = pl.pallas_call(
        functools.partial(_gla_fwd_kernel, nb=nb),
        out_shape=(jax.ShapeDtypeStruct((n, d), F32), jax.ShapeDtypeStruct((n_prompt, heads, dk, dv), F32)),
        grid_spec=pltpu.PrefetchScalarGridSpec(
            num_scalar_prefetch=1, grid=(heads, nb),
            in_specs=[pl.BlockSpec((c, dk), lambda h, j, m: (j, h)),
                      pl.BlockSpec((c, dk), lambda h, j, m: (j, heads + h)),
                      pl.BlockSpec((c, dv), lambda h, j, m: (j, heads + h)),
                      pl.BlockSpec((c, dk), lambda h, j, m: (j, h)),
                      s0_spec(0)],
            out_specs=(pl.BlockSpec((c, dv), lambda h, j, m: (j, h)),
                       pl.BlockSpec((None, None, dk, dv), lambda h, j, m: (m[M_PIDX * nb + j], h, 0, 0))),
            scratch_shapes=[pltpu.VMEM((dv, dk), F32)]),
        compiler_params=_cparams(("parallel", "arbitrary")),
    )(meta, qkvr, qkvr, qkvr, la, state)
    r = lambda j: nb - 1 - j
    out, st_b = pl.pallas_call(
        functools.partial(_gla_bwd_kernel, nb=nb),
        out_shape=(jax.ShapeDtypeStruct((n, d), BF16), jax.ShapeDtypeStruct((n_prompt, heads, dk, dv), F32)),
        grid_spec=pltpu.PrefetchScalarGridSpec(
            num_scalar_prefetch=1, grid=(heads, nb),
            in_specs=[pl.BlockSpec((c, dk), lambda h, j, m: (r(j), h)),
                      pl.BlockSpec((c, dk), lambda h, j, m: (r(j), heads + h)),
                      pl.BlockSpec((c, dv), lambda h, j, m: (r(j), heads + h)),
                      pl.BlockSpec((c, dv), lambda h, j, m: (r(j), 2 * heads + h)),
                      pl.BlockSpec((c, dk), lambda h, j, m: (r(j), heads + h)),
                      s0_spec(1),
                      pl.BlockSpec((c, dv), lambda h, j, m: (r(j), h))],
            out_specs=(pl.BlockSpec((c, dv), lambda h, j, m: (r(j), h)),
                       pl.BlockSpec((None, None, dk, dv), lambda h, j, m: (m[M_PIDX * nb + r(j)], h, 0, 0))),
            scratch_shapes=[pltpu.VMEM((dv, dk), F32)]),
        compiler_params=_cparams(("parallel", "arbitrary")),
    )(meta, qkvr, qkvr, qkvr, qkvr, la, state, of)
    return out, jnp.stack([st_f, st_b], axis=1)


def _modulate_kernel(x_ref, sc_ref, sh_ref, o_ref):
    o_ref[...] = (x_ref[...] * (1.0 + sc_ref[...]) + sh_ref[...]).astype(o_ref.dtype)


def _modulate(x, sc, sh, tm=512):
    n, d = x.shape
    return pl.pallas_call(
        _modulate_kernel,
        out_shape=jax.ShapeDtypeStruct((n, d), BF16),
        grid=(n // tm,),
        in_specs=[_row_spec(tm, d), _blk_vec_spec(tm, d), _blk_vec_spec(tm, d)],
        out_specs=_row_spec(tm, d),
        compiler_params=_cparams(("parallel",)),
    )(x, sc, sh)


def _s5_in_kernel(u_ref, w_ref, wf_ref, wb_ref):
    z = _dot(u_ref[...], w_ref[...])
    half = z.shape[1] // 2
    wf_ref[...] = z[:, :half]
    wb_ref[...] = z[:, half:]


def _s5_in(u, w_in, tr):
    g, r, kc = u.shape
    p2 = w_in.shape[2] // 2
    return pl.pallas_call(
        _s5_in_kernel,
        out_shape=(jax.ShapeDtypeStruct((r, g * p2), F32), jax.ShapeDtypeStruct((r, g * p2), F32)),
        grid=(g, r // tr),
        in_specs=[pl.BlockSpec((None, tr, kc), lambda gi, ri: (gi, ri, 0)),
                  pl.BlockSpec((None, kc, 2 * p2), lambda gi, ri: (gi, 0, 0))],
        out_specs=(pl.BlockSpec((tr, p2), lambda gi, ri: (ri, gi)),
                   pl.BlockSpec((tr, p2), lambda gi, ri: (ri, gi))),
        compiler_params=_cparams(("parallel", "parallel")),
    )(u, w_in)


def _s5_scan_kernel(meta_ref, w_ref, a1_ref, a2_ref, s0_ref, xp_ref, st_ref, x_scr, *, nb, reverse):
    j = pl.program_id(0)
    i = nb - 1 - j if reverse else j
    first = meta_ref[(M_END if reverse else M_START) * nb + i]
    last = meta_ref[(M_START if reverse else M_END) * nb + i]

    @pl.when(first == 1)
    def _():
        x_scr[...] = s0_ref[...]

    a1 = a1_ref[...]
    a2 = a2_ref[...]
    x = x_scr[...]
    half = x.shape[1] // 2
    nrow = w_ref.shape[0]
    order = range(nrow - 1, -1, -1) if reverse else range(nrow)
    for c in order:
        xp_ref[c] = x
        x = x * a1 + pltpu.roll(x, half, axis=1) * a2 + w_ref[c]
    x_scr[...] = x

    @pl.when(last == 1)
    def _():
        st_ref[...] = x


def _s5_scan(w, a1, a2, s0, meta, nb, nseq, reverse):
    r, g, p2 = w.shape
    cr = r // nb
    blk = (lambda j: nb - 1 - j) if reverse else (lambda j: j)
    return pl.pallas_call(
        functools.partial(_s5_scan_kernel, nb=nb, reverse=reverse),
        out_shape=(jax.ShapeDtypeStruct((r, g, p2), F32), jax.ShapeDtypeStruct((nseq, g, p2), F32)),
        grid_spec=pltpu.PrefetchScalarGridSpec(
            num_scalar_prefetch=1, grid=(nb,),
            in_specs=[pl.BlockSpec((cr, g, p2), lambda j, m: (blk(j), 0, 0)),
                      pl.BlockSpec((g, p2), lambda j, m: (0, 0)),
                      pl.BlockSpec((g, p2), lambda j, m: (0, 0)),
                      pl.BlockSpec((None, g, p2), lambda j, m: (m[M_SEQ * nb + blk(j)], 0, 0))],
            out_specs=(pl.BlockSpec((cr, g, p2), lambda j, m: (blk(j), 0, 0)),
                       pl.BlockSpec((None, g, p2), lambda j, m: (m[M_SEQ * nb + blk(j)], 0, 0))),
            scratch_shapes=[pltpu.VMEM((g, p2), F32)]),
        compiler_params=_cparams(("arbitrary",)),
    )(meta, w, a1, a2, s0)


def _s5_out_kernel(u_ref, m_ref, xf_ref, cf_ref, xb_ref, cb_ref, y_ref):
    y_ref[...] = (_dot(u_ref[...], m_ref[...]) + _dot(xf_ref[...].astype(BF16), cf_ref[...])
                  + _dot(xb_ref[...].astype(BF16), cb_ref[...]))


def _s5_out(u, m_intra, xf, cf, xb, cb, tr):
    g, r, kc = u.shape
    p2 = cf.shape[1]
    return pl.pallas_call(
        _s5_out_kernel,
        out_shape=jax.ShapeDtypeStruct((g, r, kc), F32),
        grid=(g, r // tr),
        in_specs=[pl.BlockSpec((None, tr, kc), lambda gi, ri: (gi, ri, 0)),
                  pl.BlockSpec((None, kc, kc), lambda gi, ri: (gi, 0, 0)),
                  pl.BlockSpec((tr, p2), lambda gi, ri: (ri, gi)),
                  pl.BlockSpec((None, p2, kc), lambda gi, ri: (gi, 0, 0)),
                  pl.BlockSpec((tr, p2), lambda gi, ri: (ri, gi)),
                  pl.BlockSpec((None, p2, kc), lambda gi, ri: (gi, 0, 0))],
        out_specs=pl.BlockSpec((None, tr, kc), lambda gi, ri: (gi, ri, 0)),
        compiler_params=_cparams(("parallel", "parallel")),
    )(u, m_intra, xf, cf, xb, cb)


def _s5_tables(a_re, a_im, log_step, b_re, b_im, c_re, c_im):
    t = S5_CHUNK
    step = jnp.exp(log_step.astype(F32))[..., None]
    ar, ai = a_re.astype(F32), a_im.astype(F32)
    mag = jnp.exp(ar * step)
    ab_re, ab_im = mag * jnp.cos(ai * step), mag * jnp.sin(ai * step)
    den = ar * ar + ai * ai
    f_re = ((ab_re - 1.0) * ar + ab_im * ai) / den
    f_im = (ab_im * ar - (ab_re - 1.0) * ai) / den
    br, bi = b_re.astype(F32), b_im.astype(F32)
    bb_re = f_re[..., None] * br - f_im[..., None] * bi
    bb_im = f_re[..., None] * bi + f_im[..., None] * br
    cr, ci = c_re.astype(F32), c_im.astype(F32)
    tau = jnp.arange(t + 1, dtype=F32)[:, None, None, None]
    pmag = jnp.exp(ar[None] * step[None] * tau)
    pw_re = pmag * jnp.cos(ai[None] * step[None] * tau)
    pw_im = pmag * jnp.sin(ai[None] * step[None] * tau)
    g, p, ch = br.shape[1], br.shape[2], br.shape[3]

    def cmul(xr, xi, yr, yi):
        return xr * yr - xi * yi, xr * yi + xi * yr

    def inject(d, powers):
        pr = pw_re[powers, d][:, :, :, None]
        pi = pw_im[powers, d][:, :, :, None]
        wr, wi = cmul(pr, pi, bb_re[d][None], bb_im[d][None])
        w = jnp.concatenate([wr, wi], axis=2)
        return w.transpose(1, 0, 3, 2).reshape(g, t * ch, 2 * p)
    w_in = jnp.concatenate([inject(0, jnp.arange(t - 1, -1, -1)), inject(1, jnp.arange(t))], axis=-1)

    def readout(d, powers):
        pr = pw_re[powers, d][:, :, None, :]
        pi = pw_im[powers, d][:, :, None, :]
        kr, ki = cmul(cr[d][None], ci[d][None], pr, pi)
        m = jnp.concatenate([kr, -ki], axis=3)
        return m.transpose(1, 3, 0, 2).reshape(g, 2 * p, t * ch)
    c_f = readout(0, jnp.arange(1, t + 1))
    c_b = readout(1, jnp.arange(t, 0, -1))

    def lag_kernels(d):
        pr = pw_re[:t, d][:, :, :, None]
        pi = pw_im[:t, d][:, :, :, None]
        wr, wi = cmul(pr, pi, bb_re[d][None], bb_im[d][None])
        return (jnp.einsum('gop,tgpc->tgoc', cr[d], wr) - jnp.einsum('gop,tgpc->tgoc', ci[d], wi))
    kf, kb = lag_kernels(0), lag_kernels(1)
    ii = jnp.arange(t)
    lag = ii[None, :] - ii[:, None]
    mf = jnp.where((lag >= 0)[:, :, None, None, None], kf[jnp.clip(lag, 0, t - 1)], 0.0)
    mb = jnp.where((lag <= 0)[:, :, None, None, None], kb[jnp.clip(-lag, 0, t - 1)], 0.0)
    m = (mf + mb).transpose(2, 0, 4, 1, 3).reshape(g, t * ch, t * ch)

    at_re, at_im = pw_re[t], pw_im[t]
    a1 = jnp.concatenate([at_re, at_re], axis=-1)
    a2 = jnp.concatenate([-at_im, at_im], axis=-1)
    return w_in.astype(BF16), m.astype(BF16), c_f.astype(BF16), c_b.astype(BF16), a1, a2


def _dispatch_kernel(dest_ref, h_ref, xs_in_ref, xs_ref, sem):
    del xs_in_ref
    tm = h_ref.shape[0]

    def issue(t, carry):
        for k in range(TOP_K):
            pltpu.make_async_copy(h_ref.at[pl.ds(t, 1)], xs_ref.at[pl.ds(dest_ref[t * TOP_K + k], 1)], sem).start()
        return carry

    lax.fori_loop(0, tm, issue, 0)
    for _ in range(TOP_K):
        pltpu.make_async_copy(h_ref, xs_ref.at[pl.ds(0, tm)], sem).wait()


def _dispatch(dest, h2, n_rows, tm=512):
    n, d = h2.shape
    return pl.pallas_call(
        _dispatch_kernel,
        out_shape=jax.ShapeDtypeStruct((n_rows, d), F32),
        grid=(n // tm,),
        in_specs=[pl.BlockSpec((tm * TOP_K,), lambda i: (i,), memory_space=pltpu.SMEM),
                  _row_spec(tm, d),
                  pl.BlockSpec(memory_space=pl.ANY)],
        out_specs=pl.BlockSpec(memory_space=pl.ANY),
        scratch_shapes=[pltpu.SemaphoreType.DMA],
        input_output_aliases={2: 0},
        compiler_params=_cparams(("arbitrary",)),
    )(dest, h2, jnp.zeros((n_rows, d), F32))


def _ffn_kernel(sp_ref, x_ref, wgu_ref, bgu_ref, wd_ref, bd_ref, o_ref, wgu_s, wd_s, *, n_blocks):
    b = pl.program_id(0)
    used = b < sp_ref[2 * n_blocks]
    dff = wd_ref.shape[0]

    @pl.when(jnp.logical_and(used, sp_ref[n_blocks + b] == 1))
    def _():
        wgu_s[...] = wgu_ref[...].astype(BF16)
        wd_s[...] = wd_ref[...].astype(BF16)

    @pl.when(used)
    def _():
        gu = _dot(x_ref[...].astype(BF16), wgu_s[...]) + bgu_ref[...]
        gate = jnp.minimum(gu[:, :dff], SWIGLU_LIMIT)
        lin = jnp.clip(gu[:, dff:], -SWIGLU_LIMIT, SWIGLU_LIMIT)
        act = gate * jax.nn.sigmoid(SWIGLU_ALPHA * gate) * (lin + 1.0)
        o_ref[...] = _dot(act.astype(BF16), wd_s[...]) + bd_ref[...]

    @pl.when(jnp.logical_not(used))
    def _():
        o_ref[...] = jnp.zeros_like(o_ref)


def _ffn(sp, xs, wgu, bgu, wd, bd):
    n_rows, d = xs.shape
    bm = MOE_BLOCK
    n_blocks = n_rows // bm
    dff = wd.shape[1]
    return pl.pallas_call(
        functools.partial(_ffn_kernel, n_blocks=n_blocks),
        out_shape=jax.ShapeDtypeStruct((n_rows, d), F32),
        grid_spec=pltpu.PrefetchScalarGridSpec(
            num_scalar_prefetch=1, grid=(n_blocks,),
            in_specs=[pl.BlockSpec((bm, d), lambda b, sp: (b, 0)),
                      pl.BlockSpec((None, d, 2 * dff), lambda b, sp: (sp[b], 0, 0)),
                      pl.BlockSpec((None, 1, 2 * dff), lambda b, sp: (sp[b], 0, 0)),
                      pl.BlockSpec((None, dff, d), lambda b, sp: (sp[b], 0, 0)),
                      pl.BlockSpec((None, 1, d), lambda b, sp: (sp[b], 0, 0))],
            out_specs=pl.BlockSpec((bm, d), lambda b, sp: (b, 0)),
            scratch_shapes=[pltpu.VMEM((d, 2 * dff), BF16), pltpu.VMEM((dff, d), BF16)]),
        compiler_params=_cparams(("arbitrary",)),
    )(sp, xs, wgu, bgu, wd, bd)


def _combine_kernel(dest_ref, yb_ref, gate_ref, x_ref, g_ref, lng_ref, lnb_ref, xo_ref, buf, sem, *, alpha):
    tm = x_ref.shape[0]

    def issue(t, carry):
        for k in range(TOP_K):
            pltpu.make_async_copy(yb_ref.at[pl.ds(dest_ref[t * TOP_K + k], 1)], buf.at[k, pl.ds(t, 1)], sem).start()
        return carry

    lax.fori_loop(0, tm, issue, 0)
    for k in range(TOP_K):
        pltpu.make_async_copy(yb_ref.at[pl.ds(0, tm)], buf.at[k], sem).wait()
    gates = gate_ref[...]
    y = gates[:, 0:1] * buf[0]
    for k in range(1, TOP_K):
        y = y + gates[:, k:k + 1] * buf[k]
    xo_ref[...] = _post_sublayer(alpha, x_ref[...], y, g_ref[...], lng_ref[...], lnb_ref[...])


def _combine(dest, yb, gate_t, x, g2, lng, lnb, alpha, tm=512):
    n, d = x.shape
    return pl.pallas_call(
        functools.partial(_combine_kernel, alpha=alpha),
        out_shape=jax.ShapeDtypeStruct((n, d), F32),
        grid=(n // tm,),
        in_specs=[pl.BlockSpec((tm * TOP_K,), lambda i: (i,), memory_space=pltpu.SMEM),
                  pl.BlockSpec(memory_space=pl.ANY),
                  _row_spec(tm, ROUTER_PAD), _row_spec(tm, d), _blk_vec_spec(tm, d), _const_spec((1, d)),
                  _const_spec((1, d))],
        out_specs=_row_spec(tm, d),
        scratch_shapes=[pltpu.VMEM((TOP_K, tm, d), F32), pltpu.SemaphoreType.DMA],
        compiler_params=_cparams(("arbitrary",)),
    )(dest, yb, gate_t, x, g2, lng, lnb)


def _moe(x, h2, idx_t, gate_t, rank_t, cnt, g2, lng, lnb, wgu, bgu, wd, bd, alpha):
    n_tok, d = h2.shape
    n_exp = wgu.shape[0]
    bm = MOE_BLOCK
    n_blocks = -(-n_tok * TOP_K // bm) + n_exp
    counts = cnt[0, :n_exp].astype(jnp.int32)
    padded = (counts + bm - 1) // bm * bm
    pend = jnp.cumsum(padded)
    pstart = pend - padded
    e_iota = jnp.arange(n_exp, dtype=jnp.int32)
    dest = rank_t[:, :TOP_K] + jnp.sum(jnp.where(idx_t[:, :TOP_K, None] == e_iota, pstart, 0), axis=-1)
    dest = dest.reshape(-1).astype(jnp.int32)
    blk_row = jnp.arange(n_blocks, dtype=jnp.int32) * bm
    block_e = jnp.minimum(jnp.sum(pend[None, :] <= blk_row[:, None], axis=1), n_exp - 1).astype(jnp.int32)
    first = jnp.concatenate([jnp.ones((1,), jnp.int32), (block_e[1:] != block_e[:-1]).astype(jnp.int32)])
    sp = jnp.concatenate([block_e, first, (pend[-1:] // bm).astype(jnp.int32)])
    xs = _dispatch(dest, h2, n_blocks * bm)
    yb = _ffn(sp, xs, wgu, bgu, wd, bd)
    return _combine(dest, yb, gate_t, x, g2, lng, lnb, alpha)


def _block_meta(n_p, l_p, n_s, l_s):
    rb = ROW_BLOCK
    assert l_p % rb == 0 and l_s % rb == 0
    rows = []
    for s in range(n_p):
        for j in range(l_p // rb):
            rows.append((int(j == 0), int(j == l_p // rb - 1), 0, 0, s, 0, s))
    for s in range(n_s):
        for j in range(l_s // rb):
            rows.append((int(j == 0), int(j == l_s // rb - 1), 1, s, n_p - 1, 1 + j, n_p + s))
    meta = np.asarray(rows, np.int32).T
    cond = np.asarray([0] * (n_p * l_p // rb) + [1 + s for s in range(n_s) for _ in range(l_s // rb)], np.int32)
    return meta, cond


def _rotary_tables(l_s, dk):
    rows = l_s // GRID_W
    row = jnp.repeat(jnp.arange(rows, dtype=F32), GRID_W)
    col = jnp.tile(jnp.arange(GRID_W, dtype=F32), rows)
    n_freq = dk // 4
    inv = ROPE_BASE ** (-jnp.arange(n_freq, dtype=F32) / n_freq)
    ang = jnp.concatenate([row[:, None] * inv, col[:, None] * inv], axis=-1)
    ident = jnp.zeros((ROW_BLOCK, dk // 2), F32)
    return (jnp.concatenate([ident + 1.0, jnp.cos(ang)], axis=0),
            jnp.concatenate([ident, jnp.sin(ang)], axis=0))


def kernel(x_prompt, x_sample, state_ret, state_gla, state_s5_re, state_s5_im, c, c_ctx, w_mod, b_mod, ln_g, ln_b, ret_w_in, ret_decay, ret_w_out, gla_w_in, gla_w_a1, gla_w_a2, gla_b_a, gla_w_out, s5_a_re, s5_a_im, s5_log_step, s5_b_re, s5_b_im, s5_c_re, s5_c_im, s5_d, s5_w_glu, moe_w_router, moe_b_router, moe_w_gu, moe_b_gu, moe_w_down, moe_b_down):
    n_p, l_p, d = x_prompt.shape
    n_s, l_s, _ = x_sample.shape
    depth = w_mod.shape[0]
    n_exp = moe_w_router.shape[-1]
    alpha = (2 * depth) ** 0.25
    rb = ROW_BLOCK
    n = n_p * l_p + n_s * l_s
    nb = n // rb
    nseq = n_p + n_s
    meta_np, cond_np = _block_meta(n_p, l_p, n_s, l_s)
    meta = jnp.asarray(meta_np.reshape(-1))

    x = jnp.concatenate([x_prompt.reshape(n_p * l_p, d), x_sample.reshape(n_s * l_s, d)], axis=0)

    ncond = 16
    conds = jnp.zeros((ncond, d), F32).at[0].set(c_ctx).at[1:1 + n_s].set(c)
    mod = _mod_all(conds, w_mod, b_mod)
    mod_b = mod[:, cond_np].reshape(depth, nb, 6, 1, d)

    cos_t, sin_t = _rotary_tables(l_s, d // RET_HEADS)
    lg_ret = jax.nn.log_sigmoid(ret_decay.astype(F32))

    wr_pad = jnp.zeros((depth, d, ROUTER_PAD), BF16).at[:, :, :n_exp].set(moe_w_router.astype(BF16))
    br_pad = jnp.full((depth, 1, ROUTER_PAD), -1e30, F32).at[:, 0, :n_exp].set(moe_b_router)

    new_ret, new_gla, new_s5_re, new_s5_im = [], [], [], []
    ret_i = gla_i = s5_i = 0
    for layer in range(depth):
        sh1, sc1, g1, sh2, sc2, g2 = [mod_b[layer, :, t] for t in range(6)]
        lng1, lnb1 = ln_g[layer, 0][None], ln_b[layer, 0][None]
        lng2, lnb2 = ln_g[layer, 1][None], ln_b[layer, 1][None]
        wr, br = wr_pad[layer], br_pad[layer]
        kind = layer % 3
        if kind == 0:
            qkvg = _proj(x, sc1, sh1, ret_w_in[ret_i].astype(BF16), tm=1024, tn=1536)
            a, st = _retention(qkvg, meta, lg_ret[ret_i].reshape(-1), cos_t, sin_t, state_ret, ret_i, nb, n_p)
            new_ret.append(st)
            x, h2, idx_t, gate_t, rank_t, cnt = _mixer_out(a, ret_w_out[ret_i].astype(BF16), x, g1, lng1, lnb1, sc2, sh2,
                                              wr, br, alpha)
            ret_i += 1
        elif kind == 1:
            qkvr = _proj(x, sc1, sh1, gla_w_in[gla_i].astype(BF16), tm=1024, tn=1536)
            rank = gla_w_a1.shape[-1]
            key = gla_w_a2.shape[-1]
            w1 = jnp.concatenate([gla_w_a1[gla_i, 0], gla_w_a1[gla_i, 1]], axis=-1).astype(BF16)
            w2 = jnp.zeros((2 * rank, 2 * key), F32)
            w2 = w2.at[:rank, :key].set(gla_w_a2[gla_i, 0]).at[rank:, key:].set(gla_w_a2[gla_i, 1]).astype(BF16)
            la = _gla_gate(x, sc1, sh1, w1, w2, gla_b_a[gla_i].reshape(1, 2 * key))
            a, st = _gla(qkvr, la, meta, state_gla, gla_i, nb, n_p)
            new_gla.append(st)
            x, h2, idx_t, gate_t, rank_t, cnt = _mixer_out(a, gla_w_out[gla_i].astype(BF16), x, g1, lng1, lnb1, sc2, sh2,
                                              wr, br, alpha)
            gla_i += 1
        else:
            t = S5_CHUNK
            g = d // S5_GROUP
            p = s5_a_re.shape[-1]
            r = n // t
            w_in, m_intra, c_f, c_b, a1, a2 = _s5_tables(
                s5_a_re[s5_i], s5_a_im[s5_i], s5_log_step[s5_i], s5_b_re[s5_i], s5_b_im[s5_i],
                s5_c_re[s5_i], s5_c_im[s5_i])
            hmod = _modulate(x, sc1, sh1)
            u = hmod.reshape(r, t, g, S5_GROUP).transpose(2, 0, 1, 3).reshape(g, r, t * S5_GROUP)
            wf, wb = _s5_in(u, w_in, tr=r // 2)
            s0 = jnp.concatenate([state_s5_re[:, s5_i], state_s5_im[:, s5_i]], axis=-1)
            s0 = jnp.concatenate([jnp.zeros((n_p,) + s0.shape[1:], F32), s0], axis=0)
            xf, stf = _s5_scan(wf.reshape(r, g, 2 * p), a1[0], a2[0], s0[:, 0], meta, nb, nseq, reverse=False)
            xb, stb = _s5_scan(wb.reshape(r, g, 2 * p), a1[1], a2[1], s0[:, 1], meta, nb, nseq, reverse=True)
            yg = _s5_out(u, m_intra, xf.reshape(r, g * 2 * p), c_f, xb.reshape(r, g * 2 * p), c_b, tr=r // 2)
            y = yg.reshape(g, r, t, S5_GROUP).transpose(1, 2, 0, 3).reshape(n, d)
            st = jnp.stack([stf[:n_p], stb[:n_p]], axis=1)
            new_s5_re.append(st[..., :p])
            new_s5_im.append(st[..., p:])
            wg = s5_w_glu[s5_i].astype(BF16)
            x, h2, idx_t, gate_t, rank_t, cnt = _glu_out(y, wg[:, :d], wg[:, d:], s5_d[s5_i][None], sc1, sh1, x, g1,
                                            lng1, lnb1, sc2, sh2, wr, br, alpha)
            s5_i += 1
        x = _moe(x, h2, idx_t, gate_t, rank_t, cnt, g2, lng2, lnb2, moe_w_gu[layer], moe_b_gu[layer][:, None, :],
                 moe_w_down[layer], moe_b_down[layer][:, None, :], alpha)

    y_prompt = x[:n_p * l_p].reshape(n_p, l_p, d)
    y_sample = x[n_p * l_p:].reshape(n_s, l_s, d)
    return (y_prompt, y_sample, jnp.stack(new_ret, axis=1), jnp.stack(new_gla, axis=1),
            jnp.stack(new_s5_re, axis=1), jnp.stack(new_s5_im, axis=1))
```

```python
import functools
import math

import numpy as np
import jax
import jax.numpy as jnp
from jax import lax
from jax.experimental import pallas as pl
from jax.experimental.pallas import tpu as pltpu

F32 = jnp.float32
BF16 = jnp.bfloat16

RET_HEADS = 4
GLA_HEADS = 4
GLA_TAU = 16.0
GRID_W = 64
ROPE_BASE = 10000.0
S5_GROUP = 16
TOP_K = 4
SWIGLU_ALPHA = 1.702
SWIGLU_LIMIT = 7.0
LN_EPS = 1e-5
GN_EPS = 1e-5

ROW_BLOCK = 256
GLA_CHUNK = 64
S5_CHUNK = 16
MOE_BLOCK = 512
ROUTER_PAD = 128
VMEM_LIMIT = 56 * 1024 * 1024

M_START, M_END, M_SAMPLE, M_SIDX, M_PIDX, M_POS, M_SEQ = range(7)
M_ROWS = 7

NT = (((1,), (1,)), ((), ()))
TN = (((0,), (0,)), ((), ()))


def _cparams(sem):
    return pltpu.CompilerParams(dimension_semantics=sem, vmem_limit_bytes=VMEM_LIMIT)


def _silu(x):
    return x * jax.nn.sigmoid(x)


def _dot(a, b):
    return jnp.dot(a, b, preferred_element_type=F32)


def _mod_kernel(c_ref, w_ref, b_ref, o_ref):
    c = c_ref[...]
    s = _silu(c).astype(BF16)
    o_ref[0] = _dot(s, w_ref[0].astype(BF16)) + b_ref[0]


def _mod_all(conds, w_mod, b_mod):
    depth, d, n6 = w_mod.shape
    nc = conds.shape[0]
    tn = 1536
    return pl.pallas_call(
        _mod_kernel,
        out_shape=jax.ShapeDtypeStruct((depth, nc, n6), F32),
        grid=(depth, n6 // tn),
        in_specs=[pl.BlockSpec((nc, d), lambda l, j: (0, 0)),
                  pl.BlockSpec((1, d, tn), lambda l, j: (l, 0, j)),
                  pl.BlockSpec((1, 1, tn), lambda l, j: (l, 0, j))],
        out_specs=pl.BlockSpec((1, nc, tn), lambda l, j: (l, 0, j)),
        compiler_params=_cparams(("parallel", "parallel")),
    )(conds, w_mod, b_mod.reshape(depth, 1, n6))


def _proj_kernel(x_ref, sc_ref, sh_ref, w_ref, o_ref, h_scr):
    @pl.when(pl.program_id(1) == 0)
    def _():
        h_scr[...] = (x_ref[...] * (1.0 + sc_ref[...]) + sh_ref[...]).astype(BF16)

    o_ref[...] = _dot(h_scr[...], w_ref[...]).astype(o_ref.dtype)


def _proj(x, sc, sh, w, tm, tn, out_dtype=BF16):
    n, d = x.shape
    nout = w.shape[1]
    kb = tm // ROW_BLOCK
    return pl.pallas_call(
        _proj_kernel,
        out_shape=jax.ShapeDtypeStruct((n, nout), out_dtype),
        grid=(n // tm, nout // tn),
        in_specs=[pl.BlockSpec((tm, d), lambda i, j: (i, 0)),
                  pl.BlockSpec((None, 1, d), lambda i, j: (i * kb, 0, 0)),
                  pl.BlockSpec((None, 1, d), lambda i, j: (i * kb, 0, 0)),
                  pl.BlockSpec((d, tn), lambda i, j: (0, j))],
        out_specs=pl.BlockSpec((tm, tn), lambda i, j: (i, j)),
        scratch_shapes=[pltpu.VMEM((tm, d), BF16)],
        compiler_params=_cparams(("parallel", "arbitrary")),
        name="proj",
    )(x, sc, sh, w)


def _layer_norm(z, g, b):
    mu = jnp.mean(z, axis=-1, keepdims=True)
    zc = z - mu
    var = jnp.mean(zc * zc, axis=-1, keepdims=True)
    return zc * lax.rsqrt(var + LN_EPS) * g + b


def _route(h2, wr_ref, br_ref, cnt_scr, idx_ref, gate_ref, rank_ref, cnt_ref):
    logits = _dot(h2.astype(BF16), wr_ref[...]) + br_ref[...]
    tm, width = logits.shape
    col = lax.broadcasted_iota(jnp.int32, (tm, width), 1)
    colf = col.astype(F32)
    l = logits
    vals, idxs = [], []
    for _ in range(TOP_K):
        m = jnp.max(l, axis=-1, keepdims=True)
        ix = jnp.min(jnp.where(l == m, colf, float(width)), axis=-1, keepdims=True)
        vals.append(m)
        idxs.append(ix)
        l = jnp.where(colf == ix, -jnp.inf, l)
    es = [jnp.exp(v - vals[0]) for v in vals]
    den = es[0] + es[1] + es[2] + es[3]
    onehot = jnp.where(l == -jnp.inf, 1.0, 0.0)

    @pl.when(pl.program_id(0) == 0)
    def _():
        cnt_scr[...] = jnp.zeros_like(cnt_scr)

    rn = lax.broadcasted_iota(jnp.int32, (tm, tm), 0)
    rm = lax.broadcasted_iota(jnp.int32, (tm, tm), 1)
    earlier = jnp.where(rn > rm, 1.0, 0.0).astype(BF16)
    cum = _dot(earlier, onehot.astype(BF16)) + cnt_scr[...]
    idx_t = jnp.zeros((tm, width), jnp.int32)
    rank_t = jnp.zeros((tm, width), jnp.int32)
    gate_t = jnp.zeros((tm, width), F32)
    for k in range(TOP_K):
        rank_k = jnp.sum(jnp.where(colf == idxs[k], cum, 0.0), axis=-1, keepdims=True)
        idx_t = jnp.where(col == k, idxs[k].astype(jnp.int32), idx_t)
        rank_t = jnp.where(col == k, rank_k.astype(jnp.int32), rank_t)
        gate_t = jnp.where(col == k, es[k] / den, gate_t)
    idx_ref[...] = idx_t
    gate_ref[...] = gate_t
    rank_ref[...] = rank_t
    cnt_new = cnt_scr[...] + jnp.sum(onehot, axis=0, keepdims=True)
    cnt_scr[...] = cnt_new
    cnt_ref[...] = cnt_new


def _post_sublayer(alpha, x, y, g, lng, lnb):
    return _layer_norm(alpha * x + g * y, lng, lnb)


def _mixer_out_kernel(a_ref, w_ref, x_ref, g_ref, lng_ref, lnb_ref, sc2_ref, sh2_ref,
                      wr_ref, br_ref, xo_ref, h2_ref, idx_ref, gate_ref, rank_ref, cnt_ref, cnt_scr, *, alpha):
    y = _dot(a_ref[...], w_ref[...])
    xn = _post_sublayer(alpha, x_ref[...], y, g_ref[...], lng_ref[...], lnb_ref[...])
    xo_ref[...] = xn
    h2 = xn * (1.0 + sc2_ref[...]) + sh2_ref[...]
    h2_ref[...] = h2
    _route(h2, wr_ref, br_ref, cnt_scr, idx_ref, gate_ref, rank_ref, cnt_ref)


def _glu_out_kernel(y_ref, w1_ref, w2_ref, d_ref, sc1_ref, sh1_ref, x_ref, g_ref, lng_ref, lnb_ref,
                    sc2_ref, sh2_ref, wr_ref, br_ref, xo_ref, h2_ref, idx_ref, gate_ref, rank_ref, cnt_ref,
                    cnt_scr, *, alpha):
    x = x_ref[...]
    h = x * (1.0 + sc1_ref[...]) + sh1_ref[...]
    a = jax.nn.gelu(y_ref[...] + d_ref[...] * h).astype(BF16)
    y = _dot(a, w1_ref[...]) * jax.nn.sigmoid(_dot(a, w2_ref[...]))
    xn = _post_sublayer(alpha, x, y, g_ref[...], lng_ref[...], lnb_ref[...])
    xo_ref[...] = xn
    h2 = xn * (1.0 + sc2_ref[...]) + sh2_ref[...]
    h2_ref[...] = h2
    _route(h2, wr_ref, br_ref, cnt_scr, idx_ref, gate_ref, rank_ref, cnt_ref)


def _row_spec(tm, width):
    return pl.BlockSpec((tm, width), lambda i: (i, 0))


def _blk_vec_spec(tm, d):
    kb = tm // ROW_BLOCK
    return pl.BlockSpec((None, 1, d), lambda i: (i * kb, 0, 0))


def _const_spec(shape):
    return pl.BlockSpec(shape, lambda i: tuple(0 for _ in shape))


def _post_out_shapes(n, d):
    return (jax.ShapeDtypeStruct((n, d), F32), jax.ShapeDtypeStruct((n, d), F32),
            jax.ShapeDtypeStruct((n, ROUTER_PAD), jnp.int32), jax.ShapeDtypeStruct((n, ROUTER_PAD), F32),
            jax.ShapeDtypeStruct((n, ROUTER_PAD), jnp.int32), jax.ShapeDtypeStruct((1, ROUTER_PAD), F32))


def _post_out_specs(tm, d):
    return (_row_spec(tm, d), _row_spec(tm, d), _row_spec(tm, ROUTER_PAD), _row_spec(tm, ROUTER_PAD),
            _row_spec(tm, ROUTER_PAD), _const_spec((1, ROUTER_PAD)))


def _mixer_out(a, w, x, g1, lng, lnb, sc2, sh2, wr, br, alpha, tm=512):
    n, d = x.shape
    kin = a.shape[1]
    return pl.pallas_call(
        functools.partial(_mixer_out_kernel, alpha=alpha),
        out_shape=_post_out_shapes(n, d),
        grid=(n // tm,),
        in_specs=[_row_spec(tm, kin), _const_spec((kin, d)), _row_spec(tm, d), _blk_vec_spec(tm, d),
                  _const_spec((1, d)), _const_spec((1, d)), _blk_vec_spec(tm, d), _blk_vec_spec(tm, d),
                  _const_spec((d, ROUTER_PAD)), _const_spec((1, ROUTER_PAD))],
        out_specs=_post_out_specs(tm, d),
        scratch_shapes=[pltpu.VMEM((1, ROUTER_PAD), F32)],
        compiler_params=_cparams(("arbitrary",)),
        name="mixer_out_route",
    )(a, w, x, g1, lng, lnb, sc2, sh2, wr, br)


def _glu_out(y, w1, w2, dskip, sc1, sh1, x, g1, lng, lnb, sc2, sh2, wr, br, alpha, tm=512):
    n, d = x.shape
    return pl.pallas_call(
        functools.partial(_glu_out_kernel, alpha=alpha),
        out_shape=_post_out_shapes(n, d),
        grid=(n // tm,),
        in_specs=[_row_spec(tm, d), _const_spec((d, d)), _const_spec((d, d)), _const_spec((1, d)),
                  _blk_vec_spec(tm, d), _blk_vec_spec(tm, d), _row_spec(tm, d), _blk_vec_spec(tm, d),
                  _const_spec((1, d)), _const_spec((1, d)), _blk_vec_spec(tm, d), _blk_vec_spec(tm, d),
                  _const_spec((d, ROUTER_PAD)), _const_spec((1, ROUTER_PAD))],
        out_specs=_post_out_specs(tm, d),
        scratch_shapes=[pltpu.VMEM((1, ROUTER_PAD), F32)],
        compiler_params=_cparams(("arbitrary",)),
        name="mixer_out_route",
    )(y, w1, w2, dskip, sc1, sh1, x, g1, lng, lnb, sc2, sh2, wr, br)


def _rotary(x, cos, sin):
    half = x.shape[-1] // 2
    x1, x2 = x[:, :half], x[:, half:]
    return jnp.concatenate([x1 * cos - x2 * sin, x1 * sin + x2 * cos], axis=-1)


def _head_norm(o):
    mu = jnp.mean(o, axis=-1, keepdims=True)
    oc = o - mu
    var = jnp.mean(oc * oc, axis=-1, keepdims=True)
    return oc * lax.rsqrt(var + GN_EPS)


def _ret_fwd_kernel(meta_ref, lg_ref, q_ref, k_ref, v_ref, cos_ref, sin_ref, s0_ref,
                    st_in_ref, o_ref, st_ref, s_scr, *, nb, heads):
    del st_in_ref
    i = pl.program_id(0)
    start = meta_ref[M_START * nb + i]
    end = meta_ref[M_END * nb + i]
    sample = meta_ref[M_SAMPLE * nb + i]
    c = q_ref.shape[0]
    dk = q_ref.shape[1] // heads
    dv = v_ref.shape[1] // heads

    @pl.when(jnp.logical_and(start == 1, sample == 1))
    def _():
        s_scr[...] = s0_ref[...]

    @pl.when(jnp.logical_and(start == 1, sample == 0))
    def _():
        s_scr[...] = jnp.zeros_like(s_scr)

    cos = cos_ref[...]
    sin = sin_ref[...]
    rn = lax.broadcasted_iota(jnp.int32, (c, c), 0)
    rm = lax.broadcasted_iota(jnp.int32, (c, c), 1)
    rel = (rn - rm).astype(F32)
    idx = lax.broadcasted_iota(jnp.int32, (c, 1), 0).astype(F32)
    for h in range(heads):
        lgf = lg_ref[h]
        lgb = lg_ref[heads + h]
        q = _rotary(q_ref[:, h * dk:(h + 1) * dk].astype(F32), cos, sin) * (dk ** -0.5)
        k = _rotary(k_ref[:, h * dk:(h + 1) * dk].astype(F32), cos, sin)
        qb = q.astype(BF16)
        v = v_ref[:, h * dv:(h + 1) * dv]
        dmat = (jnp.where(rel >= 0, jnp.exp(lgf * jnp.maximum(rel, 0.0)), 0.0)
                + jnp.where(rel <= 0, jnp.exp(lgb * jnp.maximum(-rel, 0.0)), 0.0))
        scores = lax.dot_general(qb, k.astype(BF16), NT, preferred_element_type=F32) * dmat
        xi = jnp.exp(lgf * (idx + 1.0))
        zeta = jnp.exp(lgf * (c - 1.0 - idx))
        s_prev = s_scr[h]
        o_ref[:, h * dv:(h + 1) * dv] = _dot(scores.astype(BF16), v) + _dot(qb, s_prev.astype(BF16)) * xi
        g_chunk = jnp.exp(lgf * jnp.full((1, 1), float(c), F32))
        s_scr[h] = s_prev * g_chunk + lax.dot_general((k * zeta).astype(BF16), v, TN,
                                                       preferred_element_type=F32)

    @pl.when(jnp.logical_and(end == 1, sample == 0))
    def _():
        st_ref[...] = s_scr[...]


def _ret_bwd_kernel(meta_ref, lg_ref, q_ref, k_ref, v_ref, g_ref, cos_ref, sin_ref, s0_ref, of_ref,
                    st_in_ref, o_ref, st_ref, s_scr, *, nb, heads):
    del st_in_ref
    i = nb - 1 - pl.program_id(0)
    start = meta_ref[M_START * nb + i]
    end = meta_ref[M_END * nb + i]
    sample = meta_ref[M_SAMPLE * nb + i]
    c = q_ref.shape[0]
    dk = q_ref.shape[1] // heads
    dv = v_ref.shape[1] // heads

    @pl.when(jnp.logical_and(end == 1, sample == 1))
    def _():
        s_scr[...] = s0_ref[...]

    @pl.when(jnp.logical_and(end == 1, sample == 0))
    def _():
        s_scr[...] = jnp.zeros_like(s_scr)

    cos = cos_ref[...]
    sin = sin_ref[...]
    idx = lax.broadcasted_iota(jnp.int32, (c, 1), 0).astype(F32)
    for h in range(heads):
        lgb = lg_ref[heads + h]
        q = _rotary(q_ref[:, h * dk:(h + 1) * dk].astype(F32), cos, sin) * (dk ** -0.5)
        k = _rotary(k_ref[:, h * dk:(h + 1) * dk].astype(F32), cos, sin)
        v = v_ref[:, h * dv:(h + 1) * dv]
        xi = jnp.exp(lgb * (c - idx))
        zeta = jnp.exp(lgb * idx)
        s_prev = s_scr[h]
        o = of_ref[:, h * dv:(h + 1) * dv] + _dot(q.astype(BF16), s_prev.astype(BF16)) * xi
        g_chunk = jnp.exp(lgb * jnp.full((1, 1), float(c), F32))
        s_scr[h] = s_prev * g_chunk + lax.dot_general((k * zeta).astype(BF16), v, TN,
                                                       preferred_element_type=F32)
        gate = _silu(g_ref[:, h * dv:(h + 1) * dv].astype(F32))
        o_ref[:, h * dv:(h + 1) * dv] = (gate * _head_norm(o)).astype(o_ref.dtype)

    @pl.when(jnp.logical_and(start == 1, sample == 0))
    def _():
        st_ref[...] = s_scr[...]


def _retention(qkvg, meta, lg, cos_t, sin_t, state, new_state, li, nb):
    n = qkvg.shape[0]
    heads = RET_HEADS
    d = qkvg.shape[1] // 6
    dk, dv = d // heads, 2 * d // heads
    c = ROW_BLOCK
    kw = dict(nb=nb, heads=heads)
    r = lambda j: nb - 1 - j
    pos = lambda j, m: m[M_POS * nb + j]
    st_blk = (None, None, None, heads, dk, dv)
    any_spec = pl.BlockSpec(memory_space=pl.ANY)
    of, new_state = pl.pallas_call(
        functools.partial(_ret_fwd_kernel, **kw),
        out_shape=(jax.ShapeDtypeStruct((n, heads * dv), F32),
                   jax.ShapeDtypeStruct(new_state.shape, new_state.dtype)),
        grid_spec=pltpu.PrefetchScalarGridSpec(
            num_scalar_prefetch=1, grid=(nb,),
            in_specs=[pl.BlockSpec(memory_space=pltpu.SMEM),
                      pl.BlockSpec((c, d), lambda j, m: (j, 0)),
                      pl.BlockSpec((c, d), lambda j, m: (j, 1)),
                      pl.BlockSpec((c, 2 * d), lambda j, m: (j, 1)),
                      pl.BlockSpec((c, dk // 2), lambda j, m: (pos(j, m), 0)),
                      pl.BlockSpec((c, dk // 2), lambda j, m: (pos(j, m), 0)),
                      pl.BlockSpec(st_blk, lambda j, m: (m[M_SIDX * nb + j], li, 0, 0, 0, 0)),
                      any_spec],
            out_specs=(pl.BlockSpec((c, 2 * d), lambda j, m: (j, 0)),
                       pl.BlockSpec(st_blk, lambda j, m: (m[M_PIDX * nb + j], li, 0, 0, 0, 0))),
            scratch_shapes=[pltpu.VMEM((heads, dk, dv), F32)]),
        input_output_aliases={8: 1},
        compiler_params=_cparams(("arbitrary",)),
        name="ret_fwd",
    )(meta, lg, qkvg, qkvg, qkvg, cos_t, sin_t, state, new_state)
    out, new_state = pl.pallas_call(
        functools.partial(_ret_bwd_kernel, **kw),
        out_shape=(jax.ShapeDtypeStruct((n, heads * dv), BF16),
                   jax.ShapeDtypeStruct(new_state.shape, new_state.dtype)),
        grid_spec=pltpu.PrefetchScalarGridSpec(
            num_scalar_prefetch=1, grid=(nb,),
            in_specs=[pl.BlockSpec(memory_space=pltpu.SMEM),
                      pl.BlockSpec((c, d), lambda j, m: (r(j), 0)),
                      pl.BlockSpec((c, d), lambda j, m: (r(j), 1)),
                      pl.BlockSpec((c, 2 * d), lambda j, m: (r(j), 1)),
                      pl.BlockSpec((c, 2 * d), lambda j, m: (r(j), 2)),
                      pl.BlockSpec((c, dk // 2), lambda j, m: (pos(r(j), m), 0)),
                      pl.BlockSpec((c, dk // 2), lambda j, m: (pos(r(j), m), 0)),
                      pl.BlockSpec(st_blk, lambda j, m: (m[M_SIDX * nb + r(j)], li, 1, 0, 0, 0)),
                      pl.BlockSpec((c, 2 * d), lambda j, m: (r(j), 0)),
                      any_spec],
            out_specs=(pl.BlockSpec((c, 2 * d), lambda j, m: (r(j), 0)),
                       pl.BlockSpec(st_blk, lambda j, m: (m[M_PIDX * nb + r(j)], li, 1, 0, 0, 0))),
            scratch_shapes=[pltpu.VMEM((heads, dk, dv), F32)]),
        input_output_aliases={10: 1},
        compiler_params=_cparams(("arbitrary",)),
        name="ret_bwd",
    )(meta, lg, qkvg, qkvg, qkvg, qkvg, cos_t, sin_t, state, of, new_state)
    return out, new_state


def _gla_gate_kernel(x_ref, sc_ref, sh_ref, w1_ref, w2_ref, b_ref, o_ref):
    h = (x_ref[...] * (1.0 + sc_ref[...]) + sh_ref[...]).astype(BF16)
    u = _dot(h, w1_ref[...]).astype(BF16)
    z = _dot(u, w2_ref[...]) + b_ref[...]
    log_sig = jnp.minimum(z, 0.0) - jnp.log(1.0 + jnp.exp(-jnp.abs(z)))
    o_ref[...] = log_sig / GLA_TAU


def _gla_gate(x, sc, sh, w1, w2, b, tm=512):
    n, d = x.shape
    r2 = w1.shape[1]
    nk = w2.shape[1]
    return pl.pallas_call(
        _gla_gate_kernel,
        out_shape=jax.ShapeDtypeStruct((n, nk), F32),
        grid=(n // tm,),
        in_specs=[_row_spec(tm, d), _blk_vec_spec(tm, d), _blk_vec_spec(tm, d), _const_spec((d, r2)),
                  _const_spec((r2, nk)), _const_spec((1, nk))],
        out_specs=_row_spec(tm, nk),
        compiler_params=_cparams(("parallel",)),
    )(x, sc, sh, w1, w2, b)


def _cumsum_rows(tri, x):
    hi = x.astype(BF16)
    r1 = x - hi.astype(F32)
    mid = r1.astype(BF16)
    lo = (r1 - mid.astype(F32)).astype(BF16)
    return _dot(tri, hi) + _dot(tri, mid) + _dot(tri, lo)


def _gla_chunk(q, k, v, la, s_t, tri, mask, mid_row, last_row):
    b = _cumsum_rows(tri, la)
    r = b[mid_row:mid_row + 1, :]
    bl = b[last_row:last_row + 1, :]
    qe = (q * jnp.exp(b - r)).astype(BF16)
    ke = (k * jnp.exp(r - b)).astype(BF16)
    qi = (q * jnp.exp(b)).astype(BF16)
    kd = (k * jnp.exp(bl - b)).astype(BF16)
    sc = lax.dot_general(qe, ke, NT, preferred_element_type=F32)
    sc = jnp.where(mask, sc, 0.0).astype(BF16)
    o = _dot(sc, v) + lax.dot_general(qi, s_t.astype(BF16), NT, preferred_element_type=F32)
    s_new = s_t * jnp.exp(bl) + lax.dot_general(v, kd, TN, preferred_element_type=F32)
    return o, s_new


def _gla_fwd_kernel(meta_ref, q_ref, k_ref, v_ref, la_ref, s0_ref, o_ref, st_ref, s_scr, *, nb, heads):
    i = pl.program_id(0)
    start = meta_ref[M_START * nb + i]
    end = meta_ref[M_END * nb + i]
    sample = meta_ref[M_SAMPLE * nb + i]
    dk = q_ref.shape[1] // heads
    dv = v_ref.shape[1] // heads
    cc = GLA_CHUNK

    @pl.when(jnp.logical_and(start == 1, sample == 1))
    def _():
        for h in range(heads):
            s_scr[h] = s0_ref[h].T

    @pl.when(jnp.logical_and(start == 1, sample == 0))
    def _():
        s_scr[...] = jnp.zeros_like(s_scr)

    rn = lax.broadcasted_iota(jnp.int32, (cc, cc), 0)
    rm = lax.broadcasted_iota(jnp.int32, (cc, cc), 1)
    mask = rn >= rm
    tri = jnp.where(mask, 1.0, 0.0).astype(BF16)
    for h in range(heads):
        ks = slice(h * dk, (h + 1) * dk)
        vs = slice(h * dv, (h + 1) * dv)
        s_t = s_scr[h]
        for ci in range(q_ref.shape[0] // cc):
            sl = slice(ci * cc, (ci + 1) * cc)
            q = q_ref[sl, ks].astype(F32) * (dk ** -0.5)
            o, s_t = _gla_chunk(q, k_ref[sl, ks].astype(F32), v_ref[sl, vs], la_ref[sl, ks], s_t, tri, mask,
                                cc // 2 - 1, cc - 1)
            o_ref[sl, vs] = o
        s_scr[h] = s_t

    @pl.when(jnp.logical_and(end == 1, sample == 0))
    def _():
        for h in range(heads):
            st_ref[h] = s_scr[h].T


def _gla_bwd_kernel(meta_ref, q_ref, k_ref, v_ref, r_ref, la_ref, s0_ref, of_ref, o_ref, st_ref, s_scr,
                    *, nb, heads):
    i = nb - 1 - pl.program_id(0)
    start = meta_ref[M_START * nb + i]
    end = meta_ref[M_END * nb + i]
    sample = meta_ref[M_SAMPLE * nb + i]
    dk = q_ref.shape[1] // heads
    dv = v_ref.shape[1] // heads
    cc = GLA_CHUNK

    @pl.when(jnp.logical_and(end == 1, sample == 1))
    def _():
        for h in range(heads):
            s_scr[h] = s0_ref[h].T

    @pl.when(jnp.logical_and(end == 1, sample == 0))
    def _():
        s_scr[...] = jnp.zeros_like(s_scr)

    rn = lax.broadcasted_iota(jnp.int32, (cc, cc), 0)
    rm = lax.broadcasted_iota(jnp.int32, (cc, cc), 1)
    mask = rn <= rm
    tri = jnp.where(mask, 1.0, 0.0).astype(BF16)
    for h in range(heads):
        ks = slice(h * dk, (h + 1) * dk)
        vs = slice(h * dv, (h + 1) * dv)
        s_t = s_scr[h]
        for ci in reversed(range(q_ref.shape[0] // cc)):
            sl = slice(ci * cc, (ci + 1) * cc)
            q = q_ref[sl, ks].astype(F32) * (dk ** -0.5)
            ob, s_t = _gla_chunk(q, k_ref[sl, ks].astype(F32), v_ref[sl, vs], la_ref[sl, ks], s_t, tri, mask,
                                 cc // 2, 0)
            o = of_ref[sl, vs] + ob
            o_ref[sl, vs] = (_silu(r_ref[sl, vs].astype(F32)) * _head_norm(o)).astype(o_ref.dtype)
        s_scr[h] = s_t

    @pl.when(jnp.logical_and(start == 1, sample == 0))
    def _():
        for h in range(heads):
            st_ref[h] = s_scr[h].T


def _gla(qkvr, la, meta, state, li, nb, n_prompt):
    n = qkvr.shape[0]
    heads = GLA_HEADS
    d = qkvr.shape[1] // 3
    dk, dv = d // 2 // heads, d // heads
    c = ROW_BLOCK
    kw = dict(nb=nb, heads=heads)
    r = lambda j: nb - 1 - j
    st_blk = (None, None, None, heads, dk, dv)
    of, st_f = pl.pallas_call(
        functools.partial(_gla_fwd_kernel, **kw),
        out_shape=(jax.ShapeDtypeStruct((n, d), F32), jax.ShapeDtypeStruct((n_prompt, heads, dk, dv), F32)),
        grid_spec=pltpu.PrefetchScalarGridSpec(
            num_scalar_prefetch=1, grid=(nb,),
            in_specs=[pl.BlockSpec((c, d // 2), lambda j, m: (j, 0)),
                      pl.BlockSpec((c, d // 2), lambda j, m: (j, 1)),
                      pl.BlockSpec((c, d), lambda j, m: (j, 1)),
                      pl.BlockSpec((c, d // 2), lambda j, m: (j, 0)),
                      pl.BlockSpec(st_blk, lambda j, m: (m[M_SIDX * nb + j], li, 0, 0, 0, 0))],
            out_specs=(pl.BlockSpec((c, d), lambda j, m: (j, 0)),
                       pl.BlockSpec((None, heads, dk, dv), lambda j, m: (m[M_PIDX * nb + j], 0, 0, 0))),
            scratch_shapes=[pltpu.VMEM((heads, dv, dk), F32)]),
        compiler_params=_cparams(("arbitrary",)),
        name="gla_fwd",
    )(meta, qkvr, qkvr, qkvr, la, state)
    out, st_b = pl.pallas_call(
        functools.partial(_gla_bwd_kernel, **kw),
        out_shape=(jax.ShapeDtypeStruct((n, d), BF16), jax.ShapeDtypeStruct((n_prompt, heads, dk, dv), F32)),
        grid_spec=pltpu.PrefetchScalarGridSpec(
            num_scalar_prefetch=1, grid=(nb,),
            in_specs=[pl.BlockSpec((c, d // 2), lambda j, m: (r(j), 0)),
                      pl.BlockSpec((c, d // 2), lambda j, m: (r(j), 1)),
                      pl.BlockSpec((c, d), lambda j, m: (r(j), 1)),
                      pl.BlockSpec((c, d), lambda j, m: (r(j), 2)),
                      pl.BlockSpec((c, d // 2), lambda j, m: (r(j), 1)),
                      pl.BlockSpec(st_blk, lambda j, m: (m[M_SIDX * nb + r(j)], li, 1, 0, 0, 0)),
                      pl.BlockSpec((c, d), lambda j, m: (r(j), 0))],
            out_specs=(pl.BlockSpec((c, d), lambda j, m: (r(j), 0)),
                       pl.BlockSpec((None, heads, dk, dv), lambda j, m: (m[M_PIDX * nb + r(j)], 0, 0, 0))),
            scratch_shapes=[pltpu.VMEM((heads, dv, dk), F32)]),
        compiler_params=_cparams(("arbitrary",)),
        name="gla_bwd",
    )(meta, qkvr, qkvr, qkvr, qkvr, la, state, of)
    return out, jnp.stack([st_f, st_b], axis=1)


def _modulate_kernel(x_ref, sc_ref, sh_ref, o_ref):
    o_ref[...] = (x_ref[...] * (1.0 + sc_ref[...]) + sh_ref[...]).astype(o_ref.dtype)


def _modulate(x, sc, sh, tm=512):
    n, d = x.shape
    return pl.pallas_call(
        _modulate_kernel,
        out_shape=jax.ShapeDtypeStruct((n, d), BF16),
        grid=(n // tm,),
        in_specs=[_row_spec(tm, d), _blk_vec_spec(tm, d), _blk_vec_spec(tm, d)],
        out_specs=_row_spec(tm, d),
        compiler_params=_cparams(("parallel",)),
    )(x, sc, sh)


def _s5_in_kernel(u_ref, w_ref, wf_ref, wb_ref):
    z = _dot(u_ref[...], w_ref[...])
    half = z.shape[1] // 2
    wf_ref[...] = z[:, :half]
    wb_ref[...] = z[:, half:]


def _s5_in(u, w_in, tr):
    g, r, kc = u.shape
    p2 = w_in.shape[2] // 2
    return pl.pallas_call(
        _s5_in_kernel,
        out_shape=(jax.ShapeDtypeStruct((r, g * p2), F32), jax.ShapeDtypeStruct((r, g * p2), F32)),
        grid=(g, r // tr),
        in_specs=[pl.BlockSpec((None, tr, kc), lambda gi, ri: (gi, ri, 0)),
                  pl.BlockSpec((None, kc, 2 * p2), lambda gi, ri: (gi, 0, 0))],
        out_specs=(pl.BlockSpec((tr, p2), lambda gi, ri: (ri, gi)),
                   pl.BlockSpec((tr, p2), lambda gi, ri: (ri, gi))),
        compiler_params=_cparams(("parallel", "parallel")),
        name="s5_in",
    )(u, w_in)


def _s5_scan_kernel(meta_ref, w_ref, a1_ref, a2_ref, s0_ref, xp_ref, st_ref, x_scr, *, nb, reverse):
    j = pl.program_id(0)
    i = nb - 1 - j if reverse else j
    first = meta_ref[(M_END if reverse else M_START) * nb + i]
    last = meta_ref[(M_START if reverse else M_END) * nb + i]

    @pl.when(first == 1)
    def _():
        x_scr[...] = s0_ref[...]

    a1 = a1_ref[...]
    a2 = a2_ref[...]
    x = x_scr[...]
    half = x.shape[1] // 2
    nrow = w_ref.shape[0]
    order = range(nrow - 1, -1, -1) if reverse else range(nrow)
    for c in order:
        xp_ref[c] = x
        x = x * a1 + pltpu.roll(x, half, axis=1) * a2 + w_ref[c]
    x_scr[...] = x

    @pl.when(last == 1)
    def _():
        st_ref[...] = x


def _s5_scan(w, a1, a2, s0, meta, nb, nseq, reverse):
    r, g, p2 = w.shape
    cr = r // nb
    blk = (lambda j: nb - 1 - j) if reverse else (lambda j: j)
    return pl.pallas_call(
        functools.partial(_s5_scan_kernel, nb=nb, reverse=reverse),
        out_shape=(jax.ShapeDtypeStruct((r, g, p2), F32), jax.ShapeDtypeStruct((nseq, g, p2), F32)),
        grid_spec=pltpu.PrefetchScalarGridSpec(
            num_scalar_prefetch=1, grid=(nb,),
            in_specs=[pl.BlockSpec((cr, g, p2), lambda j, m: (blk(j), 0, 0)),
                      pl.BlockSpec((g, p2), lambda j, m: (0, 0)),
                      pl.BlockSpec((g, p2), lambda j, m: (0, 0)),
                      pl.BlockSpec((None, g, p2), lambda j, m: (m[M_SEQ * nb + blk(j)], 0, 0))],
            out_specs=(pl.BlockSpec((cr, g, p2), lambda j, m: (blk(j), 0, 0)),
                       pl.BlockSpec((None, g, p2), lambda j, m: (m[M_SEQ * nb + blk(j)], 0, 0))),
            scratch_shapes=[pltpu.VMEM((g, p2), F32)]),
        compiler_params=_cparams(("arbitrary",)),
        name="s5_scan",
    )(meta, w, a1, a2, s0)


def _s5_out_kernel(u_ref, m_ref, xf_ref, cf_ref, xb_ref, cb_ref, y_ref):
    y_ref[...] = (_dot(u_ref[...], m_ref[...]) + _dot(xf_ref[...].astype(BF16), cf_ref[...])
                  + _dot(xb_ref[...].astype(BF16), cb_ref[...]))


def _s5_out(u, m_intra, xf, cf, xb, cb, tr):
    g, r, kc = u.shape
    p2 = cf.shape[1]
    return pl.pallas_call(
        _s5_out_kernel,
        out_shape=jax.ShapeDtypeStruct((g, r, kc), F32),
        grid=(g, r // tr),
        in_specs=[pl.BlockSpec((None, tr, kc), lambda gi, ri: (gi, ri, 0)),
                  pl.BlockSpec((None, kc, kc), lambda gi, ri: (gi, 0, 0)),
                  pl.BlockSpec((tr, p2), lambda gi, ri: (ri, gi)),
                  pl.BlockSpec((None, p2, kc), lambda gi, ri: (gi, 0, 0)),
                  pl.BlockSpec((tr, p2), lambda gi, ri: (ri, gi)),
                  pl.BlockSpec((None, p2, kc), lambda gi, ri: (gi, 0, 0))],
        out_specs=pl.BlockSpec((None, tr, kc), lambda gi, ri: (gi, ri, 0)),
        compiler_params=_cparams(("parallel", "parallel")),
        name="s5_out",
    )(u, m_intra, xf, cf, xb, cb)


def _s5_tables(a_re, a_im, log_step, b_re, b_im, c_re, c_im):
    t = S5_CHUNK
    step = jnp.exp(log_step.astype(F32))[..., None]
    ar, ai = a_re.astype(F32), a_im.astype(F32)
    mag = jnp.exp(ar * step)
    ab_re, ab_im = mag * jnp.cos(ai * step), mag * jnp.sin(ai * step)
    den = ar * ar + ai * ai
    f_re = ((ab_re - 1.0) * ar + ab_im * ai) / den
    f_im = (ab_im * ar - (ab_re - 1.0) * ai) / den
    br, bi = b_re.astype(F32), b_im.astype(F32)
    bb_re = f_re[..., None] * br - f_im[..., None] * bi
    bb_im = f_re[..., None] * bi + f_im[..., None] * br
    cr, ci = c_re.astype(F32), c_im.astype(F32)
    tau = jnp.arange(t + 1, dtype=F32)[:, None, None, None]
    pmag = jnp.exp(ar[None] * step[None] * tau)
    pw_re = pmag * jnp.cos(ai[None] * step[None] * tau)
    pw_im = pmag * jnp.sin(ai[None] * step[None] * tau)
    g, p, ch = br.shape[1], br.shape[2], br.shape[3]

    def cmul(xr, xi, yr, yi):
        return xr * yr - xi * yi, xr * yi + xi * yr

    def inject(d, powers):
        pr = pw_re[powers, d][:, :, :, None]
        pi = pw_im[powers, d][:, :, :, None]
        wr, wi = cmul(pr, pi, bb_re[d][None], bb_im[d][None])
        w = jnp.concatenate([wr, wi], axis=2)
        return w.transpose(1, 0, 3, 2).reshape(g, t * ch, 2 * p)
    w_in = jnp.concatenate([inject(0, jnp.arange(t - 1, -1, -1)), inject(1, jnp.arange(t))], axis=-1)

    def readout(d, powers):
        pr = pw_re[powers, d][:, :, None, :]
        pi = pw_im[powers, d][:, :, None, :]
        kr, ki = cmul(cr[d][None], ci[d][None], pr, pi)
        m = jnp.concatenate([kr, -ki], axis=3)
        return m.transpose(1, 3, 0, 2).reshape(g, 2 * p, t * ch)
    c_f = readout(0, jnp.arange(1, t + 1))
    c_b = readout(1, jnp.arange(t, 0, -1))

    def lag_kernels(d):
        pr = pw_re[:t, d][:, :, :, None]
        pi = pw_im[:t, d][:, :, :, None]
        wr, wi = cmul(pr, pi, bb_re[d][None], bb_im[d][None])
        return (jnp.einsum('gop,tgpc->tgoc', cr[d], wr) - jnp.einsum('gop,tgpc->tgoc', ci[d], wi))
    kf, kb = lag_kernels(0), lag_kernels(1)
    ii = jnp.arange(t)
    lag = ii[None, :] - ii[:, None]
    mf = jnp.where((lag >= 0)[:, :, None, None, None], kf[jnp.clip(lag, 0, t - 1)], 0.0)
    mb = jnp.where((lag <= 0)[:, :, None, None, None], kb[jnp.clip(-lag, 0, t - 1)], 0.0)
    m = (mf + mb).transpose(2, 0, 4, 1, 3).reshape(g, t * ch, t * ch)

    at_re, at_im = pw_re[t], pw_im[t]
    a1 = jnp.concatenate([at_re, at_re], axis=-1)
    a2 = jnp.concatenate([-at_im, at_im], axis=-1)
    return w_in.astype(BF16), m.astype(BF16), c_f.astype(BF16), c_b.astype(BF16), a1, a2


def _dispatch_kernel(pend_ref, dest_ref, h_ref, xs_ref, zero_scr, sem, zsem, *, n_exp):
    tm = h_ref.shape[0]
    bm = zero_scr.shape[0]

    @pl.when(pl.program_id(0) == 0)
    def _():
        zero_scr[...] = jnp.zeros_like(zero_scr)
        for e in range(n_exp):
            prev = pend_ref[e - 1] if e else 0

            @pl.when(pend_ref[e] > prev)
            def _():
                off = pl.multiple_of(pend_ref[e] - bm, bm)
                pltpu.make_async_copy(zero_scr, xs_ref.at[pl.ds(off, bm)], zsem).start()
        n_used = pend_ref[n_exp - 1] // bm
        n_blocks = xs_ref.shape[0] // bm

        def zero_tail(b, carry):
            off = pl.multiple_of(b * bm, bm)
            pltpu.make_async_copy(zero_scr, xs_ref.at[pl.ds(off, bm)], zsem).start()
            return carry

        def wait_tail(b, carry):
            pltpu.make_async_copy(zero_scr, xs_ref.at[pl.ds(0, bm)], zsem).wait()
            return carry

        lax.fori_loop(n_used, n_blocks, zero_tail, 0)
        for e in range(n_exp):
            prev = pend_ref[e - 1] if e else 0

            @pl.when(pend_ref[e] > prev)
            def _():
                pltpu.make_async_copy(zero_scr, xs_ref.at[pl.ds(0, bm)], zsem).wait()
        lax.fori_loop(n_used, n_blocks, wait_tail, 0)

    def issue(t, carry):
        for k in range(TOP_K):
            pltpu.make_async_copy(h_ref.at[pl.ds(t, 1)], xs_ref.at[pl.ds(dest_ref[t * TOP_K + k], 1)], sem).start()
        return carry

    lax.fori_loop(0, tm, issue, 0)
    for _ in range(TOP_K):
        pltpu.make_async_copy(h_ref, xs_ref.at[pl.ds(0, tm)], sem).wait()


def _dispatch(pend, dest, h2, n_rows, tm=512):
    n, d = h2.shape
    return pl.pallas_call(
        functools.partial(_dispatch_kernel, n_exp=pend.shape[0]),
        out_shape=jax.ShapeDtypeStruct((n_rows, d), F32),
        grid_spec=pltpu.PrefetchScalarGridSpec(
            num_scalar_prefetch=1, grid=(n // tm,),
            in_specs=[pl.BlockSpec((tm * TOP_K,), lambda i, p: (i,), memory_space=pltpu.SMEM),
                      pl.BlockSpec((tm, d), lambda i, p: (i, 0))],
            out_specs=pl.BlockSpec(memory_space=pl.ANY),
            scratch_shapes=[pltpu.VMEM((MOE_BLOCK, d), F32), pltpu.SemaphoreType.DMA,
                            pltpu.SemaphoreType.DMA]),
        compiler_params=_cparams(("arbitrary",)),
        name="moe_dispatch",
    )(pend, dest, h2)


def _ffn_kernel(sp_ref, x_ref, wgu_ref, bgu_ref, wd_ref, bd_ref, o_ref, wgu_s, wd_s, *, n_blocks):
    b = pl.program_id(0)
    used = b < sp_ref[2 * n_blocks]
    dff = wd_ref.shape[0]

    @pl.when(jnp.logical_and(used, sp_ref[n_blocks + b] == 1))
    def _():
        wgu_s[...] = wgu_ref[...].astype(BF16)
        wd_s[...] = wd_ref[...].astype(BF16)

    @pl.when(used)
    def _():
        gu = _dot(x_ref[...].astype(BF16), wgu_s[...]) + bgu_ref[...]
        gate = jnp.minimum(gu[:, :dff], SWIGLU_LIMIT)
        lin = jnp.clip(gu[:, dff:], -SWIGLU_LIMIT, SWIGLU_LIMIT)
        act = gate * jax.nn.sigmoid(SWIGLU_ALPHA * gate) * (lin + 1.0)
        o_ref[...] = _dot(act.astype(BF16), wd_s[...]) + bd_ref[...]

    @pl.when(jnp.logical_not(used))
    def _():
        o_ref[...] = jnp.zeros_like(o_ref)


def _ffn(sp, xs, wgu, bgu, wd, bd):
    n_rows, d = xs.shape
    bm = MOE_BLOCK
    n_blocks = n_rows // bm
    dff = wd.shape[1]
    return pl.pallas_call(
        functools.partial(_ffn_kernel, n_blocks=n_blocks),
        out_shape=jax.ShapeDtypeStruct((n_rows, d), F32),
        grid_spec=pltpu.PrefetchScalarGridSpec(
            num_scalar_prefetch=1, grid=(n_blocks,),
            in_specs=[pl.BlockSpec((bm, d), lambda b, sp: (jnp.minimum(b, sp[2 * n_blocks] - 1), 0)),
                      pl.BlockSpec((None, d, 2 * dff), lambda b, sp: (sp[b], 0, 0)),
                      pl.BlockSpec((None, 1, 2 * dff), lambda b, sp: (sp[b], 0, 0)),
                      pl.BlockSpec((None, dff, d), lambda b, sp: (sp[b], 0, 0)),
                      pl.BlockSpec((None, 1, d), lambda b, sp: (sp[b], 0, 0))],
            out_specs=pl.BlockSpec((bm, d), lambda b, sp: (b, 0)),
            scratch_shapes=[pltpu.VMEM((d, 2 * dff), BF16), pltpu.VMEM((dff, d), BF16)]),
        compiler_params=_cparams(("arbitrary",)),
        name="moe_ffn",
    )(sp, xs, wgu, bgu, wd, bd)


def _combine_kernel(dest_ref, yb_ref, gate_ref, x_ref, g_ref, lng_ref, lnb_ref, xo_ref, buf, sem, *, alpha):
    tm = x_ref.shape[0]

    def issue(t, carry):
        for k in range(TOP_K):
            pltpu.make_async_copy(yb_ref.at[pl.ds(dest_ref[t * TOP_K + k], 1)], buf.at[k, pl.ds(t, 1)], sem).start()
        return carry

    lax.fori_loop(0, tm, issue, 0)
    for k in range(TOP_K):
        pltpu.make_async_copy(yb_ref.at[pl.ds(0, tm)], buf.at[k], sem).wait()
    gates = gate_ref[...]
    y = gates[:, 0:1] * buf[0]
    for k in range(1, TOP_K):
        y = y + gates[:, k:k + 1] * buf[k]
    xo_ref[...] = _post_sublayer(alpha, x_ref[...], y, g_ref[...], lng_ref[...], lnb_ref[...])


def _combine(dest, yb, gate_t, x, g2, lng, lnb, alpha, tm=512):
    n, d = x.shape
    return pl.pallas_call(
        functools.partial(_combine_kernel, alpha=alpha),
        out_shape=jax.ShapeDtypeStruct((n, d), F32),
        grid=(n // tm,),
        in_specs=[pl.BlockSpec((tm * TOP_K,), lambda i: (i,), memory_space=pltpu.SMEM),
                  pl.BlockSpec(memory_space=pl.ANY),
                  _row_spec(tm, ROUTER_PAD), _row_spec(tm, d), _blk_vec_spec(tm, d), _const_spec((1, d)),
                  _const_spec((1, d))],
        out_specs=_row_spec(tm, d),
        scratch_shapes=[pltpu.VMEM((TOP_K, tm, d), F32), pltpu.SemaphoreType.DMA],
        compiler_params=_cparams(("arbitrary",)),
        name="moe_combine",
    )(dest, yb, gate_t, x, g2, lng, lnb)


def _moe(x, h2, idx_t, gate_t, rank_t, cnt, g2, lng, lnb, wgu, bgu, wd, bd, alpha):
    n_tok, d = h2.shape
    n_exp = wgu.shape[0]
    bm = MOE_BLOCK
    n_blocks = -(-n_tok * TOP_K // bm) + n_exp
    counts = cnt[0, :n_exp].astype(jnp.int32)
    padded = (counts + bm - 1) // bm * bm
    pend = jnp.cumsum(padded)
    pstart = pend - padded
    e_iota = jnp.arange(n_exp, dtype=jnp.int32)
    dest = rank_t[:, :TOP_K] + jnp.sum(jnp.where(idx_t[:, :TOP_K, None] == e_iota, pstart, 0), axis=-1)
    dest = dest.reshape(-1).astype(jnp.int32)
    blk_row = jnp.arange(n_blocks, dtype=jnp.int32) * bm
    block_e = jnp.minimum(jnp.sum(pend[None, :] <= blk_row[:, None], axis=1), n_exp - 1).astype(jnp.int32)
    first = jnp.concatenate([jnp.ones((1,), jnp.int32), (block_e[1:] != block_e[:-1]).astype(jnp.int32)])
    sp = jnp.concatenate([block_e, first, (pend[-1:] // bm).astype(jnp.int32)])
    xs = _dispatch(pend.astype(jnp.int32), dest, h2, n_blocks * bm)
    yb = _ffn(sp, xs, wgu, bgu, wd, bd)
    return _combine(dest, yb, gate_t, x, g2, lng, lnb, alpha)


def _block_meta(n_p, l_p, n_s, l_s):
    rb = ROW_BLOCK
    assert l_p % rb == 0 and l_s % rb == 0
    rows = []
    for s in range(n_p):
        for j in range(l_p // rb):
            rows.append((int(j == 0), int(j == l_p // rb - 1), 0, 0, s, 0, s))
    for s in range(n_s):
        for j in range(l_s // rb):
            rows.append((int(j == 0), int(j == l_s // rb - 1), 1, s, n_p - 1, 1 + j, n_p + s))
    meta = np.asarray(rows, np.int32).T
    cond = np.asarray([0] * (n_p * l_p // rb) + [1 + s for s in range(n_s) for _ in range(l_s // rb)], np.int32)
    return meta, cond


def _rotary_tables(l_s, dk):
    rows = l_s // GRID_W
    row = jnp.repeat(jnp.arange(rows, dtype=F32), GRID_W)
    col = jnp.tile(jnp.arange(GRID_W, dtype=F32), rows)
    n_freq = dk // 4
    inv = ROPE_BASE ** (-jnp.arange(n_freq, dtype=F32) / n_freq)
    ang = jnp.concatenate([row[:, None] * inv, col[:, None] * inv], axis=-1)
    ident = jnp.zeros((ROW_BLOCK, dk // 2), F32)
    return (jnp.concatenate([ident + 1.0, jnp.cos(ang)], axis=0),
            jnp.concatenate([ident, jnp.sin(ang)], axis=0))


def kernel(x_prompt, x_sample, state_ret, state_gla, state_s5_re, state_s5_im, c, c_ctx, w_mod, b_mod, ln_g, ln_b, ret_w_in, ret_decay, ret_w_out, gla_w_in, gla_w_a1, gla_w_a2, gla_b_a, gla_w_out, s5_a_re, s5_a_im, s5_log_step, s5_b_re, s5_b_im, s5_c_re, s5_c_im, s5_d, s5_w_glu, moe_w_router, moe_b_router, moe_w_gu, moe_b_gu, moe_w_down, moe_b_down):
    n_p, l_p, d = x_prompt.shape
    n_s, l_s, _ = x_sample.shape
    depth = w_mod.shape[0]
    n_exp = moe_w_router.shape[-1]
    alpha = (2 * depth) ** 0.25
    rb = ROW_BLOCK
    n = n_p * l_p + n_s * l_s
    nb = n // rb
    nseq = n_p + n_s
    meta_np, cond_np = _block_meta(n_p, l_p, n_s, l_s)
    meta = jnp.asarray(meta_np.reshape(-1))

    x = jnp.concatenate([x_prompt.reshape(n_p * l_p, d), x_sample.reshape(n_s * l_s, d)], axis=0)

    ncond = 16
    conds = jnp.zeros((ncond, d), F32).at[0].set(c_ctx).at[1:1 + n_s].set(c)
    mod = _mod_all(conds, w_mod, b_mod)
    mod_b = mod[:, cond_np].reshape(depth, nb, 6, 1, d)

    cos_t, sin_t = _rotary_tables(l_s, d // RET_HEADS)
    lg_ret = jax.nn.log_sigmoid(ret_decay.astype(F32))

    wr_pad = jnp.zeros((depth, d, ROUTER_PAD), BF16).at[:, :, :n_exp].set(moe_w_router.astype(BF16))
    br_pad = jnp.full((depth, 1, ROUTER_PAD), -1e30, F32).at[:, 0, :n_exp].set(moe_b_router)

    new_ret = jnp.zeros((n_p,) + state_ret.shape[1:], F32)
    new_gla, new_s5_re, new_s5_im = [], [], []
    ret_i = gla_i = s5_i = 0
    for layer in range(depth):
        sh1, sc1, g1, sh2, sc2, g2 = [mod_b[layer, :, t] for t in range(6)]
        lng1, lnb1 = ln_g[layer, 0][None], ln_b[layer, 0][None]
        lng2, lnb2 = ln_g[layer, 1][None], ln_b[layer, 1][None]
        wr, br = wr_pad[layer], br_pad[layer]
        kind = layer % 3
        if kind == 0:
            qkvg = _proj(x, sc1, sh1, ret_w_in[ret_i].astype(BF16), tm=1024, tn=1536)
            a, new_ret = _retention(qkvg, meta, lg_ret[ret_i].reshape(-1), cos_t, sin_t, state_ret, new_ret,
                                    ret_i, nb)
            x, h2, idx_t, gate_t, rank_t, cnt = _mixer_out(a, ret_w_out[ret_i].astype(BF16), x, g1, lng1, lnb1, sc2, sh2,
                                              wr, br, alpha)
            ret_i += 1
        elif kind == 1:
            qkvr = _proj(x, sc1, sh1, gla_w_in[gla_i].astype(BF16), tm=1024, tn=1536)
            rank = gla_w_a1.shape[-1]
            key = gla_w_a2.shape[-1]
            w1 = jnp.concatenate([gla_w_a1[gla_i, 0], gla_w_a1[gla_i, 1]], axis=-1).astype(BF16)
            w2 = jnp.zeros((2 * rank, 2 * key), F32)
            w2 = w2.at[:rank, :key].set(gla_w_a2[gla_i, 0]).at[rank:, key:].set(gla_w_a2[gla_i, 1]).astype(BF16)
            la = _gla_gate(x, sc1, sh1, w1, w2, gla_b_a[gla_i].reshape(1, 2 * key))
            a, st = _gla(qkvr, la, meta, state_gla, gla_i, nb, n_p)
            new_gla.append(st)
            x, h2, idx_t, gate_t, rank_t, cnt = _mixer_out(a, gla_w_out[gla_i].astype(BF16), x, g1, lng1, lnb1, sc2, sh2,
                                              wr, br, alpha)
            gla_i += 1
        else:
            t = S5_CHUNK
            g = d // S5_GROUP
            p = s5_a_re.shape[-1]
            r = n // t
            w_in, m_intra, c_f, c_b, a1, a2 = _s5_tables(
                s5_a_re[s5_i], s5_a_im[s5_i], s5_log_step[s5_i], s5_b_re[s5_i], s5_b_im[s5_i],
                s5_c_re[s5_i], s5_c_im[s5_i])
            hmod = _modulate(x, sc1, sh1)
            u = hmod.reshape(r, t, g, S5_GROUP).transpose(2, 0, 1, 3).reshape(g, r, t * S5_GROUP)
            wf, wb = _s5_in(u, w_in, tr=r // 2)
            s0 = jnp.concatenate([state_s5_re[:, s5_i], state_s5_im[:, s5_i]], axis=-1)
            s0 = jnp.concatenate([jnp.zeros((n_p,) + s0.shape[1:], F32), s0], axis=0)
            xf, stf = _s5_scan(wf.reshape(r, g, 2 * p), a1[0], a2[0], s0[:, 0], meta, nb, nseq, reverse=False)
            xb, stb = _s5_scan(wb.reshape(r, g, 2 * p), a1[1], a2[1], s0[:, 1], meta, nb, nseq, reverse=True)
            yg = _s5_out(u, m_intra, xf.reshape(r, g * 2 * p), c_f, xb.reshape(r, g * 2 * p), c_b, tr=r // 2)
            y = yg.reshape(g, r, t, S5_GROUP).transpose(1, 2, 0, 3).reshape(n, d)
            st = jnp.stack([stf[:n_p], stb[:n_p]], axis=1)
            new_s5_re.append(st[..., :p])
            new_s5_im.append(st[..., p:])
            wg = s5_w_glu[s5_i].astype(BF16)
            x, h2, idx_t, gate_t, rank_t, cnt = _glu_out(y, wg[:, :d], wg[:, d:], s5_d[s5_i][None], sc1, sh1, x, g1,
                                            lng1, lnb1, sc2, sh2, wr, br, alpha)
            s5_i += 1
        x = _moe(x, h2, idx_t, gate_t, rank_t, cnt, g2, lng2, lnb2, moe_w_gu[layer], moe_b_gu[layer][:, None, :],
                 moe_w_down[layer], moe_b_down[layer][:, None, :], alpha)

    y_prompt = x[:n_p * l_p].reshape(n_p, l_p, d)
    y_sample = x[n_p * l_p:].reshape(n_s, l_s, d)
    return (y_prompt, y_sample, new_ret, jnp.stack(new_gla, axis=1),
            jnp.stack(new_s5_re, axis=1), jnp.stack(new_s5_im, axis=1))
```

```python
import functools
import math

import numpy as np
import jax
import jax.numpy as jnp
from jax import lax
from jax.experimental import pallas as pl
from jax.experimental.pallas import tpu as pltpu

F32 = jnp.float32
BF16 = jnp.bfloat16

RET_HEADS = 4
GLA_HEADS = 4
GLA_TAU = 16.0
GRID_W = 64
ROPE_BASE = 10000.0
S5_GROUP = 16
TOP_K = 4
SWIGLU_ALPHA = 1.702
SWIGLU_LIMIT = 7.0
LN_EPS = 1e-5
GN_EPS = 1e-5

ROW_BLOCK = 256
GLA_CHUNK = 64
S5_CHUNK = 16
S5_LANE_GROUPS = 8
MOE_BLOCK = 512
ROUTER_PAD = 128
VMEM_LIMIT = 56 * 1024 * 1024

M_START, M_END, M_SAMPLE, M_SIDX, M_PIDX, M_POS, M_SEQ = range(7)
M_ROWS = 7

NT = (((1,), (1,)), ((), ()))
TN = (((0,), (0,)), ((), ()))


def _cparams(sem):
    return pltpu.CompilerParams(dimension_semantics=sem, vmem_limit_bytes=VMEM_LIMIT)


def _silu(x):
    return x * jax.nn.sigmoid(x)


def _dot(a, b):
    return jnp.dot(a, b, preferred_element_type=F32)


def _mod_kernel(c_ref, w_ref, b_ref, o_ref):
    c = c_ref[...]
    s = _silu(c).astype(BF16)
    o_ref[0] = _dot(s, w_ref[0].astype(BF16)) + b_ref[0]


def _mod_all(conds, w_mod, b_mod):
    depth, d, n6 = w_mod.shape
    nc = conds.shape[0]
    tn = 1536
    return pl.pallas_call(
        _mod_kernel,
        out_shape=jax.ShapeDtypeStruct((depth, nc, n6), F32),
        grid=(depth, n6 // tn),
        in_specs=[pl.BlockSpec((nc, d), lambda l, j: (0, 0)),
                  pl.BlockSpec((1, d, tn), lambda l, j: (l, 0, j)),
                  pl.BlockSpec((1, 1, tn), lambda l, j: (l, 0, j))],
        out_specs=pl.BlockSpec((1, nc, tn), lambda l, j: (l, 0, j)),
        compiler_params=_cparams(("parallel", "parallel")),
    )(conds, w_mod, b_mod.reshape(depth, 1, n6))


def _proj_kernel(x_ref, sc_ref, sh_ref, w_ref, o_ref, h_scr):
    @pl.when(pl.program_id(1) == 0)
    def _():
        h_scr[...] = (x_ref[...] * (1.0 + sc_ref[...]) + sh_ref[...]).astype(BF16)

    o_ref[...] = _dot(h_scr[...], w_ref[...]).astype(o_ref.dtype)


def _proj(x, sc, sh, w, tm, tn, out_dtype=BF16):
    n, d = x.shape
    nout = w.shape[1]
    kb = tm // ROW_BLOCK
    return pl.pallas_call(
        _proj_kernel,
        out_shape=jax.ShapeDtypeStruct((n, nout), out_dtype),
        grid=(n // tm, nout // tn),
        in_specs=[pl.BlockSpec((tm, d), lambda i, j: (i, 0)),
                  pl.BlockSpec((None, 1, d), lambda i, j: (i * kb, 0, 0)),
                  pl.BlockSpec((None, 1, d), lambda i, j: (i * kb, 0, 0)),
                  pl.BlockSpec((d, tn), lambda i, j: (0, j))],
        out_specs=pl.BlockSpec((tm, tn), lambda i, j: (i, j)),
        scratch_shapes=[pltpu.VMEM((tm, d), BF16)],
        compiler_params=_cparams(("parallel", "arbitrary")),
        name="proj",
    )(x, sc, sh, w)


def _layer_norm(z, g, b):
    mu = jnp.mean(z, axis=-1, keepdims=True)
    zc = z - mu
    var = jnp.mean(zc * zc, axis=-1, keepdims=True)
    return zc * lax.rsqrt(var + LN_EPS) * g + b


def _route(h2, wr_ref, br_ref, cnt_scr, idx_ref, gate_ref, rank_ref, cnt_ref):
    logits = _dot(h2.astype(BF16), wr_ref[...]) + br_ref[...]
    tm, width = logits.shape
    col = lax.broadcasted_iota(jnp.int32, (tm, width), 1)
    colf = col.astype(F32)
    l = logits
    vals, idxs = [], []
    for _ in range(TOP_K):
        m = jnp.max(l, axis=-1, keepdims=True)
        ix = jnp.min(jnp.where(l == m, colf, float(width)), axis=-1, keepdims=True)
        vals.append(m)
        idxs.append(ix)
        l = jnp.where(colf == ix, -jnp.inf, l)
    es = [jnp.exp(v - vals[0]) for v in vals]
    den = es[0] + es[1] + es[2] + es[3]
    onehot = jnp.where(l == -jnp.inf, 1.0, 0.0)

    @pl.when(pl.program_id(0) == 0)
    def _():
        cnt_scr[...] = jnp.zeros_like(cnt_scr)

    rn = lax.broadcasted_iota(jnp.int32, (tm, tm), 0)
    rm = lax.broadcasted_iota(jnp.int32, (tm, tm), 1)
    earlier = jnp.where(rn > rm, 1.0, 0.0).astype(BF16)
    cum = _dot(earlier, onehot.astype(BF16)) + cnt_scr[...]
    idx_t = jnp.zeros((tm, width), jnp.int32)
    rank_t = jnp.zeros((tm, width), jnp.int32)
    gate_t = jnp.zeros((tm, width), F32)
    for k in range(TOP_K):
        rank_k = jnp.sum(jnp.where(colf == idxs[k], cum, 0.0), axis=-1, keepdims=True)
        idx_t = jnp.where(col == k, idxs[k].astype(jnp.int32), idx_t)
        rank_t = jnp.where(col == k, rank_k.astype(jnp.int32), rank_t)
        gate_t = jnp.where(col == k, es[k] / den, gate_t)
    idx_ref[...] = idx_t
    gate_ref[...] = gate_t
    rank_ref[...] = rank_t
    cnt_new = cnt_scr[...] + jnp.sum(onehot, axis=0, keepdims=True)
    cnt_scr[...] = cnt_new
    cnt_ref[...] = cnt_new


def _post_sublayer(alpha, x, y, g, lng, lnb):
    return _layer_norm(alpha * x + g * y, lng, lnb)


def _mixer_out_kernel(a_ref, w_ref, x_ref, g_ref, lng_ref, lnb_ref, sc2_ref, sh2_ref,
                      wr_ref, br_ref, xo_ref, h2_ref, idx_ref, gate_ref, rank_ref, cnt_ref, cnt_scr, *, alpha):
    y = _dot(a_ref[...], w_ref[...])
    xn = _post_sublayer(alpha, x_ref[...], y, g_ref[...], lng_ref[...], lnb_ref[...])
    xo_ref[...] = xn
    h2 = xn * (1.0 + sc2_ref[...]) + sh2_ref[...]
    h2_ref[...] = h2
    _route(h2, wr_ref, br_ref, cnt_scr, idx_ref, gate_ref, rank_ref, cnt_ref)


def _glu_out_kernel(y_ref, w1_ref, w2_ref, d_ref, sc1_ref, sh1_ref, x_ref, g_ref, lng_ref, lnb_ref,
                    sc2_ref, sh2_ref, wr_ref, br_ref, xo_ref, h2_ref, idx_ref, gate_ref, rank_ref, cnt_ref,
                    cnt_scr, *, alpha):
    x = x_ref[...]
    h = x * (1.0 + sc1_ref[...]) + sh1_ref[...]
    a = jax.nn.gelu(y_ref[...] + d_ref[...] * h).astype(BF16)
    y = _dot(a, w1_ref[...]) * jax.nn.sigmoid(_dot(a, w2_ref[...]))
    xn = _post_sublayer(alpha, x, y, g_ref[...], lng_ref[...], lnb_ref[...])
    xo_ref[...] = xn
    h2 = xn * (1.0 + sc2_ref[...]) + sh2_ref[...]
    h2_ref[...] = h2
    _route(h2, wr_ref, br_ref, cnt_scr, idx_ref, gate_ref, rank_ref, cnt_ref)


def _row_spec(tm, width):
    return pl.BlockSpec((tm, width), lambda i: (i, 0))


def _blk_vec_spec(tm, d):
    kb = tm // ROW_BLOCK
    return pl.BlockSpec((None, 1, d), lambda i: (i * kb, 0, 0))


def _const_spec(shape):
    return pl.BlockSpec(shape, lambda i: tuple(0 for _ in shape))


def _post_out_shapes(n, d):
    return (jax.ShapeDtypeStruct((n, d), F32), jax.ShapeDtypeStruct((n, d), F32),
            jax.ShapeDtypeStruct((n, ROUTER_PAD), jnp.int32), jax.ShapeDtypeStruct((n, ROUTER_PAD), F32),
            jax.ShapeDtypeStruct((n, ROUTER_PAD), jnp.int32), jax.ShapeDtypeStruct((1, ROUTER_PAD), F32))


def _post_out_specs(tm, d):
    return (_row_spec(tm, d), _row_spec(tm, d), _row_spec(tm, ROUTER_PAD), _row_spec(tm, ROUTER_PAD),
            _row_spec(tm, ROUTER_PAD), _const_spec((1, ROUTER_PAD)))


def _mixer_out(a, w, x, g1, lng, lnb, sc2, sh2, wr, br, alpha, tm=512):
    n, d = x.shape
    kin = a.shape[1]
    return pl.pallas_call(
        functools.partial(_mixer_out_kernel, alpha=alpha),
        out_shape=_post_out_shapes(n, d),
        grid=(n // tm,),
        in_specs=[_row_spec(tm, kin), _const_spec((kin, d)), _row_spec(tm, d), _blk_vec_spec(tm, d),
                  _const_spec((1, d)), _const_spec((1, d)), _blk_vec_spec(tm, d), _blk_vec_spec(tm, d),
                  _const_spec((d, ROUTER_PAD)), _const_spec((1, ROUTER_PAD))],
        out_specs=_post_out_specs(tm, d),
        scratch_shapes=[pltpu.VMEM((1, ROUTER_PAD), F32)],
        compiler_params=_cparams(("arbitrary",)),
        name="mixer_out_route",
    )(a, w, x, g1, lng, lnb, sc2, sh2, wr, br)


def _glu_out(y, w1, w2, dskip, sc1, sh1, x, g1, lng, lnb, sc2, sh2, wr, br, alpha, tm=512):
    n, d = x.shape
    return pl.pallas_call(
        functools.partial(_glu_out_kernel, alpha=alpha),
        out_shape=_post_out_shapes(n, d),
        grid=(n // tm,),
        in_specs=[_row_spec(tm, d), _const_spec((d, d)), _const_spec((d, d)), _const_spec((1, d)),
                  _blk_vec_spec(tm, d), _blk_vec_spec(tm, d), _row_spec(tm, d), _blk_vec_spec(tm, d),
                  _const_spec((1, d)), _const_spec((1, d)), _blk_vec_spec(tm, d), _blk_vec_spec(tm, d),
                  _const_spec((d, ROUTER_PAD)), _const_spec((1, ROUTER_PAD))],
        out_specs=_post_out_specs(tm, d),
        scratch_shapes=[pltpu.VMEM((1, ROUTER_PAD), F32)],
        compiler_params=_cparams(("arbitrary",)),
        name="mixer_out_route",
    )(y, w1, w2, dskip, sc1, sh1, x, g1, lng, lnb, sc2, sh2, wr, br)


def _rotary(x, cos, sin):
    half = x.shape[-1] // 2
    x1, x2 = x[:, :half], x[:, half:]
    return jnp.concatenate([x1 * cos - x2 * sin, x1 * sin + x2 * cos], axis=-1)


def _head_norm(o):
    mu = jnp.mean(o, axis=-1, keepdims=True)
    oc = o - mu
    var = jnp.mean(oc * oc, axis=-1, keepdims=True)
    return oc * lax.rsqrt(var + GN_EPS)


def _ret_fwd_kernel(meta_ref, lg_ref, q_ref, k_ref, v_ref, cos_ref, sin_ref, s0_ref,
                    st_in_ref, o_ref, st_ref, s_scr, *, nb, heads):
    del st_in_ref
    i = pl.program_id(0)
    start = meta_ref[M_START * nb + i]
    end = meta_ref[M_END * nb + i]
    sample = meta_ref[M_SAMPLE * nb + i]
    c = q_ref.shape[0]
    dk = q_ref.shape[1] // heads
    dv = v_ref.shape[1] // heads

    @pl.when(jnp.logical_and(start == 1, sample == 1))
    def _():
        s_scr[...] = s0_ref[...]

    @pl.when(jnp.logical_and(start == 1, sample == 0))
    def _():
        s_scr[...] = jnp.zeros_like(s_scr)

    cos = cos_ref[...]
    sin = sin_ref[...]
    rn = lax.broadcasted_iota(jnp.int32, (c, c), 0)
    rm = lax.broadcasted_iota(jnp.int32, (c, c), 1)
    rel = (rn - rm).astype(F32)
    idx = lax.broadcasted_iota(jnp.int32, (c, 1), 0).astype(F32)
    for h in range(heads):
        lgf = lg_ref[h]
        lgb = lg_ref[heads + h]
        q = _rotary(q_ref[:, h * dk:(h + 1) * dk].astype(F32), cos, sin) * (dk ** -0.5)
        k = _rotary(k_ref[:, h * dk:(h + 1) * dk].astype(F32), cos, sin)
        qb = q.astype(BF16)
        v = v_ref[:, h * dv:(h + 1) * dv]
        dmat = (jnp.where(rel >= 0, jnp.exp(lgf * jnp.maximum(rel, 0.0)), 0.0)
                + jnp.where(rel <= 0, jnp.exp(lgb * jnp.maximum(-rel, 0.0)), 0.0))
        scores = lax.dot_general(qb, k.astype(BF16), NT, preferred_element_type=F32) * dmat
        xi = jnp.exp(lgf * (idx + 1.0))
        zeta = jnp.exp(lgf * (c - 1.0 - idx))
        s_prev = s_scr[h]
        o_ref[:, h * dv:(h + 1) * dv] = _dot(scores.astype(BF16), v) + _dot(qb, s_prev.astype(BF16)) * xi
        g_chunk = jnp.exp(lgf * jnp.full((1, 1), float(c), F32))
        s_scr[h] = s_prev * g_chunk + lax.dot_general((k * zeta).astype(BF16), v, TN,
                                                       preferred_element_type=F32)

    @pl.when(jnp.logical_and(end == 1, sample == 0))
    def _():
        st_ref[...] = s_scr[...]


def _ret_bwd_kernel(meta_ref, lg_ref, q_ref, k_ref, v_ref, g_ref, cos_ref, sin_ref, s0_ref, of_ref,
                    st_in_ref, o_ref, st_ref, s_scr, *, nb, heads):
    del st_in_ref
    i = nb - 1 - pl.program_id(0)
    start = meta_ref[M_START * nb + i]
    end = meta_ref[M_END * nb + i]
    sample = meta_ref[M_SAMPLE * nb + i]
    c = q_ref.shape[0]
    dk = q_ref.shape[1] // heads
    dv = v_ref.shape[1] // heads

    @pl.when(jnp.logical_and(end == 1, sample == 1))
    def _():
        s_scr[...] = s0_ref[...]

    @pl.when(jnp.logical_and(end == 1, sample == 0))
    def _():
        s_scr[...] = jnp.zeros_like(s_scr)

    cos = cos_ref[...]
    sin = sin_ref[...]
    idx = lax.broadcasted_iota(jnp.int32, (c, 1), 0).astype(F32)
    for h in range(heads):
        lgb = lg_ref[heads + h]
        q = _rotary(q_ref[:, h * dk:(h + 1) * dk].astype(F32), cos, sin) * (dk ** -0.5)
        k = _rotary(k_ref[:, h * dk:(h + 1) * dk].astype(F32), cos, sin)
        v = v_ref[:, h * dv:(h + 1) * dv]
        xi = jnp.exp(lgb * (c - idx))
        zeta = jnp.exp(lgb * idx)
        s_prev = s_scr[h]
        o = of_ref[:, h * dv:(h + 1) * dv] + _dot(q.astype(BF16), s_prev.astype(BF16)) * xi
        g_chunk = jnp.exp(lgb * jnp.full((1, 1), float(c), F32))
        s_scr[h] = s_prev * g_chunk + lax.dot_general((k * zeta).astype(BF16), v, TN,
                                                       preferred_element_type=F32)
        gate = _silu(g_ref[:, h * dv:(h + 1) * dv].astype(F32))
        o_ref[:, h * dv:(h + 1) * dv] = (gate * _head_norm(o)).astype(o_ref.dtype)

    @pl.when(jnp.logical_and(start == 1, sample == 0))
    def _():
        st_ref[...] = s_scr[...]


def _retention(qkvg, meta, lg, cos_t, sin_t, state, new_state, li, nb):
    n = qkvg.shape[0]
    heads = RET_HEADS
    d = qkvg.shape[1] // 6
    dk, dv = d // heads, 2 * d // heads
    c = ROW_BLOCK
    kw = dict(nb=nb, heads=heads)
    r = lambda j: nb - 1 - j
    pos = lambda j, m: m[M_POS * nb + j]
    st_blk = (None, None, None, heads, dk, dv)
    any_spec = pl.BlockSpec(memory_space=pl.ANY)
    of, new_state = pl.pallas_call(
        functools.partial(_ret_fwd_kernel, **kw),
        out_shape=(jax.ShapeDtypeStruct((n, heads * dv), F32),
                   jax.ShapeDtypeStruct(new_state.shape, new_state.dtype)),
        grid_spec=pltpu.PrefetchScalarGridSpec(
            num_scalar_prefetch=1, grid=(nb,),
            in_specs=[pl.BlockSpec(memory_space=pltpu.SMEM),
                      pl.BlockSpec((c, d), lambda j, m: (j, 0)),
                      pl.BlockSpec((c, d), lambda j, m: (j, 1)),
                      pl.BlockSpec((c, 2 * d), lambda j, m: (j, 1)),
                      pl.BlockSpec((c, dk // 2), lambda j, m: (pos(j, m), 0)),
                      pl.BlockSpec((c, dk // 2), lambda j, m: (pos(j, m), 0)),
                      pl.BlockSpec(st_blk, lambda j, m: (m[M_SIDX * nb + j], li, 0, 0, 0, 0)),
                      any_spec],
            out_specs=(pl.BlockSpec((c, 2 * d), lambda j, m: (j, 0)),
                       pl.BlockSpec(st_blk, lambda j, m: (m[M_PIDX * nb + j], li, 0, 0, 0, 0))),
            scratch_shapes=[pltpu.VMEM((heads, dk, dv), F32)]),
        input_output_aliases={8: 1},
        compiler_params=_cparams(("arbitrary",)),
        name="ret_fwd",
    )(meta, lg, qkvg, qkvg, qkvg, cos_t, sin_t, state, new_state)
    out, new_state = pl.pallas_call(
        functools.partial(_ret_bwd_kernel, **kw),
        out_shape=(jax.ShapeDtypeStruct((n, heads * dv), BF16),
                   jax.ShapeDtypeStruct(new_state.shape, new_state.dtype)),
        grid_spec=pltpu.PrefetchScalarGridSpec(
            num_scalar_prefetch=1, grid=(nb,),
            in_specs=[pl.BlockSpec(memory_space=pltpu.SMEM),
                      pl.BlockSpec((c, d), lambda j, m: (r(j), 0)),
                      pl.BlockSpec((c, d), lambda j, m: (r(j), 1)),
                      pl.BlockSpec((c, 2 * d), lambda j, m: (r(j), 1)),
                      pl.BlockSpec((c, 2 * d), lambda j, m: (r(j), 2)),
                      pl.BlockSpec((c, dk // 2), lambda j, m: (pos(r(j), m), 0)),
                      pl.BlockSpec((c, dk // 2), lambda j, m: (pos(r(j), m), 0)),
                      pl.BlockSpec(st_blk, lambda j, m: (m[M_SIDX * nb + r(j)], li, 1, 0, 0, 0)),
                      pl.BlockSpec((c, 2 * d), lambda j, m: (r(j), 0)),
                      any_spec],
            out_specs=(pl.BlockSpec((c, 2 * d), lambda j, m: (r(j), 0)),
                       pl.BlockSpec(st_blk, lambda j, m: (m[M_PIDX * nb + r(j)], li, 1, 0, 0, 0))),
            scratch_shapes=[pltpu.VMEM((heads, dk, dv), F32)]),
        input_output_aliases={10: 1},
        compiler_params=_cparams(("arbitrary",)),
        name="ret_bwd",
    )(meta, lg, qkvg, qkvg, qkvg, qkvg, cos_t, sin_t, state, of, new_state)
    return out, new_state


def _gla_gate_kernel(x_ref, sc_ref, sh_ref, w1_ref, w2_ref, b_ref, o_ref):
    h = (x_ref[...] * (1.0 + sc_ref[...]) + sh_ref[...]).astype(BF16)
    u = _dot(h, w1_ref[...]).astype(BF16)
    z = _dot(u, w2_ref[...]) + b_ref[...]
    log_sig = jnp.minimum(z, 0.0) - jnp.log(1.0 + jnp.exp(-jnp.abs(z)))
    o_ref[...] = log_sig / GLA_TAU


def _gla_gate(x, sc, sh, w1, w2, b, tm=512):
    n, d = x.shape
    r2 = w1.shape[1]
    nk = w2.shape[1]
    return pl.pallas_call(
        _gla_gate_kernel,
        out_shape=jax.ShapeDtypeStruct((n, nk), F32),
        grid=(n // tm,),
        in_specs=[_row_spec(tm, d), _blk_vec_spec(tm, d), _blk_vec_spec(tm, d), _const_spec((d, r2)),
                  _const_spec((r2, nk)), _const_spec((1, nk))],
        out_specs=_row_spec(tm, nk),
        compiler_params=_cparams(("parallel",)),
    )(x, sc, sh, w1, w2, b)


def _cumsum_rows(tri, x):
    hi = x.astype(BF16)
    r1 = x - hi.astype(F32)
    mid = r1.astype(BF16)
    lo = (r1 - mid.astype(F32)).astype(BF16)
    return _dot(tri, hi) + _dot(tri, mid) + _dot(tri, lo)


def _gla_chunk(q, k, v, la, s_t, tri, mask, mid_row, last_row):
    b = _cumsum_rows(tri, la)
    r = b[mid_row:mid_row + 1, :]
    bl = b[last_row:last_row + 1, :]
    qe = (q * jnp.exp(b - r)).astype(BF16)
    ke = (k * jnp.exp(r - b)).astype(BF16)
    qi = (q * jnp.exp(b)).astype(BF16)
    kd = (k * jnp.exp(bl - b)).astype(BF16)
    sc = lax.dot_general(qe, ke, NT, preferred_element_type=F32)
    sc = jnp.where(mask, sc, 0.0).astype(BF16)
    o = _dot(sc, v) + lax.dot_general(qi, s_t.astype(BF16), NT, preferred_element_type=F32)
    s_new = s_t * jnp.exp(bl) + lax.dot_general(v, kd, TN, preferred_element_type=F32)
    return o, s_new


def _gla_fwd_kernel(meta_ref, q_ref, k_ref, v_ref, la_ref, s0_ref, o_ref, st_ref, s_scr, *, nb, heads):
    i = pl.program_id(0)
    start = meta_ref[M_START * nb + i]
    end = meta_ref[M_END * nb + i]
    sample = meta_ref[M_SAMPLE * nb + i]
    dk = q_ref.shape[1] // heads
    dv = v_ref.shape[1] // heads
    cc = GLA_CHUNK

    @pl.when(jnp.logical_and(start == 1, sample == 1))
    def _():
        for h in range(heads):
            s_scr[h] = s0_ref[h].T

    @pl.when(jnp.logical_and(start == 1, sample == 0))
    def _():
        s_scr[...] = jnp.zeros_like(s_scr)

    rn = lax.broadcasted_iota(jnp.int32, (cc, cc), 0)
    rm = lax.broadcasted_iota(jnp.int32, (cc, cc), 1)
    mask = rn >= rm
    tri = jnp.where(mask, 1.0, 0.0).astype(BF16)
    for h in range(heads):
        ks = slice(h * dk, (h + 1) * dk)
        vs = slice(h * dv, (h + 1) * dv)
        s_t = s_scr[h]
        for ci in range(q_ref.shape[0] // cc):
            sl = slice(ci * cc, (ci + 1) * cc)
            q = q_ref[sl, ks].astype(F32) * (dk ** -0.5)
            o, s_t = _gla_chunk(q, k_ref[sl, ks].astype(F32), v_ref[sl, vs], la_ref[sl, ks], s_t, tri, mask,
                                cc // 2 - 1, cc - 1)
            o_ref[sl, vs] = o
        s_scr[h] = s_t

    @pl.when(jnp.logical_and(end == 1, sample == 0))
    def _():
        for h in range(heads):
            st_ref[h] = s_scr[h].T


def _gla_bwd_kernel(meta_ref, q_ref, k_ref, v_ref, r_ref, la_ref, s0_ref, of_ref, o_ref, st_ref, s_scr,
                    *, nb, heads):
    i = nb - 1 - pl.program_id(0)
    start = meta_ref[M_START * nb + i]
    end = meta_ref[M_END * nb + i]
    sample = meta_ref[M_SAMPLE * nb + i]
    dk = q_ref.shape[1] // heads
    dv = v_ref.shape[1] // heads
    cc = GLA_CHUNK

    @pl.when(jnp.logical_and(end == 1, sample == 1))
    def _():
        for h in range(heads):
            s_scr[h] = s0_ref[h].T

    @pl.when(jnp.logical_and(end == 1, sample == 0))
    def _():
        s_scr[...] = jnp.zeros_like(s_scr)

    rn = lax.broadcasted_iota(jnp.int32, (cc, cc), 0)
    rm = lax.broadcasted_iota(jnp.int32, (cc, cc), 1)
    mask = rn <= rm
    tri = jnp.where(mask, 1.0, 0.0).astype(BF16)
    for h in range(heads):
        ks = slice(h * dk, (h + 1) * dk)
        vs = slice(h * dv, (h + 1) * dv)
        s_t = s_scr[h]
        for ci in reversed(range(q_ref.shape[0] // cc)):
            sl = slice(ci * cc, (ci + 1) * cc)
            q = q_ref[sl, ks].astype(F32) * (dk ** -0.5)
            ob, s_t = _gla_chunk(q, k_ref[sl, ks].astype(F32), v_ref[sl, vs], la_ref[sl, ks], s_t, tri, mask,
                                 cc // 2, 0)
            o = of_ref[sl, vs] + ob
            o_ref[sl, vs] = (_silu(r_ref[sl, vs].astype(F32)) * _head_norm(o)).astype(o_ref.dtype)
        s_scr[h] = s_t

    @pl.when(jnp.logical_and(start == 1, sample == 0))
    def _():
        for h in range(heads):
            st_ref[h] = s_scr[h].T


def _gla(qkvr, la, meta, state, li, nb, n_prompt):
    n = qkvr.shape[0]
    heads = GLA_HEADS
    d = qkvr.shape[1] // 3
    dk, dv = d // 2 // heads, d // heads
    c = ROW_BLOCK
    kw = dict(nb=nb, heads=heads)
    r = lambda j: nb - 1 - j
    st_blk = (None, None, None, heads, dk, dv)
    of, st_f = pl.pallas_call(
        functools.partial(_gla_fwd_kernel, **kw),
        out_shape=(jax.ShapeDtypeStruct((n, d), F32), jax.ShapeDtypeStruct((n_prompt, heads, dk, dv), F32)),
        grid_spec=pltpu.PrefetchScalarGridSpec(
            num_scalar_prefetch=1, grid=(nb,),
            in_specs=[pl.BlockSpec((c, d // 2), lambda j, m: (j, 0)),
                      pl.BlockSpec((c, d // 2), lambda j, m: (j, 1)),
                      pl.BlockSpec((c, d), lambda j, m: (j, 1)),
                      pl.BlockSpec((c, d // 2), lambda j, m: (j, 0)),
                      pl.BlockSpec(st_blk, lambda j, m: (m[M_SIDX * nb + j], li, 0, 0, 0, 0))],
            out_specs=(pl.BlockSpec((c, d), lambda j, m: (j, 0)),
                       pl.BlockSpec((None, heads, dk, dv), lambda j, m: (m[M_PIDX * nb + j], 0, 0, 0))),
            scratch_shapes=[pltpu.VMEM((heads, dv, dk), F32)]),
        compiler_params=_cparams(("arbitrary",)),
        name="gla_fwd",
    )(meta, qkvr, qkvr, qkvr, la, state)
    out, st_b = pl.pallas_call(
        functools.partial(_gla_bwd_kernel, **kw),
        out_shape=(jax.ShapeDtypeStruct((n, d), BF16), jax.ShapeDtypeStruct((n_prompt, heads, dk, dv), F32)),
        grid_spec=pltpu.PrefetchScalarGridSpec(
            num_scalar_prefetch=1, grid=(nb,),
            in_specs=[pl.BlockSpec((c, d // 2), lambda j, m: (r(j), 0)),
                      pl.BlockSpec((c, d // 2), lambda j, m: (r(j), 1)),
                      pl.BlockSpec((c, d), lambda j, m: (r(j), 1)),
                      pl.BlockSpec((c, d), lambda j, m: (r(j), 2)),
                      pl.BlockSpec((c, d // 2), lambda j, m: (r(j), 1)),
                      pl.BlockSpec(st_blk, lambda j, m: (m[M_SIDX * nb + r(j)], li, 1, 0, 0, 0)),
                      pl.BlockSpec((c, d), lambda j, m: (r(j), 0))],
            out_specs=(pl.BlockSpec((c, d), lambda j, m: (r(j), 0)),
                       pl.BlockSpec((None, heads, dk, dv), lambda j, m: (m[M_PIDX * nb + r(j)], 0, 0, 0))),
            scratch_shapes=[pltpu.VMEM((heads, dv, dk), F32)]),
        compiler_params=_cparams(("arbitrary",)),
        name="gla_bwd",
    )(meta, qkvr, qkvr, qkvr, qkvr, la, state, of)
    return out, jnp.stack([st_f, st_b], axis=1)


def _row_tile(rows, cap):
    tile = min(cap, rows) // 8 * 8
    while rows % tile:
        tile -= 8
    return tile


def _s5_chunk_rows(x_ref, sc_ref, sh_ref, lhs):
    lanes = x_ref.shape[2]
    for t in range(x_ref.shape[1]):
        lhs[:, t * lanes:(t + 1) * lanes] = (
            x_ref[:, t, :] * (1.0 + sc_ref[:, 0, :]) + sh_ref[:, 0, :]).astype(BF16)


def _s5_in_kernel(x_ref, sc_ref, sh_ref, w_ref, wf_ref, wb_ref, lhs):
    _s5_chunk_rows(x_ref, sc_ref, sh_ref, lhs)
    z = _dot(lhs[...], w_ref[...])
    half = z.shape[1] // 2
    wf_ref[...] = z[:, :half]
    wb_ref[...] = z[:, half:]


def _s5_in(x3, sc3, sh3, win8, tr):
    r, t, d = x3.shape
    nj, kc, ncol = win8.shape
    lanes = kc // t
    x_spec = pl.BlockSpec((tr, t, lanes), lambda ji, ri: (ri, 0, ji))
    v_spec = pl.BlockSpec((tr, 1, lanes), lambda ji, ri: (ri, 0, ji))
    return pl.pallas_call(
        _s5_in_kernel,
        out_shape=(jax.ShapeDtypeStruct((r, nj * ncol // 2), F32), jax.ShapeDtypeStruct((r, nj * ncol // 2), F32)),
        grid=(nj, r // tr),
        in_specs=[x_spec, v_spec, v_spec, pl.BlockSpec((None, kc, ncol), lambda ji, ri: (ji, 0, 0))],
        out_specs=(pl.BlockSpec((tr, ncol // 2), lambda ji, ri: (ri, ji)),
                   pl.BlockSpec((tr, ncol // 2), lambda ji, ri: (ri, ji))),
        scratch_shapes=[pltpu.VMEM((tr, kc), BF16)],
        compiler_params=_cparams(("parallel", "parallel")),
        name="s5_in",
    )(x3, sc3, sh3, win8)


def _s5_scan_kernel(meta_ref, w_ref, a1_ref, a2_ref, s0_ref, xp_ref, st_ref, x_scr, *, nb, reverse):
    j = pl.program_id(0)
    i = nb - 1 - j if reverse else j
    first = meta_ref[(M_END if reverse else M_START) * nb + i]
    last = meta_ref[(M_START if reverse else M_END) * nb + i]

    @pl.when(first == 1)
    def _():
        x_scr[...] = s0_ref[...]

    a1 = a1_ref[...]
    a2 = a2_ref[...]
    x = x_scr[...]
    half = x.shape[1] // 2
    nrow = w_ref.shape[0]
    order = range(nrow - 1, -1, -1) if reverse else range(nrow)
    for c in order:
        xp_ref[c] = x
        x = x * a1 + pltpu.roll(x, half, axis=1) * a2 + w_ref[c]
    x_scr[...] = x

    @pl.when(last == 1)
    def _():
        st_ref[...] = x


def _s5_scan(w, a1, a2, s0, meta, nb, nseq, reverse):
    r, g, p2 = w.shape
    cr = r // nb
    blk = (lambda j: nb - 1 - j) if reverse else (lambda j: j)
    return pl.pallas_call(
        functools.partial(_s5_scan_kernel, nb=nb, reverse=reverse),
        out_shape=(jax.ShapeDtypeStruct((r, g, p2), F32), jax.ShapeDtypeStruct((nseq, g, p2), F32)),
        grid_spec=pltpu.PrefetchScalarGridSpec(
            num_scalar_prefetch=1, grid=(nb,),
            in_specs=[pl.BlockSpec((cr, g, p2), lambda j, m: (blk(j), 0, 0)),
                      pl.BlockSpec((g, p2), lambda j, m: (0, 0)),
                      pl.BlockSpec((g, p2), lambda j, m: (0, 0)),
                      pl.BlockSpec((None, g, p2), lambda j, m: (m[M_SEQ * nb + blk(j)], 0, 0))],
            out_specs=(pl.BlockSpec((cr, g, p2), lambda j, m: (blk(j), 0, 0)),
                       pl.BlockSpec((None, g, p2), lambda j, m: (m[M_SEQ * nb + blk(j)], 0, 0))),
            scratch_shapes=[pltpu.VMEM((g, p2), F32)]),
        compiler_params=_cparams(("arbitrary",)),
        name="s5_scan",
    )(meta, w, a1, a2, s0)


def _s5_out_kernel(x_ref, sc_ref, sh_ref, m_ref, xf_ref, xb_ref, c_ref, y_ref, lhs):
    _s5_chunk_rows(x_ref, sc_ref, sh_ref, lhs)
    states = jnp.concatenate([xf_ref[...], xb_ref[...]], axis=1).astype(BF16)
    z = _dot(lhs[...], m_ref[...]) + _dot(states, c_ref[...])
    lanes = y_ref.shape[2]
    for t in range(y_ref.shape[1]):
        y_ref[:, t, :] = z[:, t * lanes:(t + 1) * lanes]


def _s5_out(x3, sc3, sh3, m8, xf, xb, c8, tr):
    r, t, d = x3.shape
    nj, kc, _ = m8.shape
    lanes = kc // t
    ns = c8.shape[1] // 2
    x_spec = pl.BlockSpec((tr, t, lanes), lambda ji, ri: (ri, 0, ji))
    v_spec = pl.BlockSpec((tr, 1, lanes), lambda ji, ri: (ri, 0, ji))
    s_spec = pl.BlockSpec((tr, ns), lambda ji, ri: (ri, ji))
    return pl.pallas_call(
        _s5_out_kernel,
        out_shape=jax.ShapeDtypeStruct((r, t, d), F32),
        grid=(nj, r // tr),
        in_specs=[x_spec, v_spec, v_spec, pl.BlockSpec((None, kc, kc), lambda ji, ri: (ji, 0, 0)),
                  s_spec, s_spec, pl.BlockSpec((None, 2 * ns, kc), lambda ji, ri: (ji, 0, 0))],
        out_specs=x_spec,
        scratch_shapes=[pltpu.VMEM((tr, kc), BF16)],
        compiler_params=_cparams(("parallel", "parallel")),
        name="s5_out",
    )(x3, sc3, sh3, m8, xf, xb, c8)


def _s5_expand_kernel(a_ref, rep_ref, o_ref, *, row_div, col_div):
    z = _dot(a_ref[...], rep_ref[...])
    rows, cols = z.shape
    row0 = pl.program_id(1) * rows
    gi = ((lax.broadcasted_iota(jnp.int32, (rows, cols), 0) + row0) // row_div) % S5_LANE_GROUPS
    hi = (lax.broadcasted_iota(jnp.int32, (rows, cols), 1) // col_div) % S5_LANE_GROUPS
    o_ref[...] = jnp.where(gi == hi, z, 0.0).astype(o_ref.dtype)


def _s5_expand(a, rep, row_div, col_div, tr=512):
    nj, rows, kin = a.shape
    ncol = rep.shape[1]
    return pl.pallas_call(
        functools.partial(_s5_expand_kernel, row_div=row_div, col_div=col_div),
        out_shape=jax.ShapeDtypeStruct((nj, rows, ncol), BF16),
        grid=(nj, rows // tr),
        in_specs=[pl.BlockSpec((None, tr, kin), lambda j, i: (j, i, 0)),
                  pl.BlockSpec((kin, ncol), lambda j, i: (0, 0))],
        out_specs=pl.BlockSpec((None, tr, ncol), lambda j, i: (j, i, 0)),
        compiler_params=_cparams(("parallel", "parallel")),
        name="s5_expand",
    )(a, rep)


def _s5_tables(a_re, a_im, log_step, b_re, b_im, c_re, c_im):
    t = S5_CHUNK
    step = jnp.exp(log_step.astype(F32))[..., None]
    ar, ai = a_re.astype(F32), a_im.astype(F32)
    mag = jnp.exp(ar * step)
    ab_re, ab_im = mag * jnp.cos(ai * step), mag * jnp.sin(ai * step)
    den = ar * ar + ai * ai
    f_re = ((ab_re - 1.0) * ar + ab_im * ai) / den
    f_im = (ab_im * ar - (ab_re - 1.0) * ai) / den
    br, bi = b_re.astype(F32), b_im.astype(F32)
    bb_re = f_re[..., None] * br - f_im[..., None] * bi
    bb_im = f_re[..., None] * bi + f_im[..., None] * br
    cr, ci = c_re.astype(F32), c_im.astype(F32)
    tau = jnp.arange(t + 1, dtype=F32)[:, None, None, None]
    pmag = jnp.exp(ar[None] * step[None] * tau)
    pw_re = pmag * jnp.cos(ai[None] * step[None] * tau)
    pw_im = pmag * jnp.sin(ai[None] * step[None] * tau)
    g, p, ch = br.shape[1], br.shape[2], br.shape[3]

    def cmul(xr, xi, yr, yi):
        return xr * yr - xi * yi, xr * yi + xi * yr

    def inject(d, powers):
        pr = pw_re[powers, d][:, :, :, None]
        pi = pw_im[powers, d][:, :, :, None]
        wr, wi = cmul(pr, pi, bb_re[d][None], bb_im[d][None])
        w = jnp.concatenate([wr, wi], axis=2)
        return w.transpose(1, 0, 3, 2).reshape(g, t * ch, 2 * p)
    w_in = jnp.concatenate([inject(0, jnp.arange(t - 1, -1, -1)), inject(1, jnp.arange(t))], axis=-1)

    def readout(d, powers):
        pr = pw_re[powers, d][:, :, None, :]
        pi = pw_im[powers, d][:, :, None, :]
        kr, ki = cmul(cr[d][None], ci[d][None], pr, pi)
        m = jnp.concatenate([kr, -ki], axis=3)
        return m.transpose(1, 3, 0, 2).reshape(g, 2 * p, t * ch)
    c_f = readout(0, jnp.arange(1, t + 1))
    c_b = readout(1, jnp.arange(t, 0, -1))

    def lag_kernels(d):
        pr = pw_re[:t, d][:, :, :, None]
        pi = pw_im[:t, d][:, :, :, None]
        wr, wi = cmul(pr, pi, bb_re[d][None], bb_im[d][None])
        return (jnp.einsum('gop,tgpc->tgoc', cr[d], wr) - jnp.einsum('gop,tgpc->tgoc', ci[d], wi))
    kf, kb = lag_kernels(0), lag_kernels(1)
    ii = np.arange(t)
    lag = ii[None, :] - ii[:, None]
    sel_f = (lag[:, :, None] == ii[None, None, :]).astype(np.float32)
    sel_b = (-lag[:, :, None] == ii[None, None, :]).astype(np.float32)
    m = (jnp.einsum('jit,tgoc->gjcio', sel_f, kf, precision=lax.Precision.HIGHEST)
         + jnp.einsum('jit,tgoc->gjcio', sel_b, kb, precision=lax.Precision.HIGHEST))

    at_re, at_im = pw_re[t], pw_im[t]
    a1 = jnp.concatenate([at_re, at_re], axis=-1)
    a2 = jnp.concatenate([-at_im, at_im], axis=-1)

    lg = S5_LANE_GROUPS
    nj = g // lg
    kc = t * lg * ch

    def rows_tgc(a):
        return a.reshape(nj, lg, t, ch, a.shape[-1]).transpose(0, 2, 1, 3, 4).reshape(nj, kc, a.shape[-1]).astype(BF16)

    eye_x = lambda nx, ny: np.einsum('xa,yb->xyab', np.eye(nx), np.eye(ny))[:, :, :, None, :] * np.ones((1, 1, 1, lg, 1))
    rep_m = jnp.asarray(eye_x(t, ch).reshape(t * ch, t * lg * ch), BF16)
    rep_w = jnp.asarray(eye_x(2, 2 * p).reshape(4 * p, 2 * lg * 2 * p), BF16)
    m8 = _s5_expand(rows_tgc(m.reshape(g, t * ch, t * ch)), rep_m, ch, ch)
    win8 = _s5_expand(rows_tgc(w_in), rep_w, ch, 2 * p)
    c_c = jnp.concatenate([c_f.reshape(nj, lg * 2 * p, t * ch), c_b.reshape(nj, lg * 2 * p, t * ch)], axis=1)
    c8 = _s5_expand(c_c.astype(BF16), rep_m, 2 * p, ch)
    return win8, m8, c8, a1, a2


def _dispatch_kernel(pend_ref, dest_ref, h_ref, xs_ref, zero_scr, sem, zsem, *, n_exp):
    tm = h_ref.shape[0]
    bm = zero_scr.shape[0]

    @pl.when(pl.program_id(0) == 0)
    def _():
        zero_scr[...] = jnp.zeros_like(zero_scr)
        for e in range(n_exp):
            prev = pend_ref[e - 1] if e else 0

            @pl.when(pend_ref[e] > prev)
            def _():
                off = pl.multiple_of(pend_ref[e] - bm, bm)
                pltpu.make_async_copy(zero_scr, xs_ref.at[pl.ds(off, bm)], zsem).start()
        n_used = pend_ref[n_exp - 1] // bm
        n_blocks = xs_ref.shape[0] // bm

        def zero_tail(b, carry):
            off = pl.multiple_of(b * bm, bm)
            pltpu.make_async_copy(zero_scr, xs_ref.at[pl.ds(off, bm)], zsem).start()
            return carry

        def wait_tail(b, carry):
            pltpu.make_async_copy(zero_scr, xs_ref.at[pl.ds(0, bm)], zsem).wait()
            return carry

        lax.fori_loop(n_used, n_blocks, zero_tail, 0)
        for e in range(n_exp):
            prev = pend_ref[e - 1] if e else 0

            @pl.when(pend_ref[e] > prev)
            def _():
                pltpu.make_async_copy(zero_scr, xs_ref.at[pl.ds(0, bm)], zsem).wait()
        lax.fori_loop(n_used, n_blocks, wait_tail, 0)

    def issue(t, carry):
        for k in range(TOP_K):
            pltpu.make_async_copy(h_ref.at[pl.ds(t, 1)], xs_ref.at[pl.ds(dest_ref[t * TOP_K + k], 1)], sem).start()
        return carry

    lax.fori_loop(0, tm, issue, 0)
    for _ in range(TOP_K):
        pltpu.make_async_copy(h_ref, xs_ref.at[pl.ds(0, tm)], sem).wait()


def _dispatch(pend, dest, h2, n_rows, tm=512):
    n, d = h2.shape
    return pl.pallas_call(
        functools.partial(_dispatch_kernel, n_exp=pend.shape[0]),
        out_shape=jax.ShapeDtypeStruct((n_rows, d), F32),
        grid_spec=pltpu.PrefetchScalarGridSpec(
            num_scalar_prefetch=1, grid=(n // tm,),
            in_specs=[pl.BlockSpec((tm * TOP_K,), lambda i, p: (i,), memory_space=pltpu.SMEM),
                      pl.BlockSpec((tm, d), lambda i, p: (i, 0))],
            out_specs=pl.BlockSpec(memory_space=pl.ANY),
            scratch_shapes=[pltpu.VMEM((MOE_BLOCK, d), F32), pltpu.SemaphoreType.DMA,
                            pltpu.SemaphoreType.DMA]),
        compiler_params=_cparams(("arbitrary",)),
        name="moe_dispatch",
    )(pend, dest, h2)


def _ffn_kernel(sp_ref, x_ref, wgu_ref, bgu_ref, wd_ref, bd_ref, o_ref, wgu_s, wd_s, *, n_blocks):
    b = pl.program_id(0)
    used = b < sp_ref[2 * n_blocks]
    dff = wd_ref.shape[0]

    @pl.when(jnp.logical_and(used, sp_ref[n_blocks + b] == 1))
    def _():
        wgu_s[...] = wgu_ref[...].astype(BF16)
        wd_s[...] = wd_ref[...].astype(BF16)

    @pl.when(used)
    def _():
        gu = _dot(x_ref[...].astype(BF16), wgu_s[...]) + bgu_ref[...]
        gate = jnp.minimum(gu[:, :dff], SWIGLU_LIMIT)
        lin = jnp.clip(gu[:, dff:], -SWIGLU_LIMIT, SWIGLU_LIMIT)
        act = gate * jax.nn.sigmoid(SWIGLU_ALPHA * gate) * (lin + 1.0)
        o_ref[...] = _dot(act.astype(BF16), wd_s[...]) + bd_ref[...]

    @pl.when(jnp.logical_not(used))
    def _():
        o_ref[...] = jnp.zeros_like(o_ref)


def _ffn(sp, xs, wgu, bgu, wd, bd, layer):
    n_rows, d = xs.shape
    bm = MOE_BLOCK
    n_blocks = n_rows // bm
    dff = wd.shape[2]
    return pl.pallas_call(
        functools.partial(_ffn_kernel, n_blocks=n_blocks),
        out_shape=jax.ShapeDtypeStruct((n_rows, d), F32),
        grid_spec=pltpu.PrefetchScalarGridSpec(
            num_scalar_prefetch=1, grid=(n_blocks,),
            in_specs=[pl.BlockSpec((bm, d), lambda b, sp: (jnp.minimum(b, sp[2 * n_blocks] - 1), 0)),
                      pl.BlockSpec((None, None, d, 2 * dff), lambda b, sp: (layer, sp[b], 0, 0)),
                      pl.BlockSpec((None, None, 1, 2 * dff), lambda b, sp: (layer, sp[b], 0, 0)),
                      pl.BlockSpec((None, None, dff, d), lambda b, sp: (layer, sp[b], 0, 0)),
                      pl.BlockSpec((None, None, 1, d), lambda b, sp: (layer, sp[b], 0, 0))],
            out_specs=pl.BlockSpec((bm, d), lambda b, sp: (b, 0)),
            scratch_shapes=[pltpu.VMEM((d, 2 * dff), BF16), pltpu.VMEM((dff, d), BF16)]),
        compiler_params=_cparams(("arbitrary",)),
        name="moe_ffn",
    )(sp, xs, wgu, bgu, wd, bd)


def _combine_kernel(dest_ref, yb_ref, gate_ref, x_ref, g_ref, lng_ref, lnb_ref, xo_ref, buf, sem, *, alpha):
    tm = x_ref.shape[0]

    def issue(t, carry):
        for k in range(TOP_K):
            pltpu.make_async_copy(yb_ref.at[pl.ds(dest_ref[t * TOP_K + k], 1)], buf.at[k, pl.ds(t, 1)], sem).start()
        return carry

    lax.fori_loop(0, tm, issue, 0)
    for k in range(TOP_K):
        pltpu.make_async_copy(yb_ref.at[pl.ds(0, tm)], buf.at[k], sem).wait()
    gates = gate_ref[...]
    y = gates[:, 0:1] * buf[0]
    for k in range(1, TOP_K):
        y = y + gates[:, k:k + 1] * buf[k]
    xo_ref[...] = _post_sublayer(alpha, x_ref[...], y, g_ref[...], lng_ref[...], lnb_ref[...])


def _combine(dest, yb, gate_t, x, g2, lng, lnb, alpha, tm=512):
    n, d = x.shape
    return pl.pallas_call(
        functools.partial(_combine_kernel, alpha=alpha),
        out_shape=jax.ShapeDtypeStruct((n, d), F32),
        grid=(n // tm,),
        in_specs=[pl.BlockSpec((tm * TOP_K,), lambda i: (i,), memory_space=pltpu.SMEM),
                  pl.BlockSpec(memory_space=pl.ANY),
                  _row_spec(tm, ROUTER_PAD), _row_spec(tm, d), _blk_vec_spec(tm, d), _const_spec((1, d)),
                  _const_spec((1, d))],
        out_specs=_row_spec(tm, d),
        scratch_shapes=[pltpu.VMEM((TOP_K, tm, d), F32), pltpu.SemaphoreType.DMA],
        compiler_params=_cparams(("arbitrary",)),
        name="moe_combine",
    )(dest, yb, gate_t, x, g2, lng, lnb)


def _moe(x, h2, idx_t, gate_t, rank_t, cnt, g2, lng, lnb, wgu, bgu, wd, bd, layer, alpha):
    n_tok, d = h2.shape
    n_exp = wgu.shape[1]
    bm = MOE_BLOCK
    n_blocks = -(-n_tok * TOP_K // bm) + n_exp
    counts = cnt[0, :n_exp].astype(jnp.int32)
    padded = (counts + bm - 1) // bm * bm
    pend = jnp.cumsum(padded)
    pstart = pend - padded
    e_iota = jnp.arange(n_exp, dtype=jnp.int32)
    dest = rank_t[:, :TOP_K] + jnp.sum(jnp.where(idx_t[:, :TOP_K, None] == e_iota, pstart, 0), axis=-1)
    dest = dest.reshape(-1).astype(jnp.int32)
    blk_row = jnp.arange(n_blocks, dtype=jnp.int32) * bm
    block_e = jnp.minimum(jnp.sum(pend[None, :] <= blk_row[:, None], axis=1), n_exp - 1).astype(jnp.int32)
    first = jnp.concatenate([jnp.ones((1,), jnp.int32), (block_e[1:] != block_e[:-1]).astype(jnp.int32)])
    sp = jnp.concatenate([block_e, first, (pend[-1:] // bm).astype(jnp.int32)])
    xs = _dispatch(pend.astype(jnp.int32), dest, h2, n_blocks * bm)
    yb = _ffn(sp, xs, wgu, bgu, wd, bd, layer)
    return _combine(dest, yb, gate_t, x, g2, lng, lnb, alpha)


def _block_meta(n_p, l_p, n_s, l_s):
    rb = ROW_BLOCK
    assert l_p % rb == 0 and l_s % rb == 0
    rows = []
    for s in range(n_p):
        for j in range(l_p // rb):
            rows.append((int(j == 0), int(j == l_p // rb - 1), 0, 0, s, 0, s))
    for s in range(n_s):
        for j in range(l_s // rb):
            rows.append((int(j == 0), int(j == l_s // rb - 1), 1, s, n_p - 1, 1 + j, n_p + s))
    meta = np.asarray(rows, np.int32).T
    cond = np.asarray([0] * (n_p * l_p // rb) + [1 + s for s in range(n_s) for _ in range(l_s // rb)], np.int32)
    return meta, cond


def _rotary_tables(l_s, dk):
    rows = l_s // GRID_W
    row = jnp.repeat(jnp.arange(rows, dtype=F32), GRID_W)
    col = jnp.tile(jnp.arange(GRID_W, dtype=F32), rows)
    n_freq = dk // 4
    inv = ROPE_BASE ** (-jnp.arange(n_freq, dtype=F32) / n_freq)
    ang = jnp.concatenate([row[:, None] * inv, col[:, None] * inv], axis=-1)
    ident = jnp.zeros((ROW_BLOCK, dk // 2), F32)
    return (jnp.concatenate([ident + 1.0, jnp.cos(ang)], axis=0),
            jnp.concatenate([ident, jnp.sin(ang)], axis=0))


def kernel(x_prompt, x_sample, state_ret, state_gla, state_s5_re, state_s5_im, c, c_ctx, w_mod, b_mod, ln_g, ln_b, ret_w_in, ret_decay, ret_w_out, gla_w_in, gla_w_a1, gla_w_a2, gla_b_a, gla_w_out, s5_a_re, s5_a_im, s5_log_step, s5_b_re, s5_b_im, s5_c_re, s5_c_im, s5_d, s5_w_glu, moe_w_router, moe_b_router, moe_w_gu, moe_b_gu, moe_w_down, moe_b_down):
    n_p, l_p, d = x_prompt.shape
    n_s, l_s, _ = x_sample.shape
    depth = w_mod.shape[0]
    n_exp = moe_w_router.shape[-1]
    alpha = (2 * depth) ** 0.25
    rb = ROW_BLOCK
    n = n_p * l_p + n_s * l_s
    nb = n // rb
    nseq = n_p + n_s
    meta_np, cond_np = _block_meta(n_p, l_p, n_s, l_s)
    meta = jnp.asarray(meta_np.reshape(-1))

    x = jnp.concatenate([x_prompt.reshape(n_p * l_p, d), x_sample.reshape(n_s * l_s, d)], axis=0)

    ncond = 16
    conds = jnp.zeros((ncond, d), F32).at[0].set(c_ctx).at[1:1 + n_s].set(c)
    mod = _mod_all(conds, w_mod, b_mod)
    mod_b = mod[:, cond_np].reshape(depth, nb, 6, 1, d)

    cos_t, sin_t = _rotary_tables(l_s, d // RET_HEADS)
    lg_ret = jax.nn.log_sigmoid(ret_decay.astype(F32))

    wr_pad = jnp.zeros((depth, d, ROUTER_PAD), BF16).at[:, :, :n_exp].set(moe_w_router.astype(BF16))
    br_pad = jnp.full((depth, 1, ROUTER_PAD), -1e30, F32).at[:, 0, :n_exp].set(moe_b_router)

    new_ret = jnp.zeros((n_p,) + state_ret.shape[1:], F32)
    new_gla, new_s5_re, new_s5_im = [], [], []
    ret_i = gla_i = s5_i = 0
    for layer in range(depth):
        sh1, sc1, g1, sh2, sc2, g2 = [mod_b[layer, :, t] for t in range(6)]
        lng1, lnb1 = ln_g[layer, 0][None], ln_b[layer, 0][None]
        lng2, lnb2 = ln_g[layer, 1][None], ln_b[layer, 1][None]
        wr, br = wr_pad[layer], br_pad[layer]
        kind = layer % 3
        if kind == 0:
            qkvg = _proj(x, sc1, sh1, ret_w_in[ret_i].astype(BF16), tm=1024, tn=1536)
            a, new_ret = _retention(qkvg, meta, lg_ret[ret_i].reshape(-1), cos_t, sin_t, state_ret, new_ret,
                                    ret_i, nb)
            x, h2, idx_t, gate_t, rank_t, cnt = _mixer_out(a, ret_w_out[ret_i].astype(BF16), x, g1, lng1, lnb1, sc2, sh2,
                                              wr, br, alpha)
            ret_i += 1
        elif kind == 1:
            qkvr = _proj(x, sc1, sh1, gla_w_in[gla_i].astype(BF16), tm=1024, tn=1536)
            rank = gla_w_a1.shape[-1]
            key = gla_w_a2.shape[-1]
            w1 = jnp.concatenate([gla_w_a1[gla_i, 0], gla_w_a1[gla_i, 1]], axis=-1).astype(BF16)
            w2 = jnp.zeros((2 * rank, 2 * key), F32)
            w2 = w2.at[:rank, :key].set(gla_w_a2[gla_i, 0]).at[rank:, key:].set(gla_w_a2[gla_i, 1]).astype(BF16)
            la = _gla_gate(x, sc1, sh1, w1, w2, gla_b_a[gla_i].reshape(1, 2 * key))
            a, st = _gla(qkvr, la, meta, state_gla, gla_i, nb, n_p)
            new_gla.append(st)
            x, h2, idx_t, gate_t, rank_t, cnt = _mixer_out(a, gla_w_out[gla_i].astype(BF16), x, g1, lng1, lnb1, sc2, sh2,
                                              wr, br, alpha)
            gla_i += 1
        else:
            t = S5_CHUNK
            g = d // S5_GROUP
            p = s5_a_re.shape[-1]
            r = n // t
            win8, m8, c8, a1, a2 = _s5_tables(
                s5_a_re[s5_i], s5_a_im[s5_i], s5_log_step[s5_i], s5_b_re[s5_i], s5_b_im[s5_i],
                s5_c_re[s5_i], s5_c_im[s5_i])
            x3 = x.reshape(r, t, d)
            sc3 = jnp.repeat(sc1, rb // t, axis=0)
            sh3 = jnp.repeat(sh1, rb // t, axis=0)
            wf, wb = _s5_in(x3, sc3, sh3, win8, tr=_row_tile(r, 640))
            s0 = jnp.concatenate([state_s5_re[:, s5_i], state_s5_im[:, s5_i]], axis=-1)
            s0 = jnp.concatenate([jnp.zeros((n_p,) + s0.shape[1:], F32), s0], axis=0)
            xf, stf = _s5_scan(wf.reshape(r, g, 2 * p), a1[0], a2[0], s0[:, 0], meta, nb, nseq, reverse=False)
            xb, stb = _s5_scan(wb.reshape(r, g, 2 * p), a1[1], a2[1], s0[:, 1], meta, nb, nseq, reverse=True)
            y3 = _s5_out(x3, sc3, sh3, m8, xf.reshape(r, g * 2 * p), xb.reshape(r, g * 2 * p), c8,
                         tr=_row_tile(r, 256))
            y = y3.reshape(n, d)
            st = jnp.stack([stf[:n_p], stb[:n_p]], axis=1)
            new_s5_re.append(st[..., :p])
            new_s5_im.append(st[..., p:])
            wg = s5_w_glu[s5_i].astype(BF16)
            x, h2, idx_t, gate_t, rank_t, cnt = _glu_out(y, wg[:, :d], wg[:, d:], s5_d[s5_i][None], sc1, sh1, x, g1,
                                            lng1, lnb1, sc2, sh2, wr, br, alpha)
            s5_i += 1
        x = _moe(x, h2, idx_t, gate_t, rank_t, cnt, g2, lng2, lnb2, moe_w_gu, moe_b_gu[:, :, None, :],
                 moe_w_down, moe_b_down[:, :, None, :], layer, alpha)

    y_prompt = x[:n_p * l_p].reshape(n_p, l_p, d)
    y_sample = x[n_p * l_p:].reshape(n_s, l_s, d)
    return (y_prompt, y_sample, new_ret, jnp.stack(new_gla, axis=1),
            jnp.stack(new_s5_re, axis=1), jnp.stack(new_s5_im, axis=1))
```

```python
import functools
import math

import numpy as np
import jax
import jax.numpy as jnp
from jax import lax
from jax.experimental import pallas as pl
from jax.experimental.pallas import tpu as pltpu

F32 = jnp.float32
BF16 = jnp.bfloat16

RET_HEADS = 4
GLA_HEADS = 4
GLA_TAU = 16.0
GRID_W = 64
ROPE_BASE = 10000.0
S5_GROUP = 16
TOP_K = 4
SWIGLU_ALPHA = 1.702
SWIGLU_LIMIT = 7.0
LN_EPS = 1e-5
GN_EPS = 1e-5

ROW_BLOCK = 256
GLA_CHUNK = 64
S5_CHUNK = 16
S5_LANE_GROUPS = 8
MOE_BLOCK = 512
ROUTER_PAD = 128
VMEM_LIMIT = 56 * 1024 * 1024

M_START, M_END, M_SAMPLE, M_SIDX, M_PIDX, M_POS, M_SEQ = range(7)
M_ROWS = 7

NT = (((1,), (1,)), ((), ()))
TN = (((0,), (0,)), ((), ()))


def _cparams(sem):
    return pltpu.CompilerParams(dimension_semantics=sem, vmem_limit_bytes=VMEM_LIMIT)


def _silu(x):
    return x * jax.nn.sigmoid(x)


def _dot(a, b):
    return jnp.dot(a, b, preferred_element_type=F32)


def _mod_kernel(c_ref, w_ref, b_ref, o_ref):
    c = c_ref[...]
    s = _silu(c).astype(BF16)
    o_ref[0] = _dot(s, w_ref[0].astype(BF16)) + b_ref[0]


def _mod_all(conds, w_mod, b_mod):
    depth, d, n6 = w_mod.shape
    nc = conds.shape[0]
    tn = 1536
    return pl.pallas_call(
        _mod_kernel,
        out_shape=jax.ShapeDtypeStruct((depth, nc, n6), F32),
        grid=(depth, n6 // tn),
        in_specs=[pl.BlockSpec((nc, d), lambda l, j: (0, 0)),
                  pl.BlockSpec((1, d, tn), lambda l, j: (l, 0, j)),
                  pl.BlockSpec((1, 1, tn), lambda l, j: (l, 0, j))],
        out_specs=pl.BlockSpec((1, nc, tn), lambda l, j: (l, 0, j)),
        compiler_params=_cparams(("parallel", "parallel")),
    )(conds, w_mod, b_mod.reshape(depth, 1, n6))


def _proj_kernel(x_ref, sc_ref, sh_ref, w_ref, o_ref, h_scr):
    @pl.when(pl.program_id(1) == 0)
    def _():
        h_scr[...] = (x_ref[...] * (1.0 + sc_ref[...]) + sh_ref[...]).astype(BF16)

    o_ref[...] = _dot(h_scr[...], w_ref[...]).astype(o_ref.dtype)


def _proj(x, sc, sh, w, tm, tn, out_dtype=BF16):
    n, d = x.shape
    nout = w.shape[1]
    kb = tm // ROW_BLOCK
    return pl.pallas_call(
        _proj_kernel,
        out_shape=jax.ShapeDtypeStruct((n, nout), out_dtype),
        grid=(n // tm, nout // tn),
        in_specs=[pl.BlockSpec((tm, d), lambda i, j: (i, 0)),
                  pl.BlockSpec((None, 1, d), lambda i, j: (i * kb, 0, 0)),
                  pl.BlockSpec((None, 1, d), lambda i, j: (i * kb, 0, 0)),
                  pl.BlockSpec((d, tn), lambda i, j: (0, j))],
        out_specs=pl.BlockSpec((tm, tn), lambda i, j: (i, j)),
        scratch_shapes=[pltpu.VMEM((tm, d), BF16)],
        compiler_params=_cparams(("parallel", "arbitrary")),
        name="proj",
    )(x, sc, sh, w)


def _layer_norm(z, g, b):
    mu = jnp.mean(z, axis=-1, keepdims=True)
    zc = z - mu
    var = jnp.mean(zc * zc, axis=-1, keepdims=True)
    return zc * lax.rsqrt(var + LN_EPS) * g + b


def _route(h2, wr_ref, br_ref, cnt_scr, idx_ref, gate_ref, rank_ref, cnt_ref):
    logits = _dot(h2.astype(BF16), wr_ref[...]) + br_ref[...]
    tm, width = logits.shape
    col = lax.broadcasted_iota(jnp.int32, (tm, width), 1)
    colf = col.astype(F32)
    l = logits
    vals, idxs = [], []
    for _ in range(TOP_K):
        m = jnp.max(l, axis=-1, keepdims=True)
        ix = jnp.min(jnp.where(l == m, colf, float(width)), axis=-1, keepdims=True)
        vals.append(m)
        idxs.append(ix)
        l = jnp.where(colf == ix, -jnp.inf, l)
    es = [jnp.exp(v - vals[0]) for v in vals]
    den = es[0] + es[1] + es[2] + es[3]
    onehot = jnp.where(l == -jnp.inf, 1.0, 0.0)

    @pl.when(pl.program_id(0) == 0)
    def _():
        cnt_scr[...] = jnp.zeros_like(cnt_scr)

    rn = lax.broadcasted_iota(jnp.int32, (tm, tm), 0)
    rm = lax.broadcasted_iota(jnp.int32, (tm, tm), 1)
    earlier = jnp.where(rn > rm, 1.0, 0.0).astype(BF16)
    cum = _dot(earlier, onehot.astype(BF16)) + cnt_scr[...]
    idx_t = jnp.zeros((tm, width), jnp.int32)
    rank_t = jnp.zeros((tm, width), jnp.int32)
    gate_t = jnp.zeros((tm, width), F32)
    for k in range(TOP_K):
        rank_k = jnp.sum(jnp.where(colf == idxs[k], cum, 0.0), axis=-1, keepdims=True)
        idx_t = jnp.where(col == k, idxs[k].astype(jnp.int32), idx_t)
        rank_t = jnp.where(col == k, rank_k.astype(jnp.int32), rank_t)
        gate_t = jnp.where(col == k, es[k] / den, gate_t)
    idx_ref[...] = idx_t
    gate_ref[...] = gate_t
    rank_ref[...] = rank_t
    cnt_new = cnt_scr[...] + jnp.sum(onehot, axis=0, keepdims=True)
    cnt_scr[...] = cnt_new
    cnt_ref[...] = cnt_new


def _post_sublayer(alpha, x, y, g, lng, lnb):
    return _layer_norm(alpha * x + g * y, lng, lnb)


def _mixer_out_kernel(a_ref, w_ref, x_ref, g_ref, lng_ref, lnb_ref, sc2_ref, sh2_ref,
                      wr_ref, br_ref, xo_ref, h2_ref, idx_ref, gate_ref, rank_ref, cnt_ref, cnt_scr, *, alpha):
    y = _dot(a_ref[...], w_ref[...])
    xn = _post_sublayer(alpha, x_ref[...], y, g_ref[...], lng_ref[...], lnb_ref[...])
    xo_ref[...] = xn
    h2 = xn * (1.0 + sc2_ref[...]) + sh2_ref[...]
    h2_ref[...] = h2
    _route(h2, wr_ref, br_ref, cnt_scr, idx_ref, gate_ref, rank_ref, cnt_ref)


def _glu_out_kernel(y_ref, w1_ref, w2_ref, d_ref, sc1_ref, sh1_ref, x_ref, g_ref, lng_ref, lnb_ref,
                    sc2_ref, sh2_ref, wr_ref, br_ref, xo_ref, h2_ref, idx_ref, gate_ref, rank_ref, cnt_ref,
                    cnt_scr, *, alpha):
    x = x_ref[...]
    h = x * (1.0 + sc1_ref[...]) + sh1_ref[...]
    a = jax.nn.gelu(y_ref[...] + d_ref[...] * h).astype(BF16)
    y = _dot(a, w1_ref[...]) * jax.nn.sigmoid(_dot(a, w2_ref[...]))
    xn = _post_sublayer(alpha, x, y, g_ref[...], lng_ref[...], lnb_ref[...])
    xo_ref[...] = xn
    h2 = xn * (1.0 + sc2_ref[...]) + sh2_ref[...]
    h2_ref[...] = h2
    _route(h2, wr_ref, br_ref, cnt_scr, idx_ref, gate_ref, rank_ref, cnt_ref)


def _row_spec(tm, width):
    return pl.BlockSpec((tm, width), lambda i: (i, 0))


def _blk_vec_spec(tm, d):
    kb = tm // ROW_BLOCK
    return pl.BlockSpec((None, 1, d), lambda i: (i * kb, 0, 0))


def _const_spec(shape):
    return pl.BlockSpec(shape, lambda i: tuple(0 for _ in shape))


def _post_out_shapes(n, d):
    return (jax.ShapeDtypeStruct((n, d), F32), jax.ShapeDtypeStruct((n, d), F32),
            jax.ShapeDtypeStruct((n, ROUTER_PAD), jnp.int32), jax.ShapeDtypeStruct((n, ROUTER_PAD), F32),
            jax.ShapeDtypeStruct((n, ROUTER_PAD), jnp.int32), jax.ShapeDtypeStruct((1, ROUTER_PAD), F32))


def _post_out_specs(tm, d):
    return (_row_spec(tm, d), _row_spec(tm, d), _row_spec(tm, ROUTER_PAD), _row_spec(tm, ROUTER_PAD),
            _row_spec(tm, ROUTER_PAD), _const_spec((1, ROUTER_PAD)))


def _mixer_out(a, w, x, g1, lng, lnb, sc2, sh2, wr, br, alpha, tm=512):
    n, d = x.shape
    kin = a.shape[1]
    return pl.pallas_call(
        functools.partial(_mixer_out_kernel, alpha=alpha),
        out_shape=_post_out_shapes(n, d),
        grid=(n // tm,),
        in_specs=[_row_spec(tm, kin), _const_spec((kin, d)), _row_spec(tm, d), _blk_vec_spec(tm, d),
                  _const_spec((1, d)), _const_spec((1, d)), _blk_vec_spec(tm, d), _blk_vec_spec(tm, d),
                  _const_spec((d, ROUTER_PAD)), _const_spec((1, ROUTER_PAD))],
        out_specs=_post_out_specs(tm, d),
        scratch_shapes=[pltpu.VMEM((1, ROUTER_PAD), F32)],
        compiler_params=_cparams(("arbitrary",)),
        name="mixer_out_route",
    )(a, w, x, g1, lng, lnb, sc2, sh2, wr, br)


def _glu_out(y, w1, w2, dskip, sc1, sh1, x, g1, lng, lnb, sc2, sh2, wr, br, alpha, tm=512):
    n, d = x.shape
    return pl.pallas_call(
        functools.partial(_glu_out_kernel, alpha=alpha),
        out_shape=_post_out_shapes(n, d),
        grid=(n // tm,),
        in_specs=[_row_spec(tm, d), _const_spec((d, d)), _const_spec((d, d)), _const_spec((1, d)),
                  _blk_vec_spec(tm, d), _blk_vec_spec(tm, d), _row_spec(tm, d), _blk_vec_spec(tm, d),
                  _const_spec((1, d)), _const_spec((1, d)), _blk_vec_spec(tm, d), _blk_vec_spec(tm, d),
                  _const_spec((d, ROUTER_PAD)), _const_spec((1, ROUTER_PAD))],
        out_specs=_post_out_specs(tm, d),
        scratch_shapes=[pltpu.VMEM((1, ROUTER_PAD), F32)],
        compiler_params=_cparams(("arbitrary",)),
        name="mixer_out_route",
    )(y, w1, w2, dskip, sc1, sh1, x, g1, lng, lnb, sc2, sh2, wr, br)


def _rotary(x, cos, sin):
    half = x.shape[-1] // 2
    x1, x2 = x[:, :half], x[:, half:]
    return jnp.concatenate([x1 * cos - x2 * sin, x1 * sin + x2 * cos], axis=-1)


def _head_norm(o):
    mu = jnp.mean(o, axis=-1, keepdims=True)
    oc = o - mu
    var = jnp.mean(oc * oc, axis=-1, keepdims=True)
    return oc * lax.rsqrt(var + GN_EPS)


def _ret_fwd_kernel(meta_ref, lg_ref, q_ref, k_ref, v_ref, cos_ref, sin_ref, s0_ref,
                    st_in_ref, o_ref, st_ref, s_scr, *, nb, heads):
    del st_in_ref
    i = pl.program_id(0)
    start = meta_ref[M_START * nb + i]
    end = meta_ref[M_END * nb + i]
    sample = meta_ref[M_SAMPLE * nb + i]
    c = q_ref.shape[0]
    dk = q_ref.shape[1] // heads
    dv = v_ref.shape[1] // heads

    @pl.when(jnp.logical_and(start == 1, sample == 1))
    def _():
        s_scr[...] = s0_ref[...]

    @pl.when(jnp.logical_and(start == 1, sample == 0))
    def _():
        s_scr[...] = jnp.zeros_like(s_scr)

    cos = cos_ref[...]
    sin = sin_ref[...]
    rn = lax.broadcasted_iota(jnp.int32, (c, c), 0)
    rm = lax.broadcasted_iota(jnp.int32, (c, c), 1)
    rel = (rn - rm).astype(F32)
    idx = lax.broadcasted_iota(jnp.int32, (c, 1), 0).astype(F32)
    for h in range(heads):
        lgf = lg_ref[h]
        lgb = lg_ref[heads + h]
        q = _rotary(q_ref[:, h * dk:(h + 1) * dk].astype(F32), cos, sin) * (dk ** -0.5)
        k = _rotary(k_ref[:, h * dk:(h + 1) * dk].astype(F32), cos, sin)
        qb = q.astype(BF16)
        v = v_ref[:, h * dv:(h + 1) * dv]
        dmat = (jnp.where(rel >= 0, jnp.exp(lgf * jnp.maximum(rel, 0.0)), 0.0)
                + jnp.where(rel <= 0, jnp.exp(lgb * jnp.maximum(-rel, 0.0)), 0.0))
        scores = lax.dot_general(qb, k.astype(BF16), NT, preferred_element_type=F32) * dmat
        xi = jnp.exp(lgf * (idx + 1.0))
        zeta = jnp.exp(lgf * (c - 1.0 - idx))
        s_prev = s_scr[h]
        o_ref[:, h * dv:(h + 1) * dv] = (
            _dot(scores.astype(BF16), v) + _dot(qb, s_prev.astype(BF16)) * xi).astype(o_ref.dtype)
        g_chunk = jnp.exp(lgf * jnp.full((1, 1), float(c), F32))
        s_scr[h] = s_prev * g_chunk + lax.dot_general((k * zeta).astype(BF16), v, TN,
                                                       preferred_element_type=F32)

    @pl.when(jnp.logical_and(end == 1, sample == 0))
    def _():
        st_ref[...] = s_scr[...]


def _ret_bwd_kernel(meta_ref, lg_ref, q_ref, k_ref, v_ref, g_ref, cos_ref, sin_ref, s0_ref, of_ref,
                    st_in_ref, o_ref, st_ref, s_scr, *, nb, heads):
    del st_in_ref
    i = nb - 1 - pl.program_id(0)
    start = meta_ref[M_START * nb + i]
    end = meta_ref[M_END * nb + i]
    sample = meta_ref[M_SAMPLE * nb + i]
    c = q_ref.shape[0]
    dk = q_ref.shape[1] // heads
    dv = v_ref.shape[1] // heads

    @pl.when(jnp.logical_and(end == 1, sample == 1))
    def _():
        s_scr[...] = s0_ref[...]

    @pl.when(jnp.logical_and(end == 1, sample == 0))
    def _():
        s_scr[...] = jnp.zeros_like(s_scr)

    cos = cos_ref[...]
    sin = sin_ref[...]
    idx = lax.broadcasted_iota(jnp.int32, (c, 1), 0).astype(F32)
    for h in range(heads):
        lgb = lg_ref[heads + h]
        q = _rotary(q_ref[:, h * dk:(h + 1) * dk].astype(F32), cos, sin) * (dk ** -0.5)
        k = _rotary(k_ref[:, h * dk:(h + 1) * dk].astype(F32), cos, sin)
        v = v_ref[:, h * dv:(h + 1) * dv]
        xi = jnp.exp(lgb * (c - idx))
        zeta = jnp.exp(lgb * idx)
        s_prev = s_scr[h]
        o = of_ref[:, h * dv:(h + 1) * dv].astype(F32) + _dot(q.astype(BF16), s_prev.astype(BF16)) * xi
        g_chunk = jnp.exp(lgb * jnp.full((1, 1), float(c), F32))
        s_scr[h] = s_prev * g_chunk + lax.dot_general((k * zeta).astype(BF16), v, TN,
                                                       preferred_element_type=F32)
        gate = _silu(g_ref[:, h * dv:(h + 1) * dv].astype(F32))
        o_ref[:, h * dv:(h + 1) * dv] = (gate * _head_norm(o)).astype(o_ref.dtype)

    @pl.when(jnp.logical_and(start == 1, sample == 0))
    def _():
        st_ref[...] = s_scr[...]


def _retention(qkvg, meta, lg, cos_t, sin_t, state, new_state, li, nb):
    n = qkvg.shape[0]
    heads = RET_HEADS
    d = qkvg.shape[1] // 6
    dk, dv = d // heads, 2 * d // heads
    c = ROW_BLOCK
    kw = dict(nb=nb, heads=heads)
    r = lambda j: nb - 1 - j
    pos = lambda j, m: m[M_POS * nb + j]
    st_blk = (None, None, None, heads, dk, dv)
    any_spec = pl.BlockSpec(memory_space=pl.ANY)
    of, new_state = pl.pallas_call(
        functools.partial(_ret_fwd_kernel, **kw),
        out_shape=(jax.ShapeDtypeStruct((n, heads * dv), BF16),
                   jax.ShapeDtypeStruct(new_state.shape, new_state.dtype)),
        grid_spec=pltpu.PrefetchScalarGridSpec(
            num_scalar_prefetch=1, grid=(nb,),
            in_specs=[pl.BlockSpec(memory_space=pltpu.SMEM),
                      pl.BlockSpec((c, d), lambda j, m: (j, 0)),
                      pl.BlockSpec((c, d), lambda j, m: (j, 1)),
                      pl.BlockSpec((c, 2 * d), lambda j, m: (j, 1)),
                      pl.BlockSpec((c, dk // 2), lambda j, m: (pos(j, m), 0)),
                      pl.BlockSpec((c, dk // 2), lambda j, m: (pos(j, m), 0)),
                      pl.BlockSpec(st_blk, lambda j, m: (m[M_SIDX * nb + j], li, 0, 0, 0, 0)),
                      any_spec],
            out_specs=(pl.BlockSpec((c, 2 * d), lambda j, m: (j, 0)),
                       pl.BlockSpec(st_blk, lambda j, m: (m[M_PIDX * nb + j], li, 0, 0, 0, 0))),
            scratch_shapes=[pltpu.VMEM((heads, dk, dv), F32)]),
        input_output_aliases={8: 1},
        compiler_params=_cparams(("arbitrary",)),
        name="ret_fwd",
    )(meta, lg, qkvg, qkvg, qkvg, cos_t, sin_t, state, new_state)
    out, new_state = pl.pallas_call(
        functools.partial(_ret_bwd_kernel, **kw),
        out_shape=(jax.ShapeDtypeStruct((n, heads * dv), BF16),
                   jax.ShapeDtypeStruct(new_state.shape, new_state.dtype)),
        grid_spec=pltpu.PrefetchScalarGridSpec(
            num_scalar_prefetch=1, grid=(nb,),
            in_specs=[pl.BlockSpec(memory_space=pltpu.SMEM),
                      pl.BlockSpec((c, d), lambda j, m: (r(j), 0)),
                      pl.BlockSpec((c, d), lambda j, m: (r(j), 1)),
                      pl.BlockSpec((c, 2 * d), lambda j, m: (r(j), 1)),
                      pl.BlockSpec((c, 2 * d), lambda j, m: (r(j), 2)),
                      pl.BlockSpec((c, dk // 2), lambda j, m: (pos(r(j), m), 0)),
                      pl.BlockSpec((c, dk // 2), lambda j, m: (pos(r(j), m), 0)),
                      pl.BlockSpec(st_blk, lambda j, m: (m[M_SIDX * nb + r(j)], li, 1, 0, 0, 0)),
                      pl.BlockSpec((c, 2 * d), lambda j, m: (r(j), 0)),
                      any_spec],
            out_specs=(pl.BlockSpec((c, 2 * d), lambda j, m: (r(j), 0)),
                       pl.BlockSpec(st_blk, lambda j, m: (m[M_PIDX * nb + r(j)], li, 1, 0, 0, 0))),
            scratch_shapes=[pltpu.VMEM((heads, dk, dv), F32)]),
        input_output_aliases={10: 1},
        compiler_params=_cparams(("arbitrary",)),
        name="ret_bwd",
    )(meta, lg, qkvg, qkvg, qkvg, qkvg, cos_t, sin_t, state, of, new_state)
    return out, new_state


def _gla_gate_kernel(x_ref, sc_ref, sh_ref, w1_ref, w2_ref, b_ref, o_ref):
    h = (x_ref[...] * (1.0 + sc_ref[...]) + sh_ref[...]).astype(BF16)
    u = _dot(h, w1_ref[...]).astype(BF16)
    z = _dot(u, w2_ref[...]) + b_ref[...]
    log_sig = jnp.minimum(z, 0.0) - jnp.log(1.0 + jnp.exp(-jnp.abs(z)))
    o_ref[...] = log_sig / GLA_TAU


def _gla_gate(x, sc, sh, w1, w2, b, tm=512):
    n, d = x.shape
    r2 = w1.shape[1]
    nk = w2.shape[1]
    return pl.pallas_call(
        _gla_gate_kernel,
        out_shape=jax.ShapeDtypeStruct((n, nk), F32),
        grid=(n // tm,),
        in_specs=[_row_spec(tm, d), _blk_vec_spec(tm, d), _blk_vec_spec(tm, d), _const_spec((d, r2)),
                  _const_spec((r2, nk)), _const_spec((1, nk))],
        out_specs=_row_spec(tm, nk),
        compiler_params=_cparams(("parallel",)),
    )(x, sc, sh, w1, w2, b)


def _cumsum_rows(tri, x):
    hi = x.astype(BF16)
    r1 = x - hi.astype(F32)
    mid = r1.astype(BF16)
    lo = (r1 - mid.astype(F32)).astype(BF16)
    return _dot(tri, hi) + _dot(tri, mid) + _dot(tri, lo)


def _gla_chunk(q, k, v, la, s_t, tri, mask, mid_row, last_row):
    b = _cumsum_rows(tri, la)
    r = b[mid_row:mid_row + 1, :]
    bl = b[last_row:last_row + 1, :]
    qe = (q * jnp.exp(b - r)).astype(BF16)
    ke = (k * jnp.exp(r - b)).astype(BF16)
    qi = (q * jnp.exp(b)).astype(BF16)
    kd = (k * jnp.exp(bl - b)).astype(BF16)
    sc = lax.dot_general(qe, ke, NT, preferred_element_type=F32)
    sc = jnp.where(mask, sc, 0.0).astype(BF16)
    o = _dot(sc, v) + lax.dot_general(qi, s_t.astype(BF16), NT, preferred_element_type=F32)
    s_new = s_t * jnp.exp(bl) + lax.dot_general(v, kd, TN, preferred_element_type=F32)
    return o, s_new


def _gla_fwd_kernel(meta_ref, q_ref, k_ref, v_ref, la_ref, s0_ref, o_ref, st_ref, s_scr, *, nb, heads):
    i = pl.program_id(0)
    start = meta_ref[M_START * nb + i]
    end = meta_ref[M_END * nb + i]
    sample = meta_ref[M_SAMPLE * nb + i]
    dk = q_ref.shape[1] // heads
    dv = v_ref.shape[1] // heads
    cc = GLA_CHUNK

    @pl.when(jnp.logical_and(start == 1, sample == 1))
    def _():
        for h in range(heads):
            s_scr[h] = s0_ref[h].T

    @pl.when(jnp.logical_and(start == 1, sample == 0))
    def _():
        s_scr[...] = jnp.zeros_like(s_scr)

    rn = lax.broadcasted_iota(jnp.int32, (cc, cc), 0)
    rm = lax.broadcasted_iota(jnp.int32, (cc, cc), 1)
    mask = rn >= rm
    tri = jnp.where(mask, 1.0, 0.0).astype(BF16)
    for h in range(heads):
        ks = slice(h * dk, (h + 1) * dk)
        vs = slice(h * dv, (h + 1) * dv)
        s_t = s_scr[h]
        for ci in range(q_ref.shape[0] // cc):
            sl = slice(ci * cc, (ci + 1) * cc)
            q = q_ref[sl, ks].astype(F32) * (dk ** -0.5)
            o, s_t = _gla_chunk(q, k_ref[sl, ks].astype(F32), v_ref[sl, vs], la_ref[sl, ks], s_t, tri, mask,
                                cc // 2 - 1, cc - 1)
            o_ref[sl, vs] = o.astype(o_ref.dtype)
        s_scr[h] = s_t

    @pl.when(jnp.logical_and(end == 1, sample == 0))
    def _():
        for h in range(heads):
            st_ref[h] = s_scr[h].T


def _gla_bwd_kernel(meta_ref, q_ref, k_ref, v_ref, r_ref, la_ref, s0_ref, of_ref, o_ref, st_ref, s_scr,
                    *, nb, heads):
    i = nb - 1 - pl.program_id(0)
    start = meta_ref[M_START * nb + i]
    end = meta_ref[M_END * nb + i]
    sample = meta_ref[M_SAMPLE * nb + i]
    dk = q_ref.shape[1] // heads
    dv = v_ref.shape[1] // heads
    cc = GLA_CHUNK

    @pl.when(jnp.logical_and(end == 1, sample == 1))
    def _():
        for h in range(heads):
            s_scr[h] = s0_ref[h].T

    @pl.when(jnp.logical_and(end == 1, sample == 0))
    def _():
        s_scr[...] = jnp.zeros_like(s_scr)

    rn = lax.broadcasted_iota(jnp.int32, (cc, cc), 0)
    rm = lax.broadcasted_iota(jnp.int32, (cc, cc), 1)
    mask = rn <= rm
    tri = jnp.where(mask, 1.0, 0.0).astype(BF16)
    for h in range(heads):
        ks = slice(h * dk, (h + 1) * dk)
        vs = slice(h * dv, (h + 1) * dv)
        s_t = s_scr[h]
        for ci in reversed(range(q_ref.shape[0] // cc)):
            sl = slice(ci * cc, (ci + 1) * cc)
            q = q_ref[sl, ks].astype(F32) * (dk ** -0.5)
            ob, s_t = _gla_chunk(q, k_ref[sl, ks].astype(F32), v_ref[sl, vs], la_ref[sl, ks], s_t, tri, mask,
                                 cc // 2, 0)
            o = of_ref[sl, vs].astype(F32) + ob
            o_ref[sl, vs] = (_silu(r_ref[sl, vs].astype(F32)) * _head_norm(o)).astype(o_ref.dtype)
        s_scr[h] = s_t

    @pl.when(jnp.logical_and(start == 1, sample == 0))
    def _():
        for h in range(heads):
            st_ref[h] = s_scr[h].T


def _gla(qkvr, la, meta, state, li, nb, n_prompt):
    n = qkvr.shape[0]
    heads = GLA_HEADS
    d = qkvr.shape[1] // 3
    dk, dv = d // 2 // heads, d // heads
    c = ROW_BLOCK
    kw = dict(nb=nb, heads=heads)
    r = lambda j: nb - 1 - j
    st_blk = (None, None, None, heads, dk, dv)
    of, st_f = pl.pallas_call(
        functools.partial(_gla_fwd_kernel, **kw),
        out_shape=(jax.ShapeDtypeStruct((n, d), BF16), jax.ShapeDtypeStruct((n_prompt, heads, dk, dv), F32)),
        grid_spec=pltpu.PrefetchScalarGridSpec(
            num_scalar_prefetch=1, grid=(nb,),
            in_specs=[pl.BlockSpec((c, d // 2), lambda j, m: (j, 0)),
                      pl.BlockSpec((c, d // 2), lambda j, m: (j, 1)),
                      pl.BlockSpec((c, d), lambda j, m: (j, 1)),
                      pl.BlockSpec((c, d // 2), lambda j, m: (j, 0)),
                      pl.BlockSpec(st_blk, lambda j, m: (m[M_SIDX * nb + j], li, 0, 0, 0, 0))],
            out_specs=(pl.BlockSpec((c, d), lambda j, m: (j, 0)),
                       pl.BlockSpec((None, heads, dk, dv), lambda j, m: (m[M_PIDX * nb + j], 0, 0, 0))),
            scratch_shapes=[pltpu.VMEM((heads, dv, dk), F32)]),
        compiler_params=_cparams(("arbitrary",)),
        name="gla_fwd",
    )(meta, qkvr, qkvr, qkvr, la, state)
    out, st_b = pl.pallas_call(
        functools.partial(_gla_bwd_kernel, **kw),
        out_shape=(jax.ShapeDtypeStruct((n, d), BF16), jax.ShapeDtypeStruct((n_prompt, heads, dk, dv), F32)),
        grid_spec=pltpu.PrefetchScalarGridSpec(
            num_scalar_prefetch=1, grid=(nb,),
            in_specs=[pl.BlockSpec((c, d // 2), lambda j, m: (r(j), 0)),
                      pl.BlockSpec((c, d // 2), lambda j, m: (r(j), 1)),
                      pl.BlockSpec((c, d), lambda j, m: (r(j), 1)),
                      pl.BlockSpec((c, d), lambda j, m: (r(j), 2)),
                      pl.BlockSpec((c, d // 2), lambda j, m: (r(j), 1)),
                      pl.BlockSpec(st_blk, lambda j, m: (m[M_SIDX * nb + r(j)], li, 1, 0, 0, 0)),
                      pl.BlockSpec((c, d), lambda j, m: (r(j), 0))],
            out_specs=(pl.BlockSpec((c, d), lambda j, m: (r(j), 0)),
                       pl.BlockSpec((None, heads, dk, dv), lambda j, m: (m[M_PIDX * nb + r(j)], 0, 0, 0))),
            scratch_shapes=[pltpu.VMEM((heads, dv, dk), F32)]),
        compiler_params=_cparams(("arbitrary",)),
        name="gla_bwd",
    )(meta, qkvr, qkvr, qkvr, qkvr, la, state, of)
    return out, jnp.stack([st_f, st_b], axis=1)


def _row_tile(rows, cap):
    tile = min(cap, rows) // 8 * 8
    while rows % tile:
        tile -= 8
    return tile


def _s5_chunk_rows(x_ref, sc_ref, sh_ref, lhs):
    lanes = x_ref.shape[2]
    for t in range(x_ref.shape[1]):
        lhs[:, t * lanes:(t + 1) * lanes] = (
            x_ref[:, t, :] * (1.0 + sc_ref[:, 0, :]) + sh_ref[:, 0, :]).astype(BF16)


def _s5_in_kernel(x_ref, sc_ref, sh_ref, w_ref, wf_ref, wb_ref, lhs):
    _s5_chunk_rows(x_ref, sc_ref, sh_ref, lhs)
    z = _dot(lhs[...], w_ref[...])
    half = z.shape[1] // 2
    p2 = wf_ref.shape[2]
    for h in range(wf_ref.shape[1]):
        wf_ref[:, h, :] = z[:, h * p2:(h + 1) * p2]
        wb_ref[:, h, :] = z[:, half + h * p2:half + (h + 1) * p2]


def _s5_in(x3, sc3, sh3, win8, tr):
    r, t, d = x3.shape
    nj, kc, ncol = win8.shape
    lanes = kc // t
    lg = S5_LANE_GROUPS
    p2 = ncol // 2 // lg
    x_spec = pl.BlockSpec((tr, t, lanes), lambda ji, ri: (ri, 0, ji))
    v_spec = pl.BlockSpec((tr, 1, lanes), lambda ji, ri: (ri, 0, ji))
    return pl.pallas_call(
        _s5_in_kernel,
        out_shape=(jax.ShapeDtypeStruct((r, nj * lg, p2), F32), jax.ShapeDtypeStruct((r, nj * lg, p2), F32)),
        grid=(nj, r // tr),
        in_specs=[x_spec, v_spec, v_spec, pl.BlockSpec((None, kc, ncol), lambda ji, ri: (ji, 0, 0))],
        out_specs=(pl.BlockSpec((tr, lg, p2), lambda ji, ri: (ri, ji, 0)),
                   pl.BlockSpec((tr, lg, p2), lambda ji, ri: (ri, ji, 0))),
        scratch_shapes=[pltpu.VMEM((tr, kc), BF16)],
        compiler_params=_cparams(("parallel", "parallel")),
        name="s5_in",
    )(x3, sc3, sh3, win8)


def _s5_scan_kernel(meta_ref, w_ref, a1_ref, a2_ref, s0_ref, xp_ref, st_ref, x_scr, *, nb, reverse):
    j = pl.program_id(0)
    i = nb - 1 - j if reverse else j
    first = meta_ref[(M_END if reverse else M_START) * nb + i]
    last = meta_ref[(M_START if reverse else M_END) * nb + i]

    @pl.when(first == 1)
    def _():
        x_scr[...] = s0_ref[...]

    a1 = a1_ref[...]
    a2 = a2_ref[...]
    x = x_scr[...]
    half = x.shape[1] // 2
    nrow = w_ref.shape[0]
    order = range(nrow - 1, -1, -1) if reverse else range(nrow)
    for c in order:
        xp_ref[c] = x
        x = x * a1 + pltpu.roll(x, half, axis=1) * a2 + w_ref[c]
    x_scr[...] = x

    @pl.when(last == 1)
    def _():
        st_ref[...] = x


def _s5_scan(w, a1, a2, s0, meta, nb, nseq, reverse):
    r, g, p2 = w.shape
    cr = r // nb
    blk = (lambda j: nb - 1 - j) if reverse else (lambda j: j)
    return pl.pallas_call(
        functools.partial(_s5_scan_kernel, nb=nb, reverse=reverse),
        out_shape=(jax.ShapeDtypeStruct((r, g, p2), F32), jax.ShapeDtypeStruct((nseq, g, p2), F32)),
        grid_spec=pltpu.PrefetchScalarGridSpec(
            num_scalar_prefetch=1, grid=(nb,),
            in_specs=[pl.BlockSpec((cr, g, p2), lambda j, m: (blk(j), 0, 0)),
                      pl.BlockSpec((g, p2), lambda j, m: (0, 0)),
                      pl.BlockSpec((g, p2), lambda j, m: (0, 0)),
                      pl.BlockSpec((None, g, p2), lambda j, m: (m[M_SEQ * nb + blk(j)], 0, 0))],
            out_specs=(pl.BlockSpec((cr, g, p2), lambda j, m: (blk(j), 0, 0)),
                       pl.BlockSpec((None, g, p2), lambda j, m: (m[M_SEQ * nb + blk(j)], 0, 0))),
            scratch_shapes=[pltpu.VMEM((g, p2), F32)]),
        compiler_params=_cparams(("arbitrary",)),
        name="s5_scan",
    )(meta, w, a1, a2, s0)


def _s5_out_kernel(x_ref, sc_ref, sh_ref, m_ref, xf_ref, xb_ref, c_ref, y_ref, lhs, states):
    _s5_chunk_rows(x_ref, sc_ref, sh_ref, lhs)
    p2 = xf_ref.shape[2]
    ns = xf_ref.shape[1] * p2
    for h in range(xf_ref.shape[1]):
        states[:, h * p2:(h + 1) * p2] = xf_ref[:, h, :].astype(BF16)
        states[:, ns + h * p2:ns + (h + 1) * p2] = xb_ref[:, h, :].astype(BF16)
    z = _dot(lhs[...], m_ref[...]) + _dot(states[...], c_ref[...])
    lanes = y_ref.shape[2]
    for t in range(y_ref.shape[1]):
        y_ref[:, t, :] = z[:, t * lanes:(t + 1) * lanes]


def _s5_out(x3, sc3, sh3, m8, xf, xb, c8, tr):
    r, t, d = x3.shape
    nj, kc, _ = m8.shape
    lanes = kc // t
    ns = c8.shape[1] // 2
    lg = S5_LANE_GROUPS
    x_spec = pl.BlockSpec((tr, t, lanes), lambda ji, ri: (ri, 0, ji))
    v_spec = pl.BlockSpec((tr, 1, lanes), lambda ji, ri: (ri, 0, ji))
    s_spec = pl.BlockSpec((tr, lg, ns // lg), lambda ji, ri: (ri, ji, 0))
    return pl.pallas_call(
        _s5_out_kernel,
        out_shape=jax.ShapeDtypeStruct((r, t, d), F32),
        grid=(nj, r // tr),
        in_specs=[x_spec, v_spec, v_spec, pl.BlockSpec((None, kc, kc), lambda ji, ri: (ji, 0, 0)),
                  s_spec, s_spec, pl.BlockSpec((None, 2 * ns, kc), lambda ji, ri: (ji, 0, 0))],
        out_specs=x_spec,
        scratch_shapes=[pltpu.VMEM((tr, kc), BF16), pltpu.VMEM((tr, 2 * ns), BF16)],
        compiler_params=_cparams(("parallel", "parallel")),
        name="s5_out",
    )(x3, sc3, sh3, m8, xf, xb, c8)


def _s5_expand_kernel(a_ref, rep_ref, o_ref, *, row_div, col_div):
    z = _dot(a_ref[...], rep_ref[...])
    rows, cols = z.shape
    row0 = pl.program_id(1) * rows
    gi = ((lax.broadcasted_iota(jnp.int32, (rows, cols), 0) + row0) // row_div) % S5_LANE_GROUPS
    hi = (lax.broadcasted_iota(jnp.int32, (rows, cols), 1) // col_div) % S5_LANE_GROUPS
    o_ref[...] = jnp.where(gi == hi, z, 0.0).astype(o_ref.dtype)


def _s5_expand(a, rep, row_div, col_div, tr=512):
    nj, rows, kin = a.shape
    ncol = rep.shape[1]
    return pl.pallas_call(
        functools.partial(_s5_expand_kernel, row_div=row_div, col_div=col_div),
        out_shape=jax.ShapeDtypeStruct((nj, rows, ncol), BF16),
        grid=(nj, rows // tr),
        in_specs=[pl.BlockSpec((None, tr, kin), lambda j, i: (j, i, 0)),
                  pl.BlockSpec((kin, ncol), lambda j, i: (0, 0))],
        out_specs=pl.BlockSpec((None, tr, ncol), lambda j, i: (j, i, 0)),
        compiler_params=_cparams(("parallel", "parallel")),
        name="s5_expand",
    )(a, rep)


def _s5_tables(a_re, a_im, log_step, b_re, b_im, c_re, c_im):
    t = S5_CHUNK
    step = jnp.exp(log_step.astype(F32))[..., None]
    ar, ai = a_re.astype(F32), a_im.astype(F32)
    mag = jnp.exp(ar * step)
    ab_re, ab_im = mag * jnp.cos(ai * step), mag * jnp.sin(ai * step)
    den = ar * ar + ai * ai
    f_re = ((ab_re - 1.0) * ar + ab_im * ai) / den
    f_im = (ab_im * ar - (ab_re - 1.0) * ai) / den
    br, bi = b_re.astype(F32), b_im.astype(F32)
    bb_re = f_re[..., None] * br - f_im[..., None] * bi
    bb_im = f_re[..., None] * bi + f_im[..., None] * br
    cr, ci = c_re.astype(F32), c_im.astype(F32)
    tau = jnp.arange(t + 1, dtype=F32)[:, None, None, None]
    pmag = jnp.exp(ar[None] * step[None] * tau)
    pw_re = pmag * jnp.cos(ai[None] * step[None] * tau)
    pw_im = pmag * jnp.sin(ai[None] * step[None] * tau)
    g, p, ch = br.shape[1], br.shape[2], br.shape[3]

    def cmul(xr, xi, yr, yi):
        return xr * yr - xi * yi, xr * yi + xi * yr

    def inject(d, powers):
        pr = pw_re[powers, d][:, :, :, None]
        pi = pw_im[powers, d][:, :, :, None]
        wr, wi = cmul(pr, pi, bb_re[d][None], bb_im[d][None])
        w = jnp.concatenate([wr, wi], axis=2)
        return w.transpose(1, 0, 3, 2).reshape(g, t * ch, 2 * p)
    w_in = jnp.concatenate([inject(0, jnp.arange(t - 1, -1, -1)), inject(1, jnp.arange(t))], axis=-1)

    def readout(d, powers):
        pr = pw_re[powers, d][:, :, None, :]
        pi = pw_im[powers, d][:, :, None, :]
        kr, ki = cmul(cr[d][None], ci[d][None], pr, pi)
        m = jnp.concatenate([kr, -ki], axis=3)
        return m.transpose(1, 3, 0, 2).reshape(g, 2 * p, t * ch)
    c_f = readout(0, jnp.arange(1, t + 1))
    c_b = readout(1, jnp.arange(t, 0, -1))

    def lag_kernels(d):
        pr = pw_re[:t, d][:, :, :, None]
        pi = pw_im[:t, d][:, :, :, None]
        wr, wi = cmul(pr, pi, bb_re[d][None], bb_im[d][None])
        return (jnp.einsum('gop,tgpc->tgoc', cr[d], wr) - jnp.einsum('gop,tgpc->tgoc', ci[d], wi))
    kf, kb = lag_kernels(0), lag_kernels(1)
    ii = np.arange(t)
    lag = ii[None, :] - ii[:, None]
    sel_f = (lag[:, :, None] == ii[None, None, :]).astype(np.float32)
    sel_b = (-lag[:, :, None] == ii[None, None, :]).astype(np.float32)
    m = (jnp.einsum('jit,tgoc->gjcio', sel_f, kf, precision=lax.Precision.HIGHEST)
         + jnp.einsum('jit,tgoc->gjcio', sel_b, kb, precision=lax.Precision.HIGHEST))

    at_re, at_im = pw_re[t], pw_im[t]
    a1 = jnp.concatenate([at_re, at_re], axis=-1)
    a2 = jnp.concatenate([-at_im, at_im], axis=-1)

    lg = S5_LANE_GROUPS
    nj = g // lg
    kc = t * lg * ch

    def rows_tgc(a):
        return a.reshape(nj, lg, t, ch, a.shape[-1]).transpose(0, 2, 1, 3, 4).reshape(nj, kc, a.shape[-1]).astype(BF16)

    eye_x = lambda nx, ny: np.einsum('xa,yb->xyab', np.eye(nx), np.eye(ny))[:, :, :, None, :] * np.ones((1, 1, 1, lg, 1))
    rep_m = jnp.asarray(eye_x(t, ch).reshape(t * ch, t * lg * ch), BF16)
    rep_w = jnp.asarray(eye_x(2, 2 * p).reshape(4 * p, 2 * lg * 2 * p), BF16)
    m8 = _s5_expand(rows_tgc(m.reshape(g, t * ch, t * ch)), rep_m, ch, ch)
    win8 = _s5_expand(rows_tgc(w_in), rep_w, ch, 2 * p)
    c_c = jnp.concatenate([c_f.reshape(nj, lg * 2 * p, t * ch), c_b.reshape(nj, lg * 2 * p, t * ch)], axis=1)
    c8 = _s5_expand(c_c.astype(BF16), rep_m, 2 * p, ch)
    return win8, m8, c8, a1, a2


def _dispatch_kernel(pend_ref, dest_ref, h_ref, xs_ref, zero_scr, sem, zsem, *, n_exp):
    tm = h_ref.shape[0]
    bm = zero_scr.shape[0]

    @pl.when(pl.program_id(0) == 0)
    def _():
        zero_scr[...] = jnp.zeros_like(zero_scr)
        for e in range(n_exp):
            prev = pend_ref[e - 1] if e else 0

            @pl.when(pend_ref[e] > prev)
            def _():
                off = pl.multiple_of(pend_ref[e] - bm, bm)
                pltpu.make_async_copy(zero_scr, xs_ref.at[pl.ds(off, bm)], zsem).start()
        n_used = pend_ref[n_exp - 1] // bm
        n_blocks = xs_ref.shape[0] // bm

        def zero_tail(b, carry):
            off = pl.multiple_of(b * bm, bm)
            pltpu.make_async_copy(zero_scr, xs_ref.at[pl.ds(off, bm)], zsem).start()
            return carry

        def wait_tail(b, carry):
            pltpu.make_async_copy(zero_scr, xs_ref.at[pl.ds(0, bm)], zsem).wait()
            return carry

        lax.fori_loop(n_used, n_blocks, zero_tail, 0)
        for e in range(n_exp):
            prev = pend_ref[e - 1] if e else 0

            @pl.when(pend_ref[e] > prev)
            def _():
                pltpu.make_async_copy(zero_scr, xs_ref.at[pl.ds(0, bm)], zsem).wait()
        lax.fori_loop(n_used, n_blocks, wait_tail, 0)

    def issue(t, carry):
        for k in range(TOP_K):
            pltpu.make_async_copy(h_ref.at[pl.ds(t, 1)], xs_ref.at[pl.ds(dest_ref[t * TOP_K + k], 1)], sem).start()
        return carry

    lax.fori_loop(0, tm, issue, 0)
    for _ in range(TOP_K):
        pltpu.make_async_copy(h_ref, xs_ref.at[pl.ds(0, tm)], sem).wait()


def _dispatch(pend, dest, h2, n_rows, tm=512):
    n, d = h2.shape
    return pl.pallas_call(
        functools.partial(_dispatch_kernel, n_exp=pend.shape[0]),
        out_shape=jax.ShapeDtypeStruct((n_rows, d), F32),
        grid_spec=pltpu.PrefetchScalarGridSpec(
            num_scalar_prefetch=1, grid=(n // tm,),
            in_specs=[pl.BlockSpec((tm * TOP_K,), lambda i, p: (i,), memory_space=pltpu.SMEM),
                      pl.BlockSpec((tm, d), lambda i, p: (i, 0))],
            out_specs=pl.BlockSpec(memory_space=pl.ANY),
            scratch_shapes=[pltpu.VMEM((MOE_BLOCK, d), F32), pltpu.SemaphoreType.DMA,
                            pltpu.SemaphoreType.DMA]),
        compiler_params=_cparams(("arbitrary",)),
        name="moe_dispatch",
    )(pend, dest, h2)


def _ffn_kernel(sp_ref, x_ref, wgu_ref, bgu_ref, wd_ref, bd_ref, o_ref, wgu_s, wd_s, *, n_blocks):
    b = pl.program_id(0)
    used = b < sp_ref[3 * n_blocks]
    valid = sp_ref[2 * n_blocks + b]
    dff = wd_ref.shape[0]
    half = x_ref.shape[0] // 2

    @pl.when(jnp.logical_and(used, sp_ref[n_blocks + b] == 1))
    def _():
        wgu_s[...] = wgu_ref[...].astype(BF16)
        wd_s[...] = wd_ref[...].astype(BF16)

    def ffn_rows(rows):
        gu = _dot(x_ref[rows, :].astype(BF16), wgu_s[...]) + bgu_ref[...]
        gate = jnp.minimum(gu[:, :dff], SWIGLU_LIMIT)
        lin = jnp.clip(gu[:, dff:], -SWIGLU_LIMIT, SWIGLU_LIMIT)
        act = gate * jax.nn.sigmoid(SWIGLU_ALPHA * gate) * (lin + 1.0)
        o_ref[rows, :] = _dot(act.astype(BF16), wd_s[...]) + bd_ref[...]

    @pl.when(jnp.logical_and(used, valid > half))
    def _():
        ffn_rows(slice(None))

    @pl.when(jnp.logical_and(used, valid <= half))
    def _():
        ffn_rows(slice(0, half))
        o_ref[half:, :] = jnp.zeros((half, o_ref.shape[1]), o_ref.dtype)

    @pl.when(jnp.logical_not(used))
    def _():
        o_ref[...] = jnp.zeros_like(o_ref)


def _ffn(sp, xs, wgu, bgu, wd, bd, layer):
    n_rows, d = xs.shape
    bm = MOE_BLOCK
    n_blocks = n_rows // bm
    dff = wd.shape[2]
    return pl.pallas_call(
        functools.partial(_ffn_kernel, n_blocks=n_blocks),
        out_shape=jax.ShapeDtypeStruct((n_rows, d), F32),
        grid_spec=pltpu.PrefetchScalarGridSpec(
            num_scalar_prefetch=1, grid=(n_blocks,),
            in_specs=[pl.BlockSpec((bm, d), lambda b, sp: (jnp.minimum(b, sp[3 * n_blocks] - 1), 0)),
                      pl.BlockSpec((None, None, d, 2 * dff), lambda b, sp: (layer, sp[b], 0, 0)),
                      pl.BlockSpec((None, None, 1, 2 * dff), lambda b, sp: (layer, sp[b], 0, 0)),
                      pl.BlockSpec((None, None, dff, d), lambda b, sp: (layer, sp[b], 0, 0)),
                      pl.BlockSpec((None, None, 1, d), lambda b, sp: (layer, sp[b], 0, 0))],
            out_specs=pl.BlockSpec((bm, d), lambda b, sp: (b, 0)),
            scratch_shapes=[pltpu.VMEM((d, 2 * dff), BF16), pltpu.VMEM((dff, d), BF16)]),
        compiler_params=_cparams(("arbitrary",)),
        name="moe_ffn",
    )(sp, xs, wgu, bgu, wd, bd)


def _combine_kernel(dest_ref, yb_ref, gate_ref, x_ref, g_ref, lng_ref, lnb_ref, xo_ref, buf, sem, *, alpha):
    tm = x_ref.shape[0]

    def issue(t, carry):
        for k in range(TOP_K):
            pltpu.make_async_copy(yb_ref.at[pl.ds(dest_ref[t * TOP_K + k], 1)], buf.at[k, pl.ds(t, 1)], sem).start()
        return carry

    lax.fori_loop(0, tm, issue, 0)
    for k in range(TOP_K):
        pltpu.make_async_copy(yb_ref.at[pl.ds(0, tm)], buf.at[k], sem).wait()
    gates = gate_ref[...]
    y = gates[:, 0:1] * buf[0]
    for k in range(1, TOP_K):
        y = y + gates[:, k:k + 1] * buf[k]
    xo_ref[...] = _post_sublayer(alpha, x_ref[...], y, g_ref[...], lng_ref[...], lnb_ref[...])


def _combine(dest, yb, gate_t, x, g2, lng, lnb, alpha, tm=512):
    n, d = x.shape
    return pl.pallas_call(
        functools.partial(_combine_kernel, alpha=alpha),
        out_shape=jax.ShapeDtypeStruct((n, d), F32),
        grid=(n // tm,),
        in_specs=[pl.BlockSpec((tm * TOP_K,), lambda i: (i,), memory_space=pltpu.SMEM),
                  pl.BlockSpec(memory_space=pl.ANY),
                  _row_spec(tm, ROUTER_PAD), _row_spec(tm, d), _blk_vec_spec(tm, d), _const_spec((1, d)),
                  _const_spec((1, d))],
        out_specs=_row_spec(tm, d),
        scratch_shapes=[pltpu.VMEM((TOP_K, tm, d), F32), pltpu.SemaphoreType.DMA],
        compiler_params=_cparams(("arbitrary",)),
        name="moe_combine",
    )(dest, yb, gate_t, x, g2, lng, lnb)


def _moe(x, h2, idx_t, gate_t, rank_t, cnt, g2, lng, lnb, wgu, bgu, wd, bd, layer, alpha):
    n_tok, d = h2.shape
    n_exp = wgu.shape[1]
    bm = MOE_BLOCK
    n_blocks = -(-n_tok * TOP_K // bm) + n_exp
    counts = cnt[0, :n_exp].astype(jnp.int32)
    padded = (counts + bm - 1) // bm * bm
    pend = jnp.cumsum(padded)
    pstart = pend - padded
    e_iota = jnp.arange(n_exp, dtype=jnp.int32)
    dest = rank_t[:, :TOP_K] + jnp.sum(jnp.where(idx_t[:, :TOP_K, None] == e_iota, pstart, 0), axis=-1)
    dest = dest.reshape(-1).astype(jnp.int32)
    blk_row = jnp.arange(n_blocks, dtype=jnp.int32) * bm
    block_e = jnp.minimum(jnp.sum(pend[None, :] <= blk_row[:, None], axis=1), n_exp - 1).astype(jnp.int32)
    first = jnp.concatenate([jnp.ones((1,), jnp.int32), (block_e[1:] != block_e[:-1]).astype(jnp.int32)])
    row_end = jnp.sum(jnp.where(block_e[:, None] == e_iota, pstart + counts, 0), axis=1)
    valid = jnp.clip(row_end - blk_row, 0, bm).astype(jnp.int32)
    sp = jnp.concatenate([block_e, first, valid, (pend[-1:] // bm).astype(jnp.int32)])
    xs = _dispatch(pend.astype(jnp.int32), dest, h2, n_blocks * bm)
    yb = _ffn(sp, xs, wgu, bgu, wd, bd, layer)
    return _combine(dest, yb, gate_t, x, g2, lng, lnb, alpha)


def _block_meta(n_p, l_p, n_s, l_s):
    rb = ROW_BLOCK
    assert l_p % rb == 0 and l_s % rb == 0
    rows = []
    for s in range(n_p):
        for j in range(l_p // rb):
            rows.append((int(j == 0), int(j == l_p // rb - 1), 0, 0, s, 0, s))
    for s in range(n_s):
        for j in range(l_s // rb):
            rows.append((int(j == 0), int(j == l_s // rb - 1), 1, s, n_p - 1, 1 + j, n_p + s))
    meta = np.asarray(rows, np.int32).T
    cond = np.asarray([0] * (n_p * l_p // rb) + [1 + s for s in range(n_s) for _ in range(l_s // rb)], np.int32)
    return meta, cond


def _rotary_tables(l_s, dk):
    rows = l_s // GRID_W
    row = jnp.repeat(jnp.arange(rows, dtype=F32), GRID_W)
    col = jnp.tile(jnp.arange(GRID_W, dtype=F32), rows)
    n_freq = dk // 4
    inv = ROPE_BASE ** (-jnp.arange(n_freq, dtype=F32) / n_freq)
    ang = jnp.concatenate([row[:, None] * inv, col[:, None] * inv], axis=-1)
    ident = jnp.zeros((ROW_BLOCK, dk // 2), F32)
    return (jnp.concatenate([ident + 1.0, jnp.cos(ang)], axis=0),
            jnp.concatenate([ident, jnp.sin(ang)], axis=0))


def kernel(x_prompt, x_sample, state_ret, state_gla, state_s5_re, state_s5_im, c, c_ctx, w_mod, b_mod, ln_g, ln_b, ret_w_in, ret_decay, ret_w_out, gla_w_in, gla_w_a1, gla_w_a2, gla_b_a, gla_w_out, s5_a_re, s5_a_im, s5_log_step, s5_b_re, s5_b_im, s5_c_re, s5_c_im, s5_d, s5_w_glu, moe_w_router, moe_b_router, moe_w_gu, moe_b_gu, moe_w_down, moe_b_down):
    n_p, l_p, d = x_prompt.shape
    n_s, l_s, _ = x_sample.shape
    depth = w_mod.shape[0]
    n_exp = moe_w_router.shape[-1]
    alpha = (2 * depth) ** 0.25
    rb = ROW_BLOCK
    n = n_p * l_p + n_s * l_s
    nb = n // rb
    nseq = n_p + n_s
    meta_np, cond_np = _block_meta(n_p, l_p, n_s, l_s)
    meta = jnp.asarray(meta_np.reshape(-1))

    x = jnp.concatenate([x_prompt.reshape(n_p * l_p, d), x_sample.reshape(n_s * l_s, d)], axis=0)

    ncond = 16
    conds = jnp.zeros((ncond, d), F32).at[0].set(c_ctx).at[1:1 + n_s].set(c)
    mod = _mod_all(conds, w_mod, b_mod)
    mod_b = mod[:, cond_np].reshape(depth, nb, 6, 1, d)

    cos_t, sin_t = _rotary_tables(l_s, d // RET_HEADS)
    lg_ret = jax.nn.log_sigmoid(ret_decay.astype(F32))

    wr_pad = jnp.zeros((depth, d, ROUTER_PAD), BF16).at[:, :, :n_exp].set(moe_w_router.astype(BF16))
    br_pad = jnp.full((depth, 1, ROUTER_PAD), -1e30, F32).at[:, 0, :n_exp].set(moe_b_router)

    new_ret = jnp.zeros((n_p,) + state_ret.shape[1:], F32)
    new_gla, new_s5_re, new_s5_im = [], [], []
    ret_i = gla_i = s5_i = 0
    for layer in range(depth):
        sh1, sc1, g1, sh2, sc2, g2 = [mod_b[layer, :, t] for t in range(6)]
        lng1, lnb1 = ln_g[layer, 0][None], ln_b[layer, 0][None]
        lng2, lnb2 = ln_g[layer, 1][None], ln_b[layer, 1][None]
        wr, br = wr_pad[layer], br_pad[layer]
        kind = layer % 3
        if kind == 0:
            qkvg = _proj(x, sc1, sh1, ret_w_in[ret_i].astype(BF16), tm=1024, tn=1536)
            a, new_ret = _retention(qkvg, meta, lg_ret[ret_i].reshape(-1), cos_t, sin_t, state_ret, new_ret,
                                    ret_i, nb)
            x, h2, idx_t, gate_t, rank_t, cnt = _mixer_out(a, ret_w_out[ret_i].astype(BF16), x, g1, lng1, lnb1, sc2, sh2,
                                              wr, br, alpha)
            ret_i += 1
        elif kind == 1:
            qkvr = _proj(x, sc1, sh1, gla_w_in[gla_i].astype(BF16), tm=1024, tn=1536)
            rank = gla_w_a1.shape[-1]
            key = gla_w_a2.shape[-1]
            w1 = jnp.concatenate([gla_w_a1[gla_i, 0], gla_w_a1[gla_i, 1]], axis=-1).astype(BF16)
            w2 = jnp.zeros((2 * rank, 2 * key), F32)
            w2 = w2.at[:rank, :key].set(gla_w_a2[gla_i, 0]).at[rank:, key:].set(gla_w_a2[gla_i, 1]).astype(BF16)
            la = _gla_gate(x, sc1, sh1, w1, w2, gla_b_a[gla_i].reshape(1, 2 * key))
            a, st = _gla(qkvr, la, meta, state_gla, gla_i, nb, n_p)
            new_gla.append(st)
            x, h2, idx_t, gate_t, rank_t, cnt = _mixer_out(a, gla_w_out[gla_i].astype(BF16), x, g1, lng1, lnb1, sc2, sh2,
                                              wr, br, alpha)
            gla_i += 1
        else:
            t = S5_CHUNK
            g = d // S5_GROUP
            p = s5_a_re.shape[-1]
            r = n // t
            win8, m8, c8, a1, a2 = _s5_tables(
                s5_a_re[s5_i], s5_a_im[s5_i], s5_log_step[s5_i], s5_b_re[s5_i], s5_b_im[s5_i],
                s5_c_re[s5_i], s5_c_im[s5_i])
            x3 = x.reshape(r, t, d)
            sc3 = jnp.repeat(sc1, rb // t, axis=0)
            sh3 = jnp.repeat(sh1, rb // t, axis=0)
            wf, wb = _s5_in(x3, sc3, sh3, win8, tr=_row_tile(r, 640))
            s0 = jnp.concatenate([state_s5_re[:, s5_i], state_s5_im[:, s5_i]], axis=-1)
            s0 = jnp.concatenate([jnp.zeros((n_p,) + s0.shape[1:], F32), s0], axis=0)
            xf, stf = _s5_scan(wf, a1[0], a2[0], s0[:, 0], meta, nb, nseq, reverse=False)
            xb, stb = _s5_scan(wb, a1[1], a2[1], s0[:, 1], meta, nb, nseq, reverse=True)
            y3 = _s5_out(x3, sc3, sh3, m8, xf, xb, c8, tr=_row_tile(r, 256))
            y = y3.reshape(n, d)
            st = jnp.stack([stf[:n_p], stb[:n_p]], axis=1)
            new_s5_re.append(st[..., :p])
            new_s5_im.append(st[..., p:])
            wg = s5_w_glu[s5_i].astype(BF16)
            x, h2, idx_t, gate_t, rank_t, cnt = _glu_out(y, wg[:, :d], wg[:, d:], s5_d[s5_i][None], sc1, sh1, x, g1,
                                            lng1, lnb1, sc2, sh2, wr, br, alpha)
            s5_i += 1
        x = _moe(x, h2, idx_t, gate_t, rank_t, cnt, g2, lng2, lnb2, moe_w_gu, moe_b_gu[:, :, None, :],
                 moe_w_down, moe_b_down[:, :, None, :], layer, alpha)

    y_prompt = x[:n_p * l_p].reshape(n_p, l_p, d)
    y_sample = x[n_p * l_p:].reshape(n_s, l_s, d)
    return (y_prompt, y_sample, new_ret, jnp.stack(new_gla, axis=1),
            jnp.stack(new_s5_re, axis=1), jnp.stack(new_s5_im, axis=1))
```

```python
import functools
import math

import numpy as np
import jax
import jax.numpy as jnp
from jax import lax
from jax.experimental import pallas as pl
from jax.experimental.pallas import tpu as pltpu

F32 = jnp.float32
BF16 = jnp.bfloat16

RET_HEADS = 4
GLA_HEADS = 4
GLA_TAU = 16.0
GRID_W = 64
ROPE_BASE = 10000.0
S5_GROUP = 16
TOP_K = 4
SWIGLU_ALPHA = 1.702
SWIGLU_LIMIT = 7.0
LN_EPS = 1e-5
GN_EPS = 1e-5

ROW_BLOCK = 256
GLA_CHUNK = 64
S5_CHUNK = 16
S5_LANE_GROUPS = 8
MOE_BLOCK = 512
ROUTER_PAD = 128
VMEM_LIMIT = 56 * 1024 * 1024

M_START, M_END, M_SAMPLE, M_SIDX, M_PIDX, M_POS, M_SEQ = range(7)
M_ROWS = 7

NT = (((1,), (1,)), ((), ()))
TN = (((0,), (0,)), ((), ()))


def _cparams(sem):
    return pltpu.CompilerParams(dimension_semantics=sem, vmem_limit_bytes=VMEM_LIMIT)


def _silu(x):
    return x * jax.nn.sigmoid(x)


def _dot(a, b):
    return jnp.dot(a, b, preferred_element_type=F32)


def _mod_kernel(c_ref, w_ref, b_ref, o_ref):
    c = c_ref[...]
    s = _silu(c).astype(BF16)
    o_ref[0] = _dot(s, w_ref[0].astype(BF16)) + b_ref[0]


def _mod_all(conds, w_mod, b_mod):
    depth, d, n6 = w_mod.shape
    nc = conds.shape[0]
    tn = 1536
    return pl.pallas_call(
        _mod_kernel,
        out_shape=jax.ShapeDtypeStruct((depth, nc, n6), F32),
        grid=(depth, n6 // tn),
        in_specs=[pl.BlockSpec((nc, d), lambda l, j: (0, 0)),
                  pl.BlockSpec((1, d, tn), lambda l, j: (l, 0, j)),
                  pl.BlockSpec((1, 1, tn), lambda l, j: (l, 0, j))],
        out_specs=pl.BlockSpec((1, nc, tn), lambda l, j: (l, 0, j)),
        compiler_params=_cparams(("parallel", "parallel")),
    )(conds, w_mod, b_mod.reshape(depth, 1, n6))


def _proj_kernel(x_ref, sc_ref, sh_ref, w_ref, o_ref, h_scr):
    @pl.when(pl.program_id(1) == 0)
    def _():
        h_scr[...] = (x_ref[...] * (1.0 + sc_ref[...]) + sh_ref[...]).astype(BF16)

    o_ref[...] = _dot(h_scr[...], w_ref[...]).astype(o_ref.dtype)


def _proj(x, sc, sh, w, tm, tn, out_dtype=BF16):
    n, d = x.shape
    nout = w.shape[1]
    kb = tm // ROW_BLOCK
    return pl.pallas_call(
        _proj_kernel,
        out_shape=jax.ShapeDtypeStruct((n, nout), out_dtype),
        grid=(n // tm, nout // tn),
        in_specs=[pl.BlockSpec((tm, d), lambda i, j: (i, 0)),
                  pl.BlockSpec((None, 1, d), lambda i, j: (i * kb, 0, 0)),
                  pl.BlockSpec((None, 1, d), lambda i, j: (i * kb, 0, 0)),
                  pl.BlockSpec((d, tn), lambda i, j: (0, j))],
        out_specs=pl.BlockSpec((tm, tn), lambda i, j: (i, j)),
        scratch_shapes=[pltpu.VMEM((tm, d), BF16)],
        compiler_params=_cparams(("parallel", "arbitrary")),
        name="proj",
    )(x, sc, sh, w)


def _layer_norm(z, g, b):
    mu = jnp.mean(z, axis=-1, keepdims=True)
    zc = z - mu
    var = jnp.mean(zc * zc, axis=-1, keepdims=True)
    return zc * lax.rsqrt(var + LN_EPS) * g + b


def _route(h2, wr_ref, br_ref, cnt_scr, idx_ref, gate_ref, rank_ref, cnt_ref):
    logits = _dot(h2.astype(BF16), wr_ref[...]) + br_ref[...]
    tm, width = logits.shape
    col = lax.broadcasted_iota(jnp.int32, (tm, width), 1)
    colf = col.astype(F32)
    l = logits
    vals, idxs = [], []
    for _ in range(TOP_K):
        m = jnp.max(l, axis=-1, keepdims=True)
        ix = jnp.min(jnp.where(l == m, colf, float(width)), axis=-1, keepdims=True)
        vals.append(m)
        idxs.append(ix)
        l = jnp.where(colf == ix, -jnp.inf, l)
    es = [jnp.exp(v - vals[0]) for v in vals]
    den = es[0] + es[1] + es[2] + es[3]
    onehot = jnp.where(l == -jnp.inf, 1.0, 0.0)

    @pl.when(pl.program_id(0) == 0)
    def _():
        cnt_scr[...] = jnp.zeros_like(cnt_scr)

    rn = lax.broadcasted_iota(jnp.int32, (tm, tm), 0)
    rm = lax.broadcasted_iota(jnp.int32, (tm, tm), 1)
    earlier = jnp.where(rn > rm, 1.0, 0.0).astype(BF16)
    cum = _dot(earlier, onehot.astype(BF16)) + cnt_scr[...]
    idx_t = jnp.zeros((tm, width), jnp.int32)
    rank_t = jnp.zeros((tm, width), jnp.int32)
    gate_t = jnp.zeros((tm, width), F32)
    for k in range(TOP_K):
        rank_k = jnp.sum(jnp.where(colf == idxs[k], cum, 0.0), axis=-1, keepdims=True)
        idx_t = jnp.where(col == k, idxs[k].astype(jnp.int32), idx_t)
        rank_t = jnp.where(col == k, rank_k.astype(jnp.int32), rank_t)
        gate_t = jnp.where(col == k, es[k] / den, gate_t)
    idx_ref[...] = idx_t
    gate_ref[...] = gate_t
    rank_ref[...] = rank_t
    cnt_new = cnt_scr[...] + jnp.sum(onehot, axis=0, keepdims=True)
    cnt_scr[...] = cnt_new
    cnt_ref[...] = cnt_new


def _post_sublayer(alpha, x, y, g, lng, lnb):
    return _layer_norm(alpha * x + g * y, lng, lnb)


def _mixer_out_kernel(a_ref, w_ref, x_ref, g_ref, lng_ref, lnb_ref, sc2_ref, sh2_ref,
                      wr_ref, br_ref, xo_ref, h2_ref, idx_ref, gate_ref, rank_ref, cnt_ref, cnt_scr, *, alpha):
    y = _dot(a_ref[...], w_ref[...])
    xn = _post_sublayer(alpha, x_ref[...], y, g_ref[...], lng_ref[...], lnb_ref[...])
    xo_ref[...] = xn
    h2 = xn * (1.0 + sc2_ref[...]) + sh2_ref[...]
    h2_ref[...] = h2
    _route(h2, wr_ref, br_ref, cnt_scr, idx_ref, gate_ref, rank_ref, cnt_ref)


def _glu_out_kernel(y_ref, w1_ref, w2_ref, d_ref, sc1_ref, sh1_ref, x_ref, g_ref, lng_ref, lnb_ref,
                    sc2_ref, sh2_ref, wr_ref, br_ref, xo_ref, h2_ref, idx_ref, gate_ref, rank_ref, cnt_ref,
                    cnt_scr, *, alpha):
    x = x_ref[...]
    h = x * (1.0 + sc1_ref[...]) + sh1_ref[...]
    a = jax.nn.gelu(y_ref[...] + d_ref[...] * h).astype(BF16)
    y = _dot(a, w1_ref[...]) * jax.nn.sigmoid(_dot(a, w2_ref[...]))
    xn = _post_sublayer(alpha, x, y, g_ref[...], lng_ref[...], lnb_ref[...])
    xo_ref[...] = xn
    h2 = xn * (1.0 + sc2_ref[...]) + sh2_ref[...]
    h2_ref[...] = h2
    _route(h2, wr_ref, br_ref, cnt_scr, idx_ref, gate_ref, rank_ref, cnt_ref)


def _row_spec(tm, width):
    return pl.BlockSpec((tm, width), lambda i: (i, 0))


def _blk_vec_spec(tm, d):
    kb = tm // ROW_BLOCK
    return pl.BlockSpec((None, 1, d), lambda i: (i * kb, 0, 0))


def _const_spec(shape):
    return pl.BlockSpec(shape, lambda i: tuple(0 for _ in shape))


def _post_out_shapes(n, d):
    return (jax.ShapeDtypeStruct((n, d), F32), jax.ShapeDtypeStruct((n, d), F32),
            jax.ShapeDtypeStruct((n, ROUTER_PAD), jnp.int32), jax.ShapeDtypeStruct((n, ROUTER_PAD), F32),
            jax.ShapeDtypeStruct((n, ROUTER_PAD), jnp.int32), jax.ShapeDtypeStruct((1, ROUTER_PAD), F32))


def _post_out_specs(tm, d):
    return (_row_spec(tm, d), _row_spec(tm, d), _row_spec(tm, ROUTER_PAD), _row_spec(tm, ROUTER_PAD),
            _row_spec(tm, ROUTER_PAD), _const_spec((1, ROUTER_PAD)))


def _mixer_out(a, w, x, g1, lng, lnb, sc2, sh2, wr, br, alpha, tm=512):
    n, d = x.shape
    kin = a.shape[1]
    return pl.pallas_call(
        functools.partial(_mixer_out_kernel, alpha=alpha),
        out_shape=_post_out_shapes(n, d),
        grid=(n // tm,),
        in_specs=[_row_spec(tm, kin), _const_spec((kin, d)), _row_spec(tm, d), _blk_vec_spec(tm, d),
                  _const_spec((1, d)), _const_spec((1, d)), _blk_vec_spec(tm, d), _blk_vec_spec(tm, d),
                  _const_spec((d, ROUTER_PAD)), _const_spec((1, ROUTER_PAD))],
        out_specs=_post_out_specs(tm, d),
        scratch_shapes=[pltpu.VMEM((1, ROUTER_PAD), F32)],
        compiler_params=_cparams(("arbitrary",)),
        name="mixer_out_route",
    )(a, w, x, g1, lng, lnb, sc2, sh2, wr, br)


def _glu_out(y, w1, w2, dskip, sc1, sh1, x, g1, lng, lnb, sc2, sh2, wr, br, alpha, tm=512):
    n, d = x.shape
    return pl.pallas_call(
        functools.partial(_glu_out_kernel, alpha=alpha),
        out_shape=_post_out_shapes(n, d),
        grid=(n // tm,),
        in_specs=[_row_spec(tm, d), _const_spec((d, d)), _const_spec((d, d)), _const_spec((1, d)),
                  _blk_vec_spec(tm, d), _blk_vec_spec(tm, d), _row_spec(tm, d), _blk_vec_spec(tm, d),
                  _const_spec((1, d)), _const_spec((1, d)), _blk_vec_spec(tm, d), _blk_vec_spec(tm, d),
                  _const_spec((d, ROUTER_PAD)), _const_spec((1, ROUTER_PAD))],
        out_specs=_post_out_specs(tm, d),
        scratch_shapes=[pltpu.VMEM((1, ROUTER_PAD), F32)],
        compiler_params=_cparams(("arbitrary",)),
        name="mixer_out_route",
    )(y, w1, w2, dskip, sc1, sh1, x, g1, lng, lnb, sc2, sh2, wr, br)


def _rotary(x, cos, sin):
    half = x.shape[-1] // 2
    x1, x2 = x[:, :half], x[:, half:]
    return jnp.concatenate([x1 * cos - x2 * sin, x1 * sin + x2 * cos], axis=-1)


def _head_norm(o):
    mu = jnp.mean(o, axis=-1, keepdims=True)
    oc = o - mu
    var = jnp.mean(oc * oc, axis=-1, keepdims=True)
    return oc * lax.rsqrt(var + GN_EPS)


def _ret_fwd_kernel(meta_ref, lg_ref, q_ref, k_ref, v_ref, cos_ref, sin_ref, s0_ref,
                    st_in_ref, o_ref, st_ref, s_scr, *, nb, heads):
    del st_in_ref
    i = pl.program_id(0)
    start = meta_ref[M_START * nb + i]
    end = meta_ref[M_END * nb + i]
    sample = meta_ref[M_SAMPLE * nb + i]
    c = q_ref.shape[0]
    dk = q_ref.shape[1] // heads
    dv = v_ref.shape[1] // heads

    @pl.when(jnp.logical_and(start == 1, sample == 1))
    def _():
        s_scr[...] = s0_ref[...]

    @pl.when(jnp.logical_and(start == 1, sample == 0))
    def _():
        s_scr[...] = jnp.zeros_like(s_scr)

    cos = cos_ref[...]
    sin = sin_ref[...]
    rn = lax.broadcasted_iota(jnp.int32, (c, c), 0)
    rm = lax.broadcasted_iota(jnp.int32, (c, c), 1)
    rel = (rn - rm).astype(F32)
    idx = lax.broadcasted_iota(jnp.int32, (c, 1), 0).astype(F32)
    for h in range(heads):
        lgf = lg_ref[h]
        lgb = lg_ref[heads + h]
        q = _rotary(q_ref[:, h * dk:(h + 1) * dk].astype(F32), cos, sin) * (dk ** -0.5)
        k = _rotary(k_ref[:, h * dk:(h + 1) * dk].astype(F32), cos, sin)
        qb = q.astype(BF16)
        v = v_ref[:, h * dv:(h + 1) * dv]
        dmat = (jnp.where(rel >= 0, jnp.exp(lgf * jnp.maximum(rel, 0.0)), 0.0)
                + jnp.where(rel <= 0, jnp.exp(lgb * jnp.maximum(-rel, 0.0)), 0.0))
        scores = lax.dot_general(qb, k.astype(BF16), NT, preferred_element_type=F32) * dmat
        xi = jnp.exp(lgf * (idx + 1.0))
        zeta = jnp.exp(lgf * (c - 1.0 - idx))
        s_prev = s_scr[h]
        o_ref[:, h * dv:(h + 1) * dv] = (
            _dot(scores.astype(BF16), v) + _dot(qb, s_prev.astype(BF16)) * xi).astype(o_ref.dtype)
        g_chunk = jnp.exp(lgf * jnp.full((1, 1), float(c), F32))
        s_scr[h] = s_prev * g_chunk + lax.dot_general((k * zeta).astype(BF16), v, TN,
                                                       preferred_element_type=F32)

    @pl.when(jnp.logical_and(end == 1, sample == 0))
    def _():
        st_ref[...] = s_scr[...]


def _ret_bwd_kernel(meta_ref, lg_ref, q_ref, k_ref, v_ref, g_ref, cos_ref, sin_ref, s0_ref, of_ref,
                    st_in_ref, o_ref, st_ref, s_scr, *, nb, heads):
    del st_in_ref
    i = nb - 1 - pl.program_id(0)
    start = meta_ref[M_START * nb + i]
    end = meta_ref[M_END * nb + i]
    sample = meta_ref[M_SAMPLE * nb + i]
    c = q_ref.shape[0]
    dk = q_ref.shape[1] // heads
    dv = v_ref.shape[1] // heads

    @pl.when(jnp.logical_and(end == 1, sample == 1))
    def _():
        s_scr[...] = s0_ref[...]

    @pl.when(jnp.logical_and(end == 1, sample == 0))
    def _():
        s_scr[...] = jnp.zeros_like(s_scr)

    cos = cos_ref[...]
    sin = sin_ref[...]
    idx = lax.broadcasted_iota(jnp.int32, (c, 1), 0).astype(F32)
    for h in range(heads):
        lgb = lg_ref[heads + h]
        q = _rotary(q_ref[:, h * dk:(h + 1) * dk].astype(F32), cos, sin) * (dk ** -0.5)
        k = _rotary(k_ref[:, h * dk:(h + 1) * dk].astype(F32), cos, sin)
        v = v_ref[:, h * dv:(h + 1) * dv]
        xi = jnp.exp(lgb * (c - idx))
        zeta = jnp.exp(lgb * idx)
        s_prev = s_scr[h]
        o = of_ref[:, h * dv:(h + 1) * dv].astype(F32) + _dot(q.astype(BF16), s_prev.astype(BF16)) * xi
        g_chunk = jnp.exp(lgb * jnp.full((1, 1), float(c), F32))
        s_scr[h] = s_prev * g_chunk + lax.dot_general((k * zeta).astype(BF16), v, TN,
                                                       preferred_element_type=F32)
        gate = _silu(g_ref[:, h * dv:(h + 1) * dv].astype(F32))
        o_ref[:, h * dv:(h + 1) * dv] = (gate * _head_norm(o)).astype(o_ref.dtype)

    @pl.when(jnp.logical_and(start == 1, sample == 0))
    def _():
        st_ref[...] = s_scr[...]


def _retention(qkvg, meta, lg, cos_t, sin_t, state, new_state, li, nb):
    n = qkvg.shape[0]
    heads = RET_HEADS
    d = qkvg.shape[1] // 6
    dk, dv = d // heads, 2 * d // heads
    c = ROW_BLOCK
    kw = dict(nb=nb, heads=heads)
    r = lambda j: nb - 1 - j
    pos = lambda j, m: m[M_POS * nb + j]
    st_blk = (None, None, None, heads, dk, dv)
    any_spec = pl.BlockSpec(memory_space=pl.ANY)
    of, new_state = pl.pallas_call(
        functools.partial(_ret_fwd_kernel, **kw),
        out_shape=(jax.ShapeDtypeStruct((n, heads * dv), BF16),
                   jax.ShapeDtypeStruct(new_state.shape, new_state.dtype)),
        grid_spec=pltpu.PrefetchScalarGridSpec(
            num_scalar_prefetch=1, grid=(nb,),
            in_specs=[pl.BlockSpec(memory_space=pltpu.SMEM),
                      pl.BlockSpec((c, d), lambda j, m: (j, 0)),
                      pl.BlockSpec((c, d), lambda j, m: (j, 1)),
                      pl.BlockSpec((c, 2 * d), lambda j, m: (j, 1)),
                      pl.BlockSpec((c, dk // 2), lambda j, m: (pos(j, m), 0)),
                      pl.BlockSpec((c, dk // 2), lambda j, m: (pos(j, m), 0)),
                      pl.BlockSpec(st_blk, lambda j, m: (m[M_SIDX * nb + j], li, 0, 0, 0, 0)),
                      any_spec],
            out_specs=(pl.BlockSpec((c, 2 * d), lambda j, m: (j, 0)),
                       pl.BlockSpec(st_blk, lambda j, m: (m[M_PIDX * nb + j], li, 0, 0, 0, 0))),
            scratch_shapes=[pltpu.VMEM((heads, dk, dv), F32)]),
        input_output_aliases={8: 1},
        compiler_params=_cparams(("arbitrary",)),
        name="ret_fwd",
    )(meta, lg, qkvg, qkvg, qkvg, cos_t, sin_t, state, new_state)
    out, new_state = pl.pallas_call(
        functools.partial(_ret_bwd_kernel, **kw),
        out_shape=(jax.ShapeDtypeStruct((n, heads * dv), BF16),
                   jax.ShapeDtypeStruct(new_state.shape, new_state.dtype)),
        grid_spec=pltpu.PrefetchScalarGridSpec(
            num_scalar_prefetch=1, grid=(nb,),
            in_specs=[pl.BlockSpec(memory_space=pltpu.SMEM),
                      pl.BlockSpec((c, d), lambda j, m: (r(j), 0)),
                      pl.BlockSpec((c, d), lambda j, m: (r(j), 1)),
                      pl.BlockSpec((c, 2 * d), lambda j, m: (r(j), 1)),
                      pl.BlockSpec((c, 2 * d), lambda j, m: (r(j), 2)),
                      pl.BlockSpec((c, dk // 2), lambda j, m: (pos(r(j), m), 0)),
                      pl.BlockSpec((c, dk // 2), lambda j, m: (pos(r(j), m), 0)),
                      pl.BlockSpec(st_blk, lambda j, m: (m[M_SIDX * nb + r(j)], li, 1, 0, 0, 0)),
                      pl.BlockSpec((c, 2 * d), lambda j, m: (r(j), 0)),
                      any_spec],
            out_specs=(pl.BlockSpec((c, 2 * d), lambda j, m: (r(j), 0)),
                       pl.BlockSpec(st_blk, lambda j, m: (m[M_PIDX * nb + r(j)], li, 1, 0, 0, 0))),
            scratch_shapes=[pltpu.VMEM((heads, dk, dv), F32)]),
        input_output_aliases={10: 1},
        compiler_params=_cparams(("arbitrary",)),
        name="ret_bwd",
    )(meta, lg, qkvg, qkvg, qkvg, qkvg, cos_t, sin_t, state, of, new_state)
    return out, new_state


def _gla_gate_kernel(x_ref, sc_ref, sh_ref, w1_ref, w2_ref, b_ref, o_ref):
    h = (x_ref[...] * (1.0 + sc_ref[...]) + sh_ref[...]).astype(BF16)
    u = _dot(h, w1_ref[...]).astype(BF16)
    z = _dot(u, w2_ref[...]) + b_ref[...]
    log_sig = jnp.minimum(z, 0.0) - jnp.log(1.0 + jnp.exp(-jnp.abs(z)))
    o_ref[...] = log_sig / GLA_TAU


def _gla_gate(x, sc, sh, w1, w2, b, tm=512):
    n, d = x.shape
    r2 = w1.shape[1]
    nk = w2.shape[1]
    return pl.pallas_call(
        _gla_gate_kernel,
        out_shape=jax.ShapeDtypeStruct((n, nk), F32),
        grid=(n // tm,),
        in_specs=[_row_spec(tm, d), _blk_vec_spec(tm, d), _blk_vec_spec(tm, d), _const_spec((d, r2)),
                  _const_spec((r2, nk)), _const_spec((1, nk))],
        out_specs=_row_spec(tm, nk),
        compiler_params=_cparams(("parallel",)),
    )(x, sc, sh, w1, w2, b)


def _cumsum_rows(tri, x):
    hi = x.astype(BF16)
    r1 = x - hi.astype(F32)
    mid = r1.astype(BF16)
    lo = (r1 - mid.astype(F32)).astype(BF16)
    return _dot(tri, hi) + _dot(tri, mid) + _dot(tri, lo)


def _gla_chunk(q, k, v, la, s_t, tri, mask, mid_row, last_row):
    b = _cumsum_rows(tri, la)
    r = b[mid_row:mid_row + 1, :]
    bl = b[last_row:last_row + 1, :]
    qe = (q * jnp.exp(b - r)).astype(BF16)
    ke = (k * jnp.exp(r - b)).astype(BF16)
    qi = (q * jnp.exp(b)).astype(BF16)
    kd = (k * jnp.exp(bl - b)).astype(BF16)
    sc = lax.dot_general(qe, ke, NT, preferred_element_type=F32)
    sc = jnp.where(mask, sc, 0.0).astype(BF16)
    o = _dot(sc, v) + lax.dot_general(qi, s_t.astype(BF16), NT, preferred_element_type=F32)
    s_new = s_t * jnp.exp(bl) + lax.dot_general(v, kd, TN, preferred_element_type=F32)
    return o, s_new


def _gla_fwd_kernel(meta_ref, q_ref, k_ref, v_ref, la_ref, s0_ref, o_ref, st_ref, s_scr, *, nb, heads):
    i = pl.program_id(0)
    start = meta_ref[M_START * nb + i]
    end = meta_ref[M_END * nb + i]
    sample = meta_ref[M_SAMPLE * nb + i]
    dk = q_ref.shape[1] // heads
    dv = v_ref.shape[1] // heads
    cc = GLA_CHUNK

    @pl.when(jnp.logical_and(start == 1, sample == 1))
    def _():
        for h in range(heads):
            s_scr[h] = s0_ref[h].T

    @pl.when(jnp.logical_and(start == 1, sample == 0))
    def _():
        s_scr[...] = jnp.zeros_like(s_scr)

    rn = lax.broadcasted_iota(jnp.int32, (cc, cc), 0)
    rm = lax.broadcasted_iota(jnp.int32, (cc, cc), 1)
    mask = rn >= rm
    tri = jnp.where(mask, 1.0, 0.0).astype(BF16)
    for h in range(heads):
        ks = slice(h * dk, (h + 1) * dk)
        vs = slice(h * dv, (h + 1) * dv)
        s_t = s_scr[h]
        for ci in range(q_ref.shape[0] // cc):
            sl = slice(ci * cc, (ci + 1) * cc)
            q = q_ref[sl, ks].astype(F32) * (dk ** -0.5)
            o, s_t = _gla_chunk(q, k_ref[sl, ks].astype(F32), v_ref[sl, vs], la_ref[sl, ks], s_t, tri, mask,
                                cc // 2 - 1, cc - 1)
            o_ref[sl, vs] = o.astype(o_ref.dtype)
        s_scr[h] = s_t

    @pl.when(jnp.logical_and(end == 1, sample == 0))
    def _():
        for h in range(heads):
            st_ref[h] = s_scr[h].T


def _gla_bwd_kernel(meta_ref, q_ref, k_ref, v_ref, r_ref, la_ref, s0_ref, of_ref, o_ref, st_ref, s_scr,
                    *, nb, heads):
    i = nb - 1 - pl.program_id(0)
    start = meta_ref[M_START * nb + i]
    end = meta_ref[M_END * nb + i]
    sample = meta_ref[M_SAMPLE * nb + i]
    dk = q_ref.shape[1] // heads
    dv = v_ref.shape[1] // heads
    cc = GLA_CHUNK

    @pl.when(jnp.logical_and(end == 1, sample == 1))
    def _():
        for h in range(heads):
            s_scr[h] = s0_ref[h].T

    @pl.when(jnp.logical_and(end == 1, sample == 0))
    def _():
        s_scr[...] = jnp.zeros_like(s_scr)

    rn = lax.broadcasted_iota(jnp.int32, (cc, cc), 0)
    rm = lax.broadcasted_iota(jnp.int32, (cc, cc), 1)
    mask = rn <= rm
    tri = jnp.where(mask, 1.0, 0.0).astype(BF16)
    for h in range(heads):
        ks = slice(h * dk, (h + 1) * dk)
        vs = slice(h * dv, (h + 1) * dv)
        s_t = s_scr[h]
        for ci in reversed(range(q_ref.shape[0] // cc)):
            sl = slice(ci * cc, (ci + 1) * cc)
            q = q_ref[sl, ks].astype(F32) * (dk ** -0.5)
            ob, s_t = _gla_chunk(q, k_ref[sl, ks].astype(F32), v_ref[sl, vs], la_ref[sl, ks], s_t, tri, mask,
                                 cc // 2, 0)
            o = of_ref[sl, vs].astype(F32) + ob
            o_ref[sl, vs] = (_silu(r_ref[sl, vs].astype(F32)) * _head_norm(o)).astype(o_ref.dtype)
        s_scr[h] = s_t

    @pl.when(jnp.logical_and(start == 1, sample == 0))
    def _():
        for h in range(heads):
            st_ref[h] = s_scr[h].T


def _gla(qkvr, la, meta, state, li, nb, n_prompt):
    n = qkvr.shape[0]
    heads = GLA_HEADS
    d = qkvr.shape[1] // 3
    dk, dv = d // 2 // heads, d // heads
    c = ROW_BLOCK
    kw = dict(nb=nb, heads=heads)
    r = lambda j: nb - 1 - j
    st_blk = (None, None, None, heads, dk, dv)
    of, st_f = pl.pallas_call(
        functools.partial(_gla_fwd_kernel, **kw),
        out_shape=(jax.ShapeDtypeStruct((n, d), BF16), jax.ShapeDtypeStruct((n_prompt, heads, dk, dv), F32)),
        grid_spec=pltpu.PrefetchScalarGridSpec(
            num_scalar_prefetch=1, grid=(nb,),
            in_specs=[pl.BlockSpec((c, d // 2), lambda j, m: (j, 0)),
                      pl.BlockSpec((c, d // 2), lambda j, m: (j, 1)),
                      pl.BlockSpec((c, d), lambda j, m: (j, 1)),
                      pl.BlockSpec((c, d // 2), lambda j, m: (j, 0)),
                      pl.BlockSpec(st_blk, lambda j, m: (m[M_SIDX * nb + j], li, 0, 0, 0, 0))],
            out_specs=(pl.BlockSpec((c, d), lambda j, m: (j, 0)),
                       pl.BlockSpec((None, heads, dk, dv), lambda j, m: (m[M_PIDX * nb + j], 0, 0, 0))),
            scratch_shapes=[pltpu.VMEM((heads, dv, dk), F32)]),
        compiler_params=_cparams(("arbitrary",)),
        name="gla_fwd",
    )(meta, qkvr, qkvr, qkvr, la, state)
    out, st_b = pl.pallas_call(
        functools.partial(_gla_bwd_kernel, **kw),
        out_shape=(jax.ShapeDtypeStruct((n, d), BF16), jax.ShapeDtypeStruct((n_prompt, heads, dk, dv), F32)),
        grid_spec=pltpu.PrefetchScalarGridSpec(
            num_scalar_prefetch=1, grid=(nb,),
            in_specs=[pl.BlockSpec((c, d // 2), lambda j, m: (r(j), 0)),
                      pl.BlockSpec((c, d // 2), lambda j, m: (r(j), 1)),
                      pl.BlockSpec((c, d), lambda j, m: (r(j), 1)),
                      pl.BlockSpec((c, d), lambda j, m: (r(j), 2)),
                      pl.BlockSpec((c, d // 2), lambda j, m: (r(j), 1)),
                      pl.BlockSpec(st_blk, lambda j, m: (m[M_SIDX * nb + r(j)], li, 1, 0, 0, 0)),
                      pl.BlockSpec((c, d), lambda j, m: (r(j), 0))],
            out_specs=(pl.BlockSpec((c, d), lambda j, m: (r(j), 0)),
                       pl.BlockSpec((None, heads, dk, dv), lambda j, m: (m[M_PIDX * nb + r(j)], 0, 0, 0))),
            scratch_shapes=[pltpu.VMEM((heads, dv, dk), F32)]),
        compiler_params=_cparams(("arbitrary",)),
        name="gla_bwd",
    )(meta, qkvr, qkvr, qkvr, qkvr, la, state, of)
    return out, jnp.stack([st_f, st_b], axis=1)


def _row_tile(rows, cap):
    tile = min(cap, rows) // 8 * 8
    while rows % tile:
        tile -= 8
    return tile


def _s5_chunk_rows(x_ref, sc_ref, sh_ref, lhs):
    rows, lanes = sc_ref.shape
    t_len = x_ref.shape[0] // rows
    for t in range(t_len):
        xs = x_ref[pl.ds(t, rows, stride=t_len), :]
        lhs[:, t * lanes:(t + 1) * lanes] = (xs * (1.0 + sc_ref[...]) + sh_ref[...]).astype(BF16)


def _s5_in_kernel(x_ref, sc_ref, sh_ref, w_ref, wf_ref, wb_ref, lhs):
    _s5_chunk_rows(x_ref, sc_ref, sh_ref, lhs)
    z = _dot(lhs[...], w_ref[...])
    rows = sc_ref.shape[0]
    half = z.shape[1] // 2
    p2 = wf_ref.shape[1]
    lg = wf_ref.shape[0] // rows
    for h in range(lg):
        wf_ref[pl.ds(h, rows, stride=lg), :] = z[:, h * p2:(h + 1) * p2]
        wb_ref[pl.ds(h, rows, stride=lg), :] = z[:, half + h * p2:half + (h + 1) * p2]


def _s5_in(x, sc, sh, win8, tr):
    n, d = x.shape
    nj, kc, ncol = win8.shape
    r = sc.shape[0]
    t = n // r
    lanes = kc // t
    lg = S5_LANE_GROUPS
    p2 = ncol // 2 // lg
    x_spec = pl.BlockSpec((tr * t, lanes), lambda ji, ri: (ri, ji))
    v_spec = pl.BlockSpec((tr, lanes), lambda ji, ri: (ri, ji))
    st_spec = pl.BlockSpec((None, tr * lg, p2), lambda ji, ri: (ji, ri, 0))
    return pl.pallas_call(
        _s5_in_kernel,
        out_shape=(jax.ShapeDtypeStruct((nj, r * lg, p2), F32), jax.ShapeDtypeStruct((nj, r * lg, p2), F32)),
        grid=(nj, r // tr),
        in_specs=[x_spec, v_spec, v_spec, pl.BlockSpec((None, kc, ncol), lambda ji, ri: (ji, 0, 0))],
        out_specs=(st_spec, st_spec),
        scratch_shapes=[pltpu.VMEM((tr, kc), BF16)],
        compiler_params=_cparams(("parallel", "parallel")),
        name="s5_in",
    )(x, sc, sh, win8)


def _s5_scan_kernel(meta_ref, w_ref, a1_ref, a2_ref, s0_ref, xp_ref, st_ref, x_scr, *, nb, reverse):
    j = pl.program_id(0)
    i = nb - 1 - j if reverse else j
    first = meta_ref[(M_END if reverse else M_START) * nb + i]
    last = meta_ref[(M_START if reverse else M_END) * nb + i]

    @pl.when(first == 1)
    def _():
        x_scr[...] = s0_ref[...]

    a1 = a1_ref[...]
    a2 = a2_ref[...]
    x = x_scr[...]
    half = x.shape[2] // 2
    nrow = w_ref.shape[1]
    order = range(nrow - 1, -1, -1) if reverse else range(nrow)
    for c in order:
        xp_ref[:, c] = x
        x = x * a1 + pltpu.roll(x, half, axis=2) * a2 + w_ref[:, c]
    x_scr[...] = x

    @pl.when(last == 1)
    def _():
        st_ref[...] = x


def _s5_scan(w, a1, a2, s0, meta, nb, nseq, reverse):
    nj, r, lg, p2 = w.shape
    cr = r // nb
    blk = (lambda j: nb - 1 - j) if reverse else (lambda j: j)
    w_spec = pl.BlockSpec((nj, cr, lg, p2), lambda j, m: (0, blk(j), 0, 0))
    a_spec = pl.BlockSpec((nj, lg, p2), lambda j, m: (0, 0, 0))
    s_spec = pl.BlockSpec((None, nj, lg, p2), lambda j, m: (m[M_SEQ * nb + blk(j)], 0, 0, 0))
    return pl.pallas_call(
        functools.partial(_s5_scan_kernel, nb=nb, reverse=reverse),
        out_shape=(jax.ShapeDtypeStruct((nj, r, lg, p2), F32), jax.ShapeDtypeStruct((nseq, nj, lg, p2), F32)),
        grid_spec=pltpu.PrefetchScalarGridSpec(
            num_scalar_prefetch=1, grid=(nb,),
            in_specs=[w_spec, a_spec, a_spec, s_spec],
            out_specs=(w_spec, s_spec),
            scratch_shapes=[pltpu.VMEM((nj, lg, p2), F32)]),
        compiler_params=_cparams(("arbitrary",)),
        name="s5_scan",
    )(meta, w, a1, a2, s0)


def _s5_out_kernel(x_ref, sc_ref, sh_ref, m_ref, xf_ref, xb_ref, c_ref, y_ref, lhs, states):
    _s5_chunk_rows(x_ref, sc_ref, sh_ref, lhs)
    rows, lanes = sc_ref.shape
    t_len = y_ref.shape[0] // rows
    p2 = xf_ref.shape[1]
    lg = xf_ref.shape[0] // rows
    ns = lg * p2
    for h in range(lg):
        states[:, h * p2:(h + 1) * p2] = xf_ref[pl.ds(h, rows, stride=lg), :].astype(BF16)
        states[:, ns + h * p2:ns + (h + 1) * p2] = xb_ref[pl.ds(h, rows, stride=lg), :].astype(BF16)
    z = _dot(lhs[...], m_ref[...]) + _dot(states[...], c_ref[...])
    for t in range(t_len):
        y_ref[pl.ds(t, rows, stride=t_len), :] = z[:, t * lanes:(t + 1) * lanes]


def _s5_out(x, sc, sh, m8, xf, xb, c8, tr):
    n, d = x.shape
    nj, kc, _ = m8.shape
    r = sc.shape[0]
    t = n // r
    lanes = kc // t
    ns = c8.shape[1] // 2
    lg = S5_LANE_GROUPS
    x_spec = pl.BlockSpec((tr * t, lanes), lambda ji, ri: (ri, ji))
    v_spec = pl.BlockSpec((tr, lanes), lambda ji, ri: (ri, ji))
    s_spec = pl.BlockSpec((None, tr * lg, ns // lg), lambda ji, ri: (ji, ri, 0))
    return pl.pallas_call(
        _s5_out_kernel,
        out_shape=jax.ShapeDtypeStruct((n, d), F32),
        grid=(nj, r // tr),
        in_specs=[x_spec, v_spec, v_spec, pl.BlockSpec((None, kc, kc), lambda ji, ri: (ji, 0, 0)),
                  s_spec, s_spec, pl.BlockSpec((None, 2 * ns, kc), lambda ji, ri: (ji, 0, 0))],
        out_specs=x_spec,
        scratch_shapes=[pltpu.VMEM((tr, kc), BF16), pltpu.VMEM((tr, 2 * ns), BF16)],
        compiler_params=_cparams(("parallel", "parallel")),
        name="s5_out",
    )(x, sc, sh, m8, xf, xb, c8)


def _s5_expand_kernel(a_ref, rep_ref, o_ref, *, row_div, col_div):
    z = _dot(a_ref[...], rep_ref[...])
    rows, cols = z.shape
    row0 = pl.program_id(1) * rows
    gi = ((lax.broadcasted_iota(jnp.int32, (rows, cols), 0) + row0) // row_div) % S5_LANE_GROUPS
    hi = (lax.broadcasted_iota(jnp.int32, (rows, cols), 1) // col_div) % S5_LANE_GROUPS
    o_ref[...] = jnp.where(gi == hi, z, 0.0).astype(o_ref.dtype)


def _s5_expand(a, rep, row_div, col_div, tr=512):
    nj, rows, kin = a.shape
    ncol = rep.shape[1]
    return pl.pallas_call(
        functools.partial(_s5_expand_kernel, row_div=row_div, col_div=col_div),
        out_shape=jax.ShapeDtypeStruct((nj, rows, ncol), BF16),
        grid=(nj, rows // tr),
        in_specs=[pl.BlockSpec((None, tr, kin), lambda j, i: (j, i, 0)),
                  pl.BlockSpec((kin, ncol), lambda j, i: (0, 0))],
        out_specs=pl.BlockSpec((None, tr, ncol), lambda j, i: (j, i, 0)),
        compiler_params=_cparams(("parallel", "parallel")),
        name="s5_expand",
    )(a, rep)


def _s5_tables(a_re, a_im, log_step, b_re, b_im, c_re, c_im):
    t = S5_CHUNK
    step = jnp.exp(log_step.astype(F32))[..., None]
    ar, ai = a_re.astype(F32), a_im.astype(F32)
    mag = jnp.exp(ar * step)
    ab_re, ab_im = mag * jnp.cos(ai * step), mag * jnp.sin(ai * step)
    den = ar * ar + ai * ai
    f_re = ((ab_re - 1.0) * ar + ab_im * ai) / den
    f_im = (ab_im * ar - (ab_re - 1.0) * ai) / den
    br, bi = b_re.astype(F32), b_im.astype(F32)
    bb_re = f_re[..., None] * br - f_im[..., None] * bi
    bb_im = f_re[..., None] * bi + f_im[..., None] * br
    cr, ci = c_re.astype(F32), c_im.astype(F32)
    tau = jnp.arange(t + 1, dtype=F32)[:, None, None, None]
    pmag = jnp.exp(ar[None] * step[None] * tau)
    pw_re = pmag * jnp.cos(ai[None] * step[None] * tau)
    pw_im = pmag * jnp.sin(ai[None] * step[None] * tau)
    g, p, ch = br.shape[1], br.shape[2], br.shape[3]

    def cmul(xr, xi, yr, yi):
        return xr * yr - xi * yi, xr * yi + xi * yr

    def inject(d, powers):
        pr = pw_re[powers, d][:, :, :, None]
        pi = pw_im[powers, d][:, :, :, None]
        wr, wi = cmul(pr, pi, bb_re[d][None], bb_im[d][None])
        w = jnp.concatenate([wr, wi], axis=2)
        return w.transpose(1, 0, 3, 2).reshape(g, t * ch, 2 * p)
    w_in = jnp.concatenate([inject(0, jnp.arange(t - 1, -1, -1)), inject(1, jnp.arange(t))], axis=-1)

    def readout(d, powers):
        pr = pw_re[powers, d][:, :, None, :]
        pi = pw_im[powers, d][:, :, None, :]
        kr, ki = cmul(cr[d][None], ci[d][None], pr, pi)
        m = jnp.concatenate([kr, -ki], axis=3)
        return m.transpose(1, 3, 0, 2).reshape(g, 2 * p, t * ch)
    c_f = readout(0, jnp.arange(1, t + 1))
    c_b = readout(1, jnp.arange(t, 0, -1))

    def lag_kernels(d):
        pr = pw_re[:t, d][:, :, :, None]
        pi = pw_im[:t, d][:, :, :, None]
        wr, wi = cmul(pr, pi, bb_re[d][None], bb_im[d][None])
        return (jnp.einsum('gop,tgpc->tgoc', cr[d], wr) - jnp.einsum('gop,tgpc->tgoc', ci[d], wi))
    kf, kb = lag_kernels(0), lag_kernels(1)
    ii = np.arange(t)
    lag = ii[None, :] - ii[:, None]
    sel_f = (lag[:, :, None] == ii[None, None, :]).astype(np.float32)
    sel_b = (-lag[:, :, None] == ii[None, None, :]).astype(np.float32)
    m = (jnp.einsum('jit,tgoc->gjcio', sel_f, kf, precision=lax.Precision.HIGHEST)
         + jnp.einsum('jit,tgoc->gjcio', sel_b, kb, precision=lax.Precision.HIGHEST))

    at_re, at_im = pw_re[t], pw_im[t]
    a1 = jnp.concatenate([at_re, at_re], axis=-1)
    a2 = jnp.concatenate([-at_im, at_im], axis=-1)

    lg = S5_LANE_GROUPS
    nj = g // lg
    kc = t * lg * ch

    def rows_tgc(a):
        return a.reshape(nj, lg, t, ch, a.shape[-1]).transpose(0, 2, 1, 3, 4).reshape(nj, kc, a.shape[-1]).astype(BF16)

    eye_x = lambda nx, ny: np.einsum('xa,yb->xyab', np.eye(nx), np.eye(ny))[:, :, :, None, :] * np.ones((1, 1, 1, lg, 1))
    rep_m = jnp.asarray(eye_x(t, ch).reshape(t * ch, t * lg * ch), BF16)
    rep_w = jnp.asarray(eye_x(2, 2 * p).reshape(4 * p, 2 * lg * 2 * p), BF16)
    m8 = _s5_expand(rows_tgc(m.reshape(g, t * ch, t * ch)), rep_m, ch, ch)
    win8 = _s5_expand(rows_tgc(w_in), rep_w, ch, 2 * p)
    c_c = jnp.concatenate([c_f.reshape(nj, lg * 2 * p, t * ch), c_b.reshape(nj, lg * 2 * p, t * ch)], axis=1)
    c8 = _s5_expand(c_c.astype(BF16), rep_m, 2 * p, ch)
    return win8, m8, c8, a1, a2


def _dispatch_kernel(pend_ref, dest_ref, h_ref, xs_ref, zero_scr, sem, zsem, *, n_exp):
    tm = h_ref.shape[0]
    bm = zero_scr.shape[0]

    @pl.when(pl.program_id(0) == 0)
    def _():
        zero_scr[...] = jnp.zeros_like(zero_scr)
        for e in range(n_exp):
            prev = pend_ref[e - 1] if e else 0

            @pl.when(pend_ref[e] > prev)
            def _():
                off = pl.multiple_of(pend_ref[e] - bm, bm)
                pltpu.make_async_copy(zero_scr, xs_ref.at[pl.ds(off, bm)], zsem).start()
        n_used = pend_ref[n_exp - 1] // bm
        n_blocks = xs_ref.shape[0] // bm

        def zero_tail(b, carry):
            off = pl.multiple_of(b * bm, bm)
            pltpu.make_async_copy(zero_scr, xs_ref.at[pl.ds(off, bm)], zsem).start()
            return carry

        def wait_tail(b, carry):
            pltpu.make_async_copy(zero_scr, xs_ref.at[pl.ds(0, bm)], zsem).wait()
            return carry

        lax.fori_loop(n_used, n_blocks, zero_tail, 0)
        for e in range(n_exp):
            prev = pend_ref[e - 1] if e else 0

            @pl.when(pend_ref[e] > prev)
            def _():
                pltpu.make_async_copy(zero_scr, xs_ref.at[pl.ds(0, bm)], zsem).wait()
        lax.fori_loop(n_used, n_blocks, wait_tail, 0)

    def issue(t, carry):
        for k in range(TOP_K):
            pltpu.make_async_copy(h_ref.at[pl.ds(t, 1)], xs_ref.at[pl.ds(dest_ref[t * TOP_K + k], 1)], sem).start()
        return carry

    lax.fori_loop(0, tm, issue, 0)
    for _ in range(TOP_K):
        pltpu.make_async_copy(h_ref, xs_ref.at[pl.ds(0, tm)], sem).wait()


def _dispatch(pend, dest, h2, n_rows, tm=512):
    n, d = h2.shape
    return pl.pallas_call(
        functools.partial(_dispatch_kernel, n_exp=pend.shape[0]),
        out_shape=jax.ShapeDtypeStruct((n_rows, d), F32),
        grid_spec=pltpu.PrefetchScalarGridSpec(
            num_scalar_prefetch=1, grid=(n // tm,),
            in_specs=[pl.BlockSpec((tm * TOP_K,), lambda i, p: (i,), memory_space=pltpu.SMEM),
                      pl.BlockSpec((tm, d), lambda i, p: (i, 0))],
            out_specs=pl.BlockSpec(memory_space=pl.ANY),
            scratch_shapes=[pltpu.VMEM((MOE_BLOCK, d), F32), pltpu.SemaphoreType.DMA,
                            pltpu.SemaphoreType.DMA]),
        compiler_params=_cparams(("arbitrary",)),
        name="moe_dispatch",
    )(pend, dest, h2)


def _ffn_kernel(sp_ref, x_ref, wgu_ref, bgu_ref, wd_ref, bd_ref, o_ref, wgu_s, wd_s, *, n_blocks):
    b = pl.program_id(0)
    used = b < sp_ref[3 * n_blocks]
    valid = sp_ref[2 * n_blocks + b]
    dff = wd_ref.shape[0]
    half = x_ref.shape[0] // 2

    @pl.when(jnp.logical_and(used, sp_ref[n_blocks + b] == 1))
    def _():
        wgu_s[...] = wgu_ref[...].astype(BF16)
        wd_s[...] = wd_ref[...].astype(BF16)

    def ffn_rows(rows):
        gu = _dot(x_ref[rows, :].astype(BF16), wgu_s[...]) + bgu_ref[...]
        gate = jnp.minimum(gu[:, :dff], SWIGLU_LIMIT)
        lin = jnp.clip(gu[:, dff:], -SWIGLU_LIMIT, SWIGLU_LIMIT)
        act = gate * jax.nn.sigmoid(SWIGLU_ALPHA * gate) * (lin + 1.0)
        o_ref[rows, :] = _dot(act.astype(BF16), wd_s[...]) + bd_ref[...]

    @pl.when(jnp.logical_and(used, valid > half))
    def _():
        ffn_rows(slice(None))

    @pl.when(jnp.logical_and(used, valid <= half))
    def _():
        ffn_rows(slice(0, half))
        o_ref[half:, :] = jnp.zeros((half, o_ref.shape[1]), o_ref.dtype)

    @pl.when(jnp.logical_not(used))
    def _():
        o_ref[...] = jnp.zeros_like(o_ref)


def _ffn(sp, xs, wgu, bgu, wd, bd, layer):
    n_rows, d = xs.shape
    bm = MOE_BLOCK
    n_blocks = n_rows // bm
    dff = wd.shape[2]
    return pl.pallas_call(
        functools.partial(_ffn_kernel, n_blocks=n_blocks),
        out_shape=jax.ShapeDtypeStruct((n_rows, d), F32),
        grid_spec=pltpu.PrefetchScalarGridSpec(
            num_scalar_prefetch=1, grid=(n_blocks,),
            in_specs=[pl.BlockSpec((bm, d), lambda b, sp: (jnp.minimum(b, sp[3 * n_blocks] - 1), 0)),
                      pl.BlockSpec((None, None, d, 2 * dff), lambda b, sp: (layer, sp[b], 0, 0)),
                      pl.BlockSpec((None, None, 1, 2 * dff), lambda b, sp: (layer, sp[b], 0, 0)),
                      pl.BlockSpec((None, None, dff, d), lambda b, sp: (layer, sp[b], 0, 0)),
                      pl.BlockSpec((None, None, 1, d), lambda b, sp: (layer, sp[b], 0, 0))],
            out_specs=pl.BlockSpec((bm, d), lambda b, sp: (b, 0)),
            scratch_shapes=[pltpu.VMEM((d, 2 * dff), BF16), pltpu.VMEM((dff, d), BF16)]),
        compiler_params=_cparams(("arbitrary",)),
        name="moe_ffn",
    )(sp, xs, wgu, bgu, wd, bd)


def _combine_kernel(dest_ref, yb_ref, gate_ref, x_ref, g_ref, lng_ref, lnb_ref, xo_ref, buf, sem, *, alpha):
    tm = x_ref.shape[0]

    def issue(t, carry):
        for k in range(TOP_K):
            pltpu.make_async_copy(yb_ref.at[pl.ds(dest_ref[t * TOP_K + k], 1)], buf.at[k, pl.ds(t, 1)], sem).start()
        return carry

    lax.fori_loop(0, tm, issue, 0)
    for k in range(TOP_K):
        pltpu.make_async_copy(yb_ref.at[pl.ds(0, tm)], buf.at[k], sem).wait()
    gates = gate_ref[...]
    y = gates[:, 0:1] * buf[0]
    for k in range(1, TOP_K):
        y = y + gates[:, k:k + 1] * buf[k]
    xo_ref[...] = _post_sublayer(alpha, x_ref[...], y, g_ref[...], lng_ref[...], lnb_ref[...])


def _combine(dest, yb, gate_t, x, g2, lng, lnb, alpha, tm=512):
    n, d = x.shape
    return pl.pallas_call(
        functools.partial(_combine_kernel, alpha=alpha),
        out_shape=jax.ShapeDtypeStruct((n, d), F32),
        grid=(n // tm,),
        in_specs=[pl.BlockSpec((tm * TOP_K,), lambda i: (i,), memory_space=pltpu.SMEM),
                  pl.BlockSpec(memory_space=pl.ANY),
                  _row_spec(tm, ROUTER_PAD), _row_spec(tm, d), _blk_vec_spec(tm, d), _const_spec((1, d)),
                  _const_spec((1, d))],
        out_specs=_row_spec(tm, d),
        scratch_shapes=[pltpu.VMEM((TOP_K, tm, d), F32), pltpu.SemaphoreType.DMA],
        compiler_params=_cparams(("arbitrary",)),
        name="moe_combine",
    )(dest, yb, gate_t, x, g2, lng, lnb)


def _moe(x, h2, idx_t, gate_t, rank_t, cnt, g2, lng, lnb, wgu, bgu, wd, bd, layer, alpha):
    n_tok, d = h2.shape
    n_exp = wgu.shape[1]
    bm = MOE_BLOCK
    n_blocks = -(-n_tok * TOP_K // bm) + n_exp
    counts = cnt[0, :n_exp].astype(jnp.int32)
    padded = (counts + bm - 1) // bm * bm
    pend = jnp.cumsum(padded)
    pstart = pend - padded
    e_iota = jnp.arange(n_exp, dtype=jnp.int32)
    dest = rank_t[:, :TOP_K] + jnp.sum(jnp.where(idx_t[:, :TOP_K, None] == e_iota, pstart, 0), axis=-1)
    dest = dest.reshape(-1).astype(jnp.int32)
    blk_row = jnp.arange(n_blocks, dtype=jnp.int32) * bm
    block_e = jnp.minimum(jnp.sum(pend[None, :] <= blk_row[:, None], axis=1), n_exp - 1).astype(jnp.int32)
    first = jnp.concatenate([jnp.ones((1,), jnp.int32), (block_e[1:] != block_e[:-1]).astype(jnp.int32)])
    row_end = jnp.sum(jnp.where(block_e[:, None] == e_iota, pstart + counts, 0), axis=1)
    valid = jnp.clip(row_end - blk_row, 0, bm).astype(jnp.int32)
    sp = jnp.concatenate([block_e, first, valid, (pend[-1:] // bm).astype(jnp.int32)])
    xs = _dispatch(pend.astype(jnp.int32), dest, h2, n_blocks * bm)
    yb = _ffn(sp, xs, wgu, bgu, wd, bd, layer)
    return _combine(dest, yb, gate_t, x, g2, lng, lnb, alpha)


def _block_meta(n_p, l_p, n_s, l_s):
    rb = ROW_BLOCK
    assert l_p % rb == 0 and l_s % rb == 0
    rows = []
    for s in range(n_p):
        for j in range(l_p // rb):
            rows.append((int(j == 0), int(j == l_p // rb - 1), 0, 0, s, 0, s))
    for s in range(n_s):
        for j in range(l_s // rb):
            rows.append((int(j == 0), int(j == l_s // rb - 1), 1, s, n_p - 1, 1 + j, n_p + s))
    meta = np.asarray(rows, np.int32).T
    cond = np.asarray([0] * (n_p * l_p // rb) + [1 + s for s in range(n_s) for _ in range(l_s // rb)], np.int32)
    return meta, cond


def _rotary_tables(l_s, dk):
    rows = l_s // GRID_W
    row = jnp.repeat(jnp.arange(rows, dtype=F32), GRID_W)
    col = jnp.tile(jnp.arange(GRID_W, dtype=F32), rows)
    n_freq = dk // 4
    inv = ROPE_BASE ** (-jnp.arange(n_freq, dtype=F32) / n_freq)
    ang = jnp.concatenate([row[:, None] * inv, col[:, None] * inv], axis=-1)
    ident = jnp.zeros((ROW_BLOCK, dk // 2), F32)
    return (jnp.concatenate([ident + 1.0, jnp.cos(ang)], axis=0),
            jnp.concatenate([ident, jnp.sin(ang)], axis=0))


def kernel(x_prompt, x_sample, state_ret, state_gla, state_s5_re, state_s5_im, c, c_ctx, w_mod, b_mod, ln_g, ln_b, ret_w_in, ret_decay, ret_w_out, gla_w_in, gla_w_a1, gla_w_a2, gla_b_a, gla_w_out, s5_a_re, s5_a_im, s5_log_step, s5_b_re, s5_b_im, s5_c_re, s5_c_im, s5_d, s5_w_glu, moe_w_router, moe_b_router, moe_w_gu, moe_b_gu, moe_w_down, moe_b_down):
    n_p, l_p, d = x_prompt.shape
    n_s, l_s, _ = x_sample.shape
    depth = w_mod.shape[0]
    n_exp = moe_w_router.shape[-1]
    alpha = (2 * depth) ** 0.25
    rb = ROW_BLOCK
    n = n_p * l_p + n_s * l_s
    nb = n // rb
    nseq = n_p + n_s
    meta_np, cond_np = _block_meta(n_p, l_p, n_s, l_s)
    meta = jnp.asarray(meta_np.reshape(-1))

    x = jnp.concatenate([x_prompt.reshape(n_p * l_p, d), x_sample.reshape(n_s * l_s, d)], axis=0)

    ncond = 16
    conds = jnp.zeros((ncond, d), F32).at[0].set(c_ctx).at[1:1 + n_s].set(c)
    mod = _mod_all(conds, w_mod, b_mod)
    mod_b = mod[:, cond_np].reshape(depth, nb, 6, 1, d)

    cos_t, sin_t = _rotary_tables(l_s, d // RET_HEADS)
    lg_ret = jax.nn.log_sigmoid(ret_decay.astype(F32))

    wr_pad = jnp.zeros((depth, d, ROUTER_PAD), BF16).at[:, :, :n_exp].set(moe_w_router.astype(BF16))
    br_pad = jnp.full((depth, 1, ROUTER_PAD), -1e30, F32).at[:, 0, :n_exp].set(moe_b_router)

    new_ret = jnp.zeros((n_p,) + state_ret.shape[1:], F32)
    new_gla, new_s5_re, new_s5_im = [], [], []
    ret_i = gla_i = s5_i = 0
    for layer in range(depth):
        sh1, sc1, g1, sh2, sc2, g2 = [mod_b[layer, :, t] for t in range(6)]
        lng1, lnb1 = ln_g[layer, 0][None], ln_b[layer, 0][None]
        lng2, lnb2 = ln_g[layer, 1][None], ln_b[layer, 1][None]
        wr, br = wr_pad[layer], br_pad[layer]
        kind = layer % 3
        if kind == 0:
            qkvg = _proj(x, sc1, sh1, ret_w_in[ret_i].astype(BF16), tm=1024, tn=1536)
            a, new_ret = _retention(qkvg, meta, lg_ret[ret_i].reshape(-1), cos_t, sin_t, state_ret, new_ret,
                                    ret_i, nb)
            x, h2, idx_t, gate_t, rank_t, cnt = _mixer_out(a, ret_w_out[ret_i].astype(BF16), x, g1, lng1, lnb1, sc2, sh2,
                                              wr, br, alpha)
            ret_i += 1
        elif kind == 1:
            qkvr = _proj(x, sc1, sh1, gla_w_in[gla_i].astype(BF16), tm=1024, tn=1536)
            rank = gla_w_a1.shape[-1]
            key = gla_w_a2.shape[-1]
            w1 = jnp.concatenate([gla_w_a1[gla_i, 0], gla_w_a1[gla_i, 1]], axis=-1).astype(BF16)
            w2 = jnp.zeros((2 * rank, 2 * key), F32)
            w2 = w2.at[:rank, :key].set(gla_w_a2[gla_i, 0]).at[rank:, key:].set(gla_w_a2[gla_i, 1]).astype(BF16)
            la = _gla_gate(x, sc1, sh1, w1, w2, gla_b_a[gla_i].reshape(1, 2 * key))
            a, st = _gla(qkvr, la, meta, state_gla, gla_i, nb, n_p)
            new_gla.append(st)
            x, h2, idx_t, gate_t, rank_t, cnt = _mixer_out(a, gla_w_out[gla_i].astype(BF16), x, g1, lng1, lnb1, sc2, sh2,
                                              wr, br, alpha)
            gla_i += 1
        else:
            t = S5_CHUNK
            g = d // S5_GROUP
            p = s5_a_re.shape[-1]
            r = n // t
            win8, m8, c8, a1, a2 = _s5_tables(
                s5_a_re[s5_i], s5_a_im[s5_i], s5_log_step[s5_i], s5_b_re[s5_i], s5_b_im[s5_i],
                s5_c_re[s5_i], s5_c_im[s5_i])
            lg = S5_LANE_GROUPS
            nj = g // lg
            scr = jnp.repeat(sc1[:, 0], rb // t, axis=0)
            shr = jnp.repeat(sh1[:, 0], rb // t, axis=0)
            wf, wb = _s5_in(x, scr, shr, win8, tr=_row_tile(r, 640))
            s0 = jnp.concatenate([state_s5_re[:, s5_i], state_s5_im[:, s5_i]], axis=-1)
            s0 = jnp.concatenate([jnp.zeros((n_p,) + s0.shape[1:], F32), s0], axis=0)
            s0 = s0.reshape(nseq, 2, nj, lg, 2 * p)
            a1 = a1.reshape(2, nj, lg, 2 * p)
            a2 = a2.reshape(2, nj, lg, 2 * p)
            xf, stf = _s5_scan(wf.reshape(nj, r, lg, 2 * p), a1[0], a2[0], s0[:, 0], meta, nb, nseq, reverse=False)
            xb, stb = _s5_scan(wb.reshape(nj, r, lg, 2 * p), a1[1], a2[1], s0[:, 1], meta, nb, nseq, reverse=True)
            y = _s5_out(x, scr, shr, m8, xf.reshape(nj, r * lg, 2 * p), xb.reshape(nj, r * lg, 2 * p), c8,
                        tr=_row_tile(r, 256))
            st = jnp.stack([stf[:n_p], stb[:n_p]], axis=1).reshape(n_p, 2, g, 2 * p)
            new_s5_re.append(st[..., :p])
            new_s5_im.append(st[..., p:])
            wg = s5_w_glu[s5_i].astype(BF16)
            x, h2, idx_t, gate_t, rank_t, cnt = _glu_out(y, wg[:, :d], wg[:, d:], s5_d[s5_i][None], sc1, sh1, x, g1,
                                            lng1, lnb1, sc2, sh2, wr, br, alpha)
            s5_i += 1
        x = _moe(x, h2, idx_t, gate_t, rank_t, cnt, g2, lng2, lnb2, moe_w_gu, moe_b_gu[:, :, None, :],
                 moe_w_down, moe_b_down[:, :, None, :], layer, alpha)

    y_prompt = x[:n_p * l_p].reshape(n_p, l_p, d)
    y_sample = x[n_p * l_p:].reshape(n_s, l_s, d)
    return (y_prompt, y_sample, new_ret, jnp.stack(new_gla, axis=1),
            jnp.stack(new_s5_re, axis=1), jnp.stack(new_s5_im, axis=1))
```

```python
import functools
import math

import numpy as np
import jax
import jax.numpy as jnp
from jax import lax
from jax.experimental import pallas as pl
from jax.experimental.pallas import tpu as pltpu

F32 = jnp.float32
BF16 = jnp.bfloat16

RET_HEADS = 4
GLA_HEADS = 4
GLA_TAU = 16.0
GRID_W = 64
ROPE_BASE = 10000.0
S5_GROUP = 16
TOP_K = 4
SWIGLU_ALPHA = 1.702
SWIGLU_LIMIT = 7.0
LN_EPS = 1e-5
GN_EPS = 1e-5

ROW_BLOCK = 256
GLA_CHUNK = 64
S5_CHUNK = 16
S5_LANE_GROUPS = 8
MOE_BLOCK = 512
ROUTER_PAD = 128
VMEM_LIMIT = 56 * 1024 * 1024

M_START, M_END, M_SAMPLE, M_SIDX, M_PIDX, M_POS, M_SEQ = range(7)
M_ROWS = 7

NT = (((1,), (1,)), ((), ()))
TN = (((0,), (0,)), ((), ()))


def _cparams(sem):
    return pltpu.CompilerParams(dimension_semantics=sem, vmem_limit_bytes=VMEM_LIMIT)


def _silu(x):
    return x * jax.nn.sigmoid(x)


def _dot(a, b):
    return jnp.dot(a, b, preferred_element_type=F32)


def _mod_kernel(c_ref, w_ref, b_ref, o_ref):
    c = c_ref[...]
    s = _silu(c).astype(BF16)
    o_ref[0] = _dot(s, w_ref[0].astype(BF16)) + b_ref[0]


def _mod_all(conds, w_mod, b_mod):
    depth, d, n6 = w_mod.shape
    nc = conds.shape[0]
    tn = 1536
    return pl.pallas_call(
        _mod_kernel,
        out_shape=jax.ShapeDtypeStruct((depth, nc, n6), F32),
        grid=(depth, n6 // tn),
        in_specs=[pl.BlockSpec((nc, d), lambda l, j: (0, 0)),
                  pl.BlockSpec((1, d, tn), lambda l, j: (l, 0, j)),
                  pl.BlockSpec((1, 1, tn), lambda l, j: (l, 0, j))],
        out_specs=pl.BlockSpec((1, nc, tn), lambda l, j: (l, 0, j)),
        compiler_params=_cparams(("parallel", "parallel")),
    )(conds, w_mod, b_mod.reshape(depth, 1, n6))


def _proj_kernel(x_ref, sc_ref, sh_ref, w_ref, o_ref, h_scr):
    @pl.when(pl.program_id(1) == 0)
    def _():
        h_scr[...] = (x_ref[...] * (1.0 + sc_ref[...]) + sh_ref[...]).astype(BF16)

    o_ref[...] = _dot(h_scr[...], w_ref[...]).astype(o_ref.dtype)


def _proj(x, sc, sh, w, tm, tn, out_dtype=BF16):
    n, d = x.shape
    nout = w.shape[1]
    kb = tm // ROW_BLOCK
    return pl.pallas_call(
        _proj_kernel,
        out_shape=jax.ShapeDtypeStruct((n, nout), out_dtype),
        grid=(n // tm, nout // tn),
        in_specs=[pl.BlockSpec((tm, d), lambda i, j: (i, 0)),
                  pl.BlockSpec((None, 1, d), lambda i, j: (i * kb, 0, 0)),
                  pl.BlockSpec((None, 1, d), lambda i, j: (i * kb, 0, 0)),
                  pl.BlockSpec((d, tn), lambda i, j: (0, j))],
        out_specs=pl.BlockSpec((tm, tn), lambda i, j: (i, j)),
        scratch_shapes=[pltpu.VMEM((tm, d), BF16)],
        compiler_params=_cparams(("parallel", "arbitrary")),
        name="proj",
    )(x, sc, sh, w)


def _layer_norm(z, g, b):
    mu = jnp.mean(z, axis=-1, keepdims=True)
    zc = z - mu
    var = jnp.mean(zc * zc, axis=-1, keepdims=True)
    return zc * lax.rsqrt(var + LN_EPS) * g + b


def _route(h2, wr_ref, br_ref, cnt_scr, idx_ref, gate_ref, rank_ref, cnt_ref):
    logits = _dot(h2.astype(BF16), wr_ref[...]) + br_ref[...]
    tm, width = logits.shape
    col = lax.broadcasted_iota(jnp.int32, (tm, width), 1)
    colf = col.astype(F32)
    l = logits
    vals, idxs = [], []
    for _ in range(TOP_K):
        m = jnp.max(l, axis=-1, keepdims=True)
        ix = jnp.min(jnp.where(l == m, colf, float(width)), axis=-1, keepdims=True)
        vals.append(m)
        idxs.append(ix)
        l = jnp.where(colf == ix, -jnp.inf, l)
    es = [jnp.exp(v - vals[0]) for v in vals]
    den = es[0] + es[1] + es[2] + es[3]
    onehot = jnp.where(l == -jnp.inf, 1.0, 0.0)

    @pl.when(pl.program_id(0) == 0)
    def _():
        cnt_scr[...] = jnp.zeros_like(cnt_scr)

    rn = lax.broadcasted_iota(jnp.int32, (tm, tm), 0)
    rm = lax.broadcasted_iota(jnp.int32, (tm, tm), 1)
    earlier = jnp.where(rn > rm, 1.0, 0.0).astype(BF16)
    cum = _dot(earlier, onehot.astype(BF16)) + cnt_scr[...]
    idx_t = jnp.zeros((tm, width), jnp.int32)
    rank_t = jnp.zeros((tm, width), jnp.int32)
    gate_t = jnp.zeros((tm, width), F32)
    for k in range(TOP_K):
        rank_k = jnp.sum(jnp.where(colf == idxs[k], cum, 0.0), axis=-1, keepdims=True)
        idx_t = jnp.where(col == k, idxs[k].astype(jnp.int32), idx_t)
        rank_t = jnp.where(col == k, rank_k.astype(jnp.int32), rank_t)
        gate_t = jnp.where(col == k, es[k] / den, gate_t)
    idx_ref[...] = idx_t
    gate_ref[...] = gate_t
    rank_ref[...] = rank_t
    cnt_new = cnt_scr[...] + jnp.sum(onehot, axis=0, keepdims=True)
    cnt_scr[...] = cnt_new
    cnt_ref[...] = cnt_new


def _post_sublayer(alpha, x, y, g, lng, lnb):
    return _layer_norm(alpha * x + g * y, lng, lnb)


def _mixer_out_kernel(a_ref, w_ref, x_ref, g_ref, lng_ref, lnb_ref, sc2_ref, sh2_ref,
                      wr_ref, br_ref, xo_ref, h2_ref, idx_ref, gate_ref, rank_ref, cnt_ref, cnt_scr, *, alpha):
    y = _dot(a_ref[...], w_ref[...])
    xn = _post_sublayer(alpha, x_ref[...], y, g_ref[...], lng_ref[...], lnb_ref[...])
    xo_ref[...] = xn
    h2 = xn * (1.0 + sc2_ref[...]) + sh2_ref[...]
    h2_ref[...] = h2
    _route(h2, wr_ref, br_ref, cnt_scr, idx_ref, gate_ref, rank_ref, cnt_ref)


def _glu_out_kernel(y_ref, w1_ref, w2_ref, d_ref, sc1_ref, sh1_ref, x_ref, g_ref, lng_ref, lnb_ref,
                    sc2_ref, sh2_ref, wr_ref, br_ref, xo_ref, h2_ref, idx_ref, gate_ref, rank_ref, cnt_ref,
                    cnt_scr, *, alpha):
    x = x_ref[...]
    h = x * (1.0 + sc1_ref[...]) + sh1_ref[...]
    a = jax.nn.gelu(y_ref[...] + d_ref[...] * h).astype(BF16)
    y = _dot(a, w1_ref[...]) * jax.nn.sigmoid(_dot(a, w2_ref[...]))
    xn = _post_sublayer(alpha, x, y, g_ref[...], lng_ref[...], lnb_ref[...])
    xo_ref[...] = xn
    h2 = xn * (1.0 + sc2_ref[...]) + sh2_ref[...]
    h2_ref[...] = h2
    _route(h2, wr_ref, br_ref, cnt_scr, idx_ref, gate_ref, rank_ref, cnt_ref)


def _row_spec(tm, width):
    return pl.BlockSpec((tm, width), lambda i: (i, 0))


def _blk_vec_spec(tm, d):
    kb = tm // ROW_BLOCK
    return pl.BlockSpec((None, 1, d), lambda i: (i * kb, 0, 0))


def _const_spec(shape):
    return pl.BlockSpec(shape, lambda i: tuple(0 for _ in shape))


def _post_out_shapes(n, d):
    return (jax.ShapeDtypeStruct((n, d), F32), jax.ShapeDtypeStruct((n, d), F32),
            jax.ShapeDtypeStruct((n, ROUTER_PAD), jnp.int32), jax.ShapeDtypeStruct((n, ROUTER_PAD), F32),
            jax.ShapeDtypeStruct((n, ROUTER_PAD), jnp.int32), jax.ShapeDtypeStruct((1, ROUTER_PAD), F32))


def _post_out_specs(tm, d):
    return (_row_spec(tm, d), _row_spec(tm, d), _row_spec(tm, ROUTER_PAD), _row_spec(tm, ROUTER_PAD),
            _row_spec(tm, ROUTER_PAD), _const_spec((1, ROUTER_PAD)))


def _mixer_out(a, w, x, g1, lng, lnb, sc2, sh2, wr, br, alpha, tm=512):
    n, d = x.shape
    kin = a.shape[1]
    return pl.pallas_call(
        functools.partial(_mixer_out_kernel, alpha=alpha),
        out_shape=_post_out_shapes(n, d),
        grid=(n // tm,),
        in_specs=[_row_spec(tm, kin), _const_spec((kin, d)), _row_spec(tm, d), _blk_vec_spec(tm, d),
                  _const_spec((1, d)), _const_spec((1, d)), _blk_vec_spec(tm, d), _blk_vec_spec(tm, d),
                  _const_spec((d, ROUTER_PAD)), _const_spec((1, ROUTER_PAD))],
        out_specs=_post_out_specs(tm, d),
        scratch_shapes=[pltpu.VMEM((1, ROUTER_PAD), F32)],
        compiler_params=_cparams(("arbitrary",)),
        name="mixer_out_route",
    )(a, w, x, g1, lng, lnb, sc2, sh2, wr, br)


def _glu_out(y, w1, w2, dskip, sc1, sh1, x, g1, lng, lnb, sc2, sh2, wr, br, alpha, tm=512):
    n, d = x.shape
    return pl.pallas_call(
        functools.partial(_glu_out_kernel, alpha=alpha),
        out_shape=_post_out_shapes(n, d),
        grid=(n // tm,),
        in_specs=[_row_spec(tm, d), _const_spec((d, d)), _const_spec((d, d)), _const_spec((1, d)),
                  _blk_vec_spec(tm, d), _blk_vec_spec(tm, d), _row_spec(tm, d), _blk_vec_spec(tm, d),
                  _const_spec((1, d)), _const_spec((1, d)), _blk_vec_spec(tm, d), _blk_vec_spec(tm, d),
                  _const_spec((d, ROUTER_PAD)), _const_spec((1, ROUTER_PAD))],
        out_specs=_post_out_specs(tm, d),
        scratch_shapes=[pltpu.VMEM((1, ROUTER_PAD), F32)],
        compiler_params=_cparams(("arbitrary",)),
        name="mixer_out_route",
    )(y, w1, w2, dskip, sc1, sh1, x, g1, lng, lnb, sc2, sh2, wr, br)


def _rotary(x, cos, sin):
    half = x.shape[-1] // 2
    x1, x2 = x[:, :half], x[:, half:]
    return jnp.concatenate([x1 * cos - x2 * sin, x1 * sin + x2 * cos], axis=-1)


def _head_norm(o):
    mu = jnp.mean(o, axis=-1, keepdims=True)
    oc = o - mu
    var = jnp.mean(oc * oc, axis=-1, keepdims=True)
    return oc * lax.rsqrt(var + GN_EPS)


def _ret_fwd_kernel(meta_ref, lg_ref, q_ref, k_ref, v_ref, cos_ref, sin_ref, s0_ref,
                    st_in_ref, o_ref, st_ref, s_scr, *, nb, heads):
    del st_in_ref
    i = pl.program_id(0)
    start = meta_ref[M_START * nb + i]
    end = meta_ref[M_END * nb + i]
    sample = meta_ref[M_SAMPLE * nb + i]
    c = q_ref.shape[0]
    dk = q_ref.shape[1] // heads
    dv = v_ref.shape[1] // heads

    @pl.when(jnp.logical_and(start == 1, sample == 1))
    def _():
        s_scr[...] = s0_ref[...]

    @pl.when(jnp.logical_and(start == 1, sample == 0))
    def _():
        s_scr[...] = jnp.zeros_like(s_scr)

    cos = cos_ref[...]
    sin = sin_ref[...]
    rn = lax.broadcasted_iota(jnp.int32, (c, c), 0)
    rm = lax.broadcasted_iota(jnp.int32, (c, c), 1)
    rel = (rn - rm).astype(F32)
    idx = lax.broadcasted_iota(jnp.int32, (c, 1), 0).astype(F32)
    for h in range(heads):
        lgf = lg_ref[h]
        lgb = lg_ref[heads + h]
        q = _rotary(q_ref[:, h * dk:(h + 1) * dk].astype(F32), cos, sin) * (dk ** -0.5)
        k = _rotary(k_ref[:, h * dk:(h + 1) * dk].astype(F32), cos, sin)
        qb = q.astype(BF16)
        v = v_ref[:, h * dv:(h + 1) * dv]
        dmat = (jnp.where(rel >= 0, jnp.exp(lgf * jnp.maximum(rel, 0.0)), 0.0)
                + jnp.where(rel <= 0, jnp.exp(lgb * jnp.maximum(-rel, 0.0)), 0.0))
        scores = lax.dot_general(qb, k.astype(BF16), NT, preferred_element_type=F32) * dmat
        xi = jnp.exp(lgf * (idx + 1.0))
        zeta = jnp.exp(lgf * (c - 1.0 - idx))
        s_prev = s_scr[h]
        o_ref[:, h * dv:(h + 1) * dv] = (
            _dot(scores.astype(BF16), v) + _dot(qb, s_prev.astype(BF16)) * xi).astype(o_ref.dtype)
        g_chunk = jnp.exp(lgf * jnp.full((1, 1), float(c), F32))
        s_scr[h] = s_prev * g_chunk + lax.dot_general((k * zeta).astype(BF16), v, TN,
                                                       preferred_element_type=F32)

    @pl.when(jnp.logical_and(end == 1, sample == 0))
    def _():
        st_ref[...] = s_scr[...]


def _ret_bwd_kernel(meta_ref, lg_ref, q_ref, k_ref, v_ref, g_ref, cos_ref, sin_ref, s0_ref, of_ref,
                    st_in_ref, o_ref, st_ref, s_scr, *, nb, heads):
    del st_in_ref
    i = nb - 1 - pl.program_id(0)
    start = meta_ref[M_START * nb + i]
    end = meta_ref[M_END * nb + i]
    sample = meta_ref[M_SAMPLE * nb + i]
    c = q_ref.shape[0]
    dk = q_ref.shape[1] // heads
    dv = v_ref.shape[1] // heads

    @pl.when(jnp.logical_and(end == 1, sample == 1))
    def _():
        s_scr[...] = s0_ref[...]

    @pl.when(jnp.logical_and(end == 1, sample == 0))
    def _():
        s_scr[...] = jnp.zeros_like(s_scr)

    cos = cos_ref[...]
    sin = sin_ref[...]
    idx = lax.broadcasted_iota(jnp.int32, (c, 1), 0).astype(F32)
    for h in range(heads):
        lgb = lg_ref[heads + h]
        q = _rotary(q_ref[:, h * dk:(h + 1) * dk].astype(F32), cos, sin) * (dk ** -0.5)
        k = _rotary(k_ref[:, h * dk:(h + 1) * dk].astype(F32), cos, sin)
        v = v_ref[:, h * dv:(h + 1) * dv]
        xi = jnp.exp(lgb * (c - idx))
        zeta = jnp.exp(lgb * idx)
        s_prev = s_scr[h]
        o = of_ref[:, h * dv:(h + 1) * dv].astype(F32) + _dot(q.astype(BF16), s_prev.astype(BF16)) * xi
        g_chunk = jnp.exp(lgb * jnp.full((1, 1), float(c), F32))
        s_scr[h] = s_prev * g_chunk + lax.dot_general((k * zeta).astype(BF16), v, TN,
                                                       preferred_element_type=F32)
        gate = _silu(g_ref[:, h * dv:(h + 1) * dv].astype(F32))
        o_ref[:, h * dv:(h + 1) * dv] = (gate * _head_norm(o)).astype(o_ref.dtype)

    @pl.when(jnp.logical_and(start == 1, sample == 0))
    def _():
        st_ref[...] = s_scr[...]


def _retention(qkvg, meta, lg, cos_t, sin_t, state, new_state, li, nb):
    n = qkvg.shape[0]
    heads = RET_HEADS
    d = qkvg.shape[1] // 6
    dk, dv = d // heads, 2 * d // heads
    c = ROW_BLOCK
    kw = dict(nb=nb, heads=heads)
    r = lambda j: nb - 1 - j
    pos = lambda j, m: m[M_POS * nb + j]
    st_blk = (None, None, None, heads, dk, dv)
    any_spec = pl.BlockSpec(memory_space=pl.ANY)
    of, new_state = pl.pallas_call(
        functools.partial(_ret_fwd_kernel, **kw),
        out_shape=(jax.ShapeDtypeStruct((n, heads * dv), BF16),
                   jax.ShapeDtypeStruct(new_state.shape, new_state.dtype)),
        grid_spec=pltpu.PrefetchScalarGridSpec(
            num_scalar_prefetch=1, grid=(nb,),
            in_specs=[pl.BlockSpec(memory_space=pltpu.SMEM),
                      pl.BlockSpec((c, d), lambda j, m: (j, 0)),
                      pl.BlockSpec((c, d), lambda j, m: (j, 1)),
                      pl.BlockSpec((c, 2 * d), lambda j, m: (j, 1)),
                      pl.BlockSpec((c, dk // 2), lambda j, m: (pos(j, m), 0)),
                      pl.BlockSpec((c, dk // 2), lambda j, m: (pos(j, m), 0)),
                      pl.BlockSpec(st_blk, lambda j, m: (m[M_SIDX * nb + j], li, 0, 0, 0, 0)),
                      any_spec],
            out_specs=(pl.BlockSpec((c, 2 * d), lambda j, m: (j, 0)),
                       pl.BlockSpec(st_blk, lambda j, m: (m[M_PIDX * nb + j], li, 0, 0, 0, 0))),
            scratch_shapes=[pltpu.VMEM((heads, dk, dv), F32)]),
        input_output_aliases={8: 1},
        compiler_params=_cparams(("arbitrary",)),
        name="ret_fwd",
    )(meta, lg, qkvg, qkvg, qkvg, cos_t, sin_t, state, new_state)
    out, new_state = pl.pallas_call(
        functools.partial(_ret_bwd_kernel, **kw),
        out_shape=(jax.ShapeDtypeStruct((n, heads * dv), BF16),
                   jax.ShapeDtypeStruct(new_state.shape, new_state.dtype)),
        grid_spec=pltpu.PrefetchScalarGridSpec(
            num_scalar_prefetch=1, grid=(nb,),
            in_specs=[pl.BlockSpec(memory_space=pltpu.SMEM),
                      pl.BlockSpec((c, d), lambda j, m: (r(j), 0)),
                      pl.BlockSpec((c, d), lambda j, m: (r(j), 1)),
                      pl.BlockSpec((c, 2 * d), lambda j, m: (r(j), 1)),
                      pl.BlockSpec((c, 2 * d), lambda j, m: (r(j), 2)),
                      pl.BlockSpec((c, dk // 2), lambda j, m: (pos(r(j), m), 0)),
                      pl.BlockSpec((c, dk // 2), lambda j, m: (pos(r(j), m), 0)),
                      pl.BlockSpec(st_blk, lambda j, m: (m[M_SIDX * nb + r(j)], li, 1, 0, 0, 0)),
                      pl.BlockSpec((c, 2 * d), lambda j, m: (r(j), 0)),
                      any_spec],
            out_specs=(pl.BlockSpec((c, 2 * d), lambda j, m: (r(j), 0)),
                       pl.BlockSpec(st_blk, lambda j, m: (m[M_PIDX * nb + r(j)], li, 1, 0, 0, 0))),
            scratch_shapes=[pltpu.VMEM((heads, dk, dv), F32)]),
        input_output_aliases={10: 1},
        compiler_params=_cparams(("arbitrary",)),
        name="ret_bwd",
    )(meta, lg, qkvg, qkvg, qkvg, qkvg, cos_t, sin_t, state, of, new_state)
    return out, new_state


def _gla_gate_kernel(x_ref, sc_ref, sh_ref, w1_ref, w2_ref, b_ref, o_ref):
    h = (x_ref[...] * (1.0 + sc_ref[...]) + sh_ref[...]).astype(BF16)
    u = _dot(h, w1_ref[...]).astype(BF16)
    z = _dot(u, w2_ref[...]) + b_ref[...]
    log_sig = jnp.minimum(z, 0.0) - jnp.log(1.0 + jnp.exp(-jnp.abs(z)))
    o_ref[...] = log_sig / GLA_TAU


def _gla_gate(x, sc, sh, w1, w2, b, tm=512):
    n, d = x.shape
    r2 = w1.shape[1]
    nk = w2.shape[1]
    return pl.pallas_call(
        _gla_gate_kernel,
        out_shape=jax.ShapeDtypeStruct((n, nk), F32),
        grid=(n // tm,),
        in_specs=[_row_spec(tm, d), _blk_vec_spec(tm, d), _blk_vec_spec(tm, d), _const_spec((d, r2)),
                  _const_spec((r2, nk)), _const_spec((1, nk))],
        out_specs=_row_spec(tm, nk),
        compiler_params=_cparams(("parallel",)),
    )(x, sc, sh, w1, w2, b)


def _cumsum_rows(tri, x):
    hi = x.astype(BF16)
    r1 = x - hi.astype(F32)
    mid = r1.astype(BF16)
    lo = (r1 - mid.astype(F32)).astype(BF16)
    return _dot(tri, hi) + _dot(tri, mid) + _dot(tri, lo)


def _gla_chunk(q, k, v, la, s_t, tri, mask, mid_row, last_row):
    b = _cumsum_rows(tri, la)
    r = b[mid_row:mid_row + 1, :]
    bl = b[last_row:last_row + 1, :]
    qe = (q * jnp.exp(b - r)).astype(BF16)
    ke = (k * jnp.exp(r - b)).astype(BF16)
    qi = (q * jnp.exp(b)).astype(BF16)
    kd = (k * jnp.exp(bl - b)).astype(BF16)
    sc = lax.dot_general(qe, ke, NT, preferred_element_type=F32)
    sc = jnp.where(mask, sc, 0.0).astype(BF16)
    o = _dot(sc, v) + lax.dot_general(qi, s_t.astype(BF16), NT, preferred_element_type=F32)
    s_new = s_t * jnp.exp(bl) + lax.dot_general(v, kd, TN, preferred_element_type=F32)
    return o, s_new


def _gla_fwd_kernel(meta_ref, q_ref, k_ref, v_ref, la_ref, s0_ref, o_ref, st_ref, s_scr, *, nb, heads):
    i = pl.program_id(0)
    start = meta_ref[M_START * nb + i]
    end = meta_ref[M_END * nb + i]
    sample = meta_ref[M_SAMPLE * nb + i]
    dk = q_ref.shape[1] // heads
    dv = v_ref.shape[1] // heads
    cc = GLA_CHUNK

    @pl.when(jnp.logical_and(start == 1, sample == 1))
    def _():
        for h in range(heads):
            s_scr[h] = s0_ref[h].T

    @pl.when(jnp.logical_and(start == 1, sample == 0))
    def _():
        s_scr[...] = jnp.zeros_like(s_scr)

    rn = lax.broadcasted_iota(jnp.int32, (cc, cc), 0)
    rm = lax.broadcasted_iota(jnp.int32, (cc, cc), 1)
    mask = rn >= rm
    tri = jnp.where(mask, 1.0, 0.0).astype(BF16)
    for h in range(heads):
        ks = slice(h * dk, (h + 1) * dk)
        vs = slice(h * dv, (h + 1) * dv)
        s_t = s_scr[h]
        for ci in range(q_ref.shape[0] // cc):
            sl = slice(ci * cc, (ci + 1) * cc)
            q = q_ref[sl, ks].astype(F32) * (dk ** -0.5)
            o, s_t = _gla_chunk(q, k_ref[sl, ks].astype(F32), v_ref[sl, vs], la_ref[sl, ks], s_t, tri, mask,
                                cc // 2 - 1, cc - 1)
            o_ref[sl, vs] = o.astype(o_ref.dtype)
        s_scr[h] = s_t

    @pl.when(jnp.logical_and(end == 1, sample == 0))
    def _():
        for h in range(heads):
            st_ref[h] = s_scr[h].T


def _gla_bwd_kernel(meta_ref, q_ref, k_ref, v_ref, r_ref, la_ref, s0_ref, of_ref, o_ref, st_ref, s_scr,
                    *, nb, heads):
    i = nb - 1 - pl.program_id(0)
    start = meta_ref[M_START * nb + i]
    end = meta_ref[M_END * nb + i]
    sample = meta_ref[M_SAMPLE * nb + i]
    dk = q_ref.shape[1] // heads
    dv = v_ref.shape[1] // heads
    cc = GLA_CHUNK

    @pl.when(jnp.logical_and(end == 1, sample == 1))
    def _():
        for h in range(heads):
            s_scr[h] = s0_ref[h].T

    @pl.when(jnp.logical_and(end == 1, sample == 0))
    def _():
        s_scr[...] = jnp.zeros_like(s_scr)

    rn = lax.broadcasted_iota(jnp.int32, (cc, cc), 0)
    rm = lax.broadcasted_iota(jnp.int32, (cc, cc), 1)
    mask = rn <= rm
    tri = jnp.where(mask, 1.0, 0.0).astype(BF16)
    for h in range(heads):
        ks = slice(h * dk, (h + 1) * dk)
        vs = slice(h * dv, (h + 1) * dv)
        s_t = s_scr[h]
        for ci in reversed(range(q_ref.shape[0] // cc)):
            sl = slice(ci * cc, (ci + 1) * cc)
            q = q_ref[sl, ks].astype(F32) * (dk ** -0.5)
            ob, s_t = _gla_chunk(q, k_ref[sl, ks].astype(F32), v_ref[sl, vs], la_ref[sl, ks], s_t, tri, mask,
                                 cc // 2, 0)
            o = of_ref[sl, vs].astype(F32) + ob
            o_ref[sl, vs] = (_silu(r_ref[sl, vs].astype(F32)) * _head_norm(o)).astype(o_ref.dtype)
        s_scr[h] = s_t

    @pl.when(jnp.logical_and(start == 1, sample == 0))
    def _():
        for h in range(heads):
            st_ref[h] = s_scr[h].T


def _gla(qkvr, la, meta, state, li, nb, n_prompt):
    n = qkvr.shape[0]
    heads = GLA_HEADS
    d = qkvr.shape[1] // 3
    dk, dv = d // 2 // heads, d // heads
    c = ROW_BLOCK
    kw = dict(nb=nb, heads=heads)
    r = lambda j: nb - 1 - j
    st_blk = (None, None, None, heads, dk, dv)
    of, st_f = pl.pallas_call(
        functools.partial(_gla_fwd_kernel, **kw),
        out_shape=(jax.ShapeDtypeStruct((n, d), BF16), jax.ShapeDtypeStruct((n_prompt, heads, dk, dv), F32)),
        grid_spec=pltpu.PrefetchScalarGridSpec(
            num_scalar_prefetch=1, grid=(nb,),
            in_specs=[pl.BlockSpec((c, d // 2), lambda j, m: (j, 0)),
                      pl.BlockSpec((c, d // 2), lambda j, m: (j, 1)),
                      pl.BlockSpec((c, d), lambda j, m: (j, 1)),
                      pl.BlockSpec((c, d // 2), lambda j, m: (j, 0)),
                      pl.BlockSpec(st_blk, lambda j, m: (m[M_SIDX * nb + j], li, 0, 0, 0, 0))],
            out_specs=(pl.BlockSpec((c, d), lambda j, m: (j, 0)),
                       pl.BlockSpec((None, heads, dk, dv), lambda j, m: (m[M_PIDX * nb + j], 0, 0, 0))),
            scratch_shapes=[pltpu.VMEM((heads, dv, dk), F32)]),
        compiler_params=_cparams(("arbitrary",)),
        name="gla_fwd",
    )(meta, qkvr, qkvr, qkvr, la, state)
    out, st_b = pl.pallas_call(
        functools.partial(_gla_bwd_kernel, **kw),
        out_shape=(jax.ShapeDtypeStruct((n, d), BF16), jax.ShapeDtypeStruct((n_prompt, heads, dk, dv), F32)),
        grid_spec=pltpu.PrefetchScalarGridSpec(
            num_scalar_prefetch=1, grid=(nb,),
            in_specs=[pl.BlockSpec((c, d // 2), lambda j, m: (r(j), 0)),
                      pl.BlockSpec((c, d // 2), lambda j, m: (r(j), 1)),
                      pl.BlockSpec((c, d), lambda j, m: (r(j), 1)),
                      pl.BlockSpec((c, d), lambda j, m: (r(j), 2)),
                      pl.BlockSpec((c, d // 2), lambda j, m: (r(j), 1)),
                      pl.BlockSpec(st_blk, lambda j, m: (m[M_SIDX * nb + r(j)], li, 1, 0, 0, 0)),
                      pl.BlockSpec((c, d), lambda j, m: (r(j), 0))],
            out_specs=(pl.BlockSpec((c, d), lambda j, m: (r(j), 0)),
                       pl.BlockSpec((None, heads, dk, dv), lambda j, m: (m[M_PIDX * nb + r(j)], 0, 0, 0))),
            scratch_shapes=[pltpu.VMEM((heads, dv, dk), F32)]),
        compiler_params=_cparams(("arbitrary",)),
        name="gla_bwd",
    )(meta, qkvr, qkvr, qkvr, qkvr, la, state, of)
    return out, jnp.stack([st_f, st_b], axis=1)


def _row_tile(rows, cap):
    tile = min(cap, rows) // 8 * 8
    while rows % tile:
        tile -= 8
    return tile


def _s5_chunk_rows(x_ref, sc_ref, sh_ref, lhs):
    rows, lanes = sc_ref.shape
    t_len = x_ref.shape[0] // rows
    for t in range(t_len):
        xs = x_ref[pl.ds(t, rows, stride=t_len), :]
        lhs[:, t * lanes:(t + 1) * lanes] = (xs * (1.0 + sc_ref[...]) + sh_ref[...]).astype(BF16)


def _s5_in_kernel(x_ref, sc_ref, sh_ref, w_ref, wf_ref, wb_ref, lhs):
    _s5_chunk_rows(x_ref, sc_ref, sh_ref, lhs)
    z = _dot(lhs[...], w_ref[...])
    rows = sc_ref.shape[0]
    half = z.shape[1] // 2
    p2 = wf_ref.shape[1]
    lg = wf_ref.shape[0] // rows
    for h in range(lg):
        wf_ref[pl.ds(h, rows, stride=lg), :] = z[:, h * p2:(h + 1) * p2]
        wb_ref[pl.ds(h, rows, stride=lg), :] = z[:, half + h * p2:half + (h + 1) * p2]


def _s5_in(x, sc, sh, win8, tr):
    n, d = x.shape
    nj, kc, ncol = win8.shape
    r = sc.shape[0]
    t = n // r
    lanes = kc // t
    lg = S5_LANE_GROUPS
    p2 = ncol // 2 // lg
    x_spec = pl.BlockSpec((tr * t, lanes), lambda ji, ri: (ri, ji))
    v_spec = pl.BlockSpec((tr, lanes), lambda ji, ri: (ri, ji))
    st_spec = pl.BlockSpec((None, tr * lg, p2), lambda ji, ri: (ji, ri, 0))
    return pl.pallas_call(
        _s5_in_kernel,
        out_shape=(jax.ShapeDtypeStruct((nj, r * lg, p2), F32), jax.ShapeDtypeStruct((nj, r * lg, p2), F32)),
        grid=(nj, r // tr),
        in_specs=[x_spec, v_spec, v_spec, pl.BlockSpec((None, kc, ncol), lambda ji, ri: (ji, 0, 0))],
        out_specs=(st_spec, st_spec),
        scratch_shapes=[pltpu.VMEM((tr, kc), BF16)],
        compiler_params=_cparams(("parallel", "parallel")),
        name="s5_in",
    )(x, sc, sh, win8)


def _s5_scan_kernel(meta_ref, w_ref, a1_ref, a2_ref, s0_ref, xp_ref, st_ref, x_scr, *, nb, reverse):
    j = pl.program_id(0)
    i = nb - 1 - j if reverse else j
    first = meta_ref[(M_END if reverse else M_START) * nb + i]
    last = meta_ref[(M_START if reverse else M_END) * nb + i]

    @pl.when(first == 1)
    def _():
        x_scr[...] = s0_ref[...]

    a1 = a1_ref[...]
    a2 = a2_ref[...]
    x = x_scr[...]
    half = x.shape[2] // 2
    nrow = w_ref.shape[1]
    order = range(nrow - 1, -1, -1) if reverse else range(nrow)
    for c in order:
        xp_ref[:, c] = x
        x = x * a1 + pltpu.roll(x, half, axis=2) * a2 + w_ref[:, c]
    x_scr[...] = x

    @pl.when(last == 1)
    def _():
        st_ref[...] = x


def _s5_scan(w, a1, a2, s0, meta, nb, nseq, reverse):
    nj, r, lg, p2 = w.shape
    cr = r // nb
    blk = (lambda j: nb - 1 - j) if reverse else (lambda j: j)
    w_spec = pl.BlockSpec((nj, cr, lg, p2), lambda j, m: (0, blk(j), 0, 0))
    a_spec = pl.BlockSpec((nj, lg, p2), lambda j, m: (0, 0, 0))
    s_spec = pl.BlockSpec((None, nj, lg, p2), lambda j, m: (m[M_SEQ * nb + blk(j)], 0, 0, 0))
    return pl.pallas_call(
        functools.partial(_s5_scan_kernel, nb=nb, reverse=reverse),
        out_shape=(jax.ShapeDtypeStruct((nj, r, lg, p2), F32), jax.ShapeDtypeStruct((nseq, nj, lg, p2), F32)),
        grid_spec=pltpu.PrefetchScalarGridSpec(
            num_scalar_prefetch=1, grid=(nb,),
            in_specs=[w_spec, a_spec, a_spec, s_spec],
            out_specs=(w_spec, s_spec),
            scratch_shapes=[pltpu.VMEM((nj, lg, p2), F32)]),
        compiler_params=_cparams(("arbitrary",)),
        name="s5_scan",
    )(meta, w, a1, a2, s0)


def _s5_out_kernel(x_ref, sc_ref, sh_ref, m_ref, xf_ref, xb_ref, c_ref, y_ref, lhs, states):
    _s5_chunk_rows(x_ref, sc_ref, sh_ref, lhs)
    rows, lanes = sc_ref.shape
    t_len = y_ref.shape[0] // rows
    p2 = xf_ref.shape[1]
    lg = xf_ref.shape[0] // rows
    ns = lg * p2
    for h in range(lg):
        states[:, h * p2:(h + 1) * p2] = xf_ref[pl.ds(h, rows, stride=lg), :].astype(BF16)
        states[:, ns + h * p2:ns + (h + 1) * p2] = xb_ref[pl.ds(h, rows, stride=lg), :].astype(BF16)
    z = _dot(lhs[...], m_ref[...]) + _dot(states[...], c_ref[...])
    for t in range(t_len):
        y_ref[pl.ds(t, rows, stride=t_len), :] = z[:, t * lanes:(t + 1) * lanes]


def _s5_out(x, sc, sh, m8, xf, xb, c8, tr):
    n, d = x.shape
    nj, kc, _ = m8.shape
    r = sc.shape[0]
    t = n // r
    lanes = kc // t
    ns = c8.shape[1] // 2
    lg = S5_LANE_GROUPS
    x_spec = pl.BlockSpec((tr * t, lanes), lambda ji, ri: (ri, ji))
    v_spec = pl.BlockSpec((tr, lanes), lambda ji, ri: (ri, ji))
    s_spec = pl.BlockSpec((None, tr * lg, ns // lg), lambda ji, ri: (ji, ri, 0))
    return pl.pallas_call(
        _s5_out_kernel,
        out_shape=jax.ShapeDtypeStruct((n, d), F32),
        grid=(nj, r // tr),
        in_specs=[x_spec, v_spec, v_spec, pl.BlockSpec((None, kc, kc), lambda ji, ri: (ji, 0, 0)),
                  s_spec, s_spec, pl.BlockSpec((None, 2 * ns, kc), lambda ji, ri: (ji, 0, 0))],
        out_specs=x_spec,
        scratch_shapes=[pltpu.VMEM((tr, kc), BF16), pltpu.VMEM((tr, 2 * ns), BF16)],
        compiler_params=_cparams(("parallel", "parallel")),
        name="s5_out",
    )(x, sc, sh, m8, xf, xb, c8)


def _s5_expand_kernel(a_ref, rep_ref, o_ref, *, row_div, col_div):
    z = _dot(a_ref[...], rep_ref[...])
    rows, cols = z.shape
    row0 = pl.program_id(1) * rows
    gi = ((lax.broadcasted_iota(jnp.int32, (rows, cols), 0) + row0) // row_div) % S5_LANE_GROUPS
    hi = (lax.broadcasted_iota(jnp.int32, (rows, cols), 1) // col_div) % S5_LANE_GROUPS
    o_ref[...] = jnp.where(gi == hi, z, 0.0).astype(o_ref.dtype)


def _s5_expand(a, rep, row_div, col_div, tr=512):
    nj, rows, kin = a.shape
    ncol = rep.shape[1]
    return pl.pallas_call(
        functools.partial(_s5_expand_kernel, row_div=row_div, col_div=col_div),
        out_shape=jax.ShapeDtypeStruct((nj, rows, ncol), BF16),
        grid=(nj, rows // tr),
        in_specs=[pl.BlockSpec((None, tr, kin), lambda j, i: (j, i, 0)),
                  pl.BlockSpec((kin, ncol), lambda j, i: (0, 0))],
        out_specs=pl.BlockSpec((None, tr, ncol), lambda j, i: (j, i, 0)),
        compiler_params=_cparams(("parallel", "parallel")),
        name="s5_expand",
    )(a, rep)


def _s5_tables(a_re, a_im, log_step, b_re, b_im, c_re, c_im):
    t = S5_CHUNK
    step = jnp.exp(log_step.astype(F32))[..., None]
    ar, ai = a_re.astype(F32), a_im.astype(F32)
    mag = jnp.exp(ar * step)
    ab_re, ab_im = mag * jnp.cos(ai * step), mag * jnp.sin(ai * step)
    den = ar * ar + ai * ai
    f_re = ((ab_re - 1.0) * ar + ab_im * ai) / den
    f_im = (ab_im * ar - (ab_re - 1.0) * ai) / den
    br, bi = b_re.astype(F32), b_im.astype(F32)
    bb_re = f_re[..., None] * br - f_im[..., None] * bi
    bb_im = f_re[..., None] * bi + f_im[..., None] * br
    cr, ci = c_re.astype(F32), c_im.astype(F32)
    tau = jnp.arange(t + 1, dtype=F32)[:, None, None, None]
    pmag = jnp.exp(ar[None] * step[None] * tau)
    pw_re = pmag * jnp.cos(ai[None] * step[None] * tau)
    pw_im = pmag * jnp.sin(ai[None] * step[None] * tau)
    g, p, ch = br.shape[1], br.shape[2], br.shape[3]

    def cmul(xr, xi, yr, yi):
        return xr * yr - xi * yi, xr * yi + xi * yr

    def inject(d, powers):
        pr = pw_re[powers, d][:, :, :, None]
        pi = pw_im[powers, d][:, :, :, None]
        wr, wi = cmul(pr, pi, bb_re[d][None], bb_im[d][None])
        w = jnp.concatenate([wr, wi], axis=2)
        return w.transpose(1, 0, 3, 2).reshape(g, t * ch, 2 * p)
    w_in = jnp.concatenate([inject(0, jnp.arange(t - 1, -1, -1)), inject(1, jnp.arange(t))], axis=-1)

    def readout(d, powers):
        pr = pw_re[powers, d][:, :, None, :]
        pi = pw_im[powers, d][:, :, None, :]
        kr, ki = cmul(cr[d][None], ci[d][None], pr, pi)
        m = jnp.concatenate([kr, -ki], axis=3)
        return m.transpose(1, 3, 0, 2).reshape(g, 2 * p, t * ch)
    c_f = readout(0, jnp.arange(1, t + 1))
    c_b = readout(1, jnp.arange(t, 0, -1))

    def lag_kernels(d):
        pr = pw_re[:t, d][:, :, :, None]
        pi = pw_im[:t, d][:, :, :, None]
        wr, wi = cmul(pr, pi, bb_re[d][None], bb_im[d][None])
        return (jnp.einsum('gop,tgpc->tgoc', cr[d], wr) - jnp.einsum('gop,tgpc->tgoc', ci[d], wi))
    kf, kb = lag_kernels(0), lag_kernels(1)
    ii = np.arange(t)
    lag = ii[None, :] - ii[:, None]
    sel_f = (lag[:, :, None] == ii[None, None, :]).astype(np.float32)
    sel_b = (-lag[:, :, None] == ii[None, None, :]).astype(np.float32)
    m = (jnp.einsum('jit,tgoc->gjcio', sel_f, kf, precision=lax.Precision.HIGHEST)
         + jnp.einsum('jit,tgoc->gjcio', sel_b, kb, precision=lax.Precision.HIGHEST))

    at_re, at_im = pw_re[t], pw_im[t]
    a1 = jnp.concatenate([at_re, at_re], axis=-1)
    a2 = jnp.concatenate([-at_im, at_im], axis=-1)

    lg = S5_LANE_GROUPS
    nj = g // lg
    kc = t * lg * ch

    def rows_tgc(a):
        return a.reshape(nj, lg, t, ch, a.shape[-1]).transpose(0, 2, 1, 3, 4).reshape(nj, kc, a.shape[-1]).astype(BF16)

    eye_x = lambda nx, ny: np.einsum('xa,yb->xyab', np.eye(nx), np.eye(ny))[:, :, :, None, :] * np.ones((1, 1, 1, lg, 1))
    rep_m = jnp.asarray(eye_x(t, ch).reshape(t * ch, t * lg * ch), BF16)
    rep_w = jnp.asarray(eye_x(2, 2 * p).reshape(4 * p, 2 * lg * 2 * p), BF16)
    m8 = _s5_expand(rows_tgc(m.reshape(g, t * ch, t * ch)), rep_m, ch, ch)
    win8 = _s5_expand(rows_tgc(w_in), rep_w, ch, 2 * p)
    c_c = jnp.concatenate([c_f.reshape(nj, lg * 2 * p, t * ch), c_b.reshape(nj, lg * 2 * p, t * ch)], axis=1)
    c8 = _s5_expand(c_c.astype(BF16), rep_m, 2 * p, ch)
    return win8, m8, c8, a1, a2


def _dispatch_kernel(pend_ref, dest_ref, h_ref, xs_ref, zero_scr, sem, zsem, *, n_exp):
    tm = h_ref.shape[0]
    bm = zero_scr.shape[0]

    @pl.when(pl.program_id(0) == 0)
    def _():
        zero_scr[...] = jnp.zeros_like(zero_scr)
        for e in range(n_exp):
            prev = pend_ref[e - 1] if e else 0

            @pl.when(pend_ref[e] > prev)
            def _():
                off = pl.multiple_of(pend_ref[e] - bm, bm)
                pltpu.make_async_copy(zero_scr, xs_ref.at[pl.ds(off, bm)], zsem).start()
        n_used = pend_ref[n_exp - 1] // bm
        n_blocks = xs_ref.shape[0] // bm

        def zero_tail(b, carry):
            off = pl.multiple_of(b * bm, bm)
            pltpu.make_async_copy(zero_scr, xs_ref.at[pl.ds(off, bm)], zsem).start()
            return carry

        def wait_tail(b, carry):
            pltpu.make_async_copy(zero_scr, xs_ref.at[pl.ds(0, bm)], zsem).wait()
            return carry

        lax.fori_loop(n_used, n_blocks, zero_tail, 0)
        for e in range(n_exp):
            prev = pend_ref[e - 1] if e else 0

            @pl.when(pend_ref[e] > prev)
            def _():
                pltpu.make_async_copy(zero_scr, xs_ref.at[pl.ds(0, bm)], zsem).wait()
        lax.fori_loop(n_used, n_blocks, wait_tail, 0)

    def issue(t, carry):
        for k in range(TOP_K):
            pltpu.make_async_copy(h_ref.at[pl.ds(t, 1)], xs_ref.at[pl.ds(dest_ref[t * TOP_K + k], 1)], sem).start()
        return carry

    lax.fori_loop(0, tm, issue, 0)
    for _ in range(TOP_K):
        pltpu.make_async_copy(h_ref, xs_ref.at[pl.ds(0, tm)], sem).wait()


def _dispatch(pend, dest, h2, n_rows, tm=512):
    n, d = h2.shape
    return pl.pallas_call(
        functools.partial(_dispatch_kernel, n_exp=pend.shape[0]),
        out_shape=jax.ShapeDtypeStruct((n_rows, d), F32),
        grid_spec=pltpu.PrefetchScalarGridSpec(
            num_scalar_prefetch=1, grid=(n // tm,),
            in_specs=[pl.BlockSpec((tm * TOP_K,), lambda i, p: (i,), memory_space=pltpu.SMEM),
                      pl.BlockSpec((tm, d), lambda i, p: (i, 0))],
            out_specs=pl.BlockSpec(memory_space=pl.ANY),
            scratch_shapes=[pltpu.VMEM((MOE_BLOCK, d), F32), pltpu.SemaphoreType.DMA,
                            pltpu.SemaphoreType.DMA]),
        compiler_params=_cparams(("arbitrary",)),
        name="moe_dispatch",
    )(pend, dest, h2)


def _ffn_kernel(sp_ref, x_ref, wgu_ref, bgu_ref, wd_ref, bd_ref, o_ref, wgu_s, wd_s, *, n_blocks):
    b = pl.program_id(0)
    used = b < sp_ref[3 * n_blocks]
    valid = sp_ref[2 * n_blocks + b]
    dff = wd_ref.shape[0]
    half = x_ref.shape[0] // 2

    @pl.when(jnp.logical_and(used, sp_ref[n_blocks + b] == 1))
    def _():
        wgu_s[...] = wgu_ref[...].astype(BF16)
        wd_s[...] = wd_ref[...].astype(BF16)

    def ffn_rows(rows):
        gu = _dot(x_ref[rows, :].astype(BF16), wgu_s[...]) + bgu_ref[...]
        gate = jnp.minimum(gu[:, :dff], SWIGLU_LIMIT)
        lin = jnp.clip(gu[:, dff:], -SWIGLU_LIMIT, SWIGLU_LIMIT)
        act = gate * jax.nn.sigmoid(SWIGLU_ALPHA * gate) * (lin + 1.0)
        o_ref[rows, :] = _dot(act.astype(BF16), wd_s[...]) + bd_ref[...]

    @pl.when(jnp.logical_and(used, valid > half))
    def _():
        ffn_rows(slice(None))

    @pl.when(jnp.logical_and(used, valid <= half))
    def _():
        ffn_rows(slice(0, half))
        o_ref[half:, :] = jnp.zeros((half, o_ref.shape[1]), o_ref.dtype)

    @pl.when(jnp.logical_not(used))
    def _():
        o_ref[...] = jnp.zeros_like(o_ref)


def _ffn(sp, xs, wgu, bgu, wd, bd, layer):
    n_rows, d = xs.shape
    bm = MOE_BLOCK
    n_blocks = n_rows // bm
    dff = wd.shape[2]
    return pl.pallas_call(
        functools.partial(_ffn_kernel, n_blocks=n_blocks),
        out_shape=jax.ShapeDtypeStruct((n_rows, d), F32),
        grid_spec=pltpu.PrefetchScalarGridSpec(
            num_scalar_prefetch=1, grid=(n_blocks,),
            in_specs=[pl.BlockSpec((bm, d), lambda b, sp: (jnp.minimum(b, sp[3 * n_blocks] - 1), 0)),
                      pl.BlockSpec((None, None, d, 2 * dff), lambda b, sp: (layer, sp[b], 0, 0)),
                      pl.BlockSpec((None, None, 1, 2 * dff), lambda b, sp: (layer, sp[b], 0, 0)),
                      pl.BlockSpec((None, None, dff, d), lambda b, sp: (layer, sp[b], 0, 0)),
                      pl.BlockSpec((None, None, 1, d), lambda b, sp: (layer, sp[b], 0, 0))],
            out_specs=pl.BlockSpec((bm, d), lambda b, sp: (b, 0)),
            scratch_shapes=[pltpu.VMEM((d, 2 * dff), BF16), pltpu.VMEM((dff, d), BF16)]),
        compiler_params=_cparams(("arbitrary",)),
        name="moe_ffn",
    )(sp, xs, wgu, bgu, wd, bd)


def _combine_kernel(dest_ref, yb_ref, gate_ref, x_ref, g_ref, lng_ref, lnb_ref, *rest, alpha, split_tiles):
    if split_tiles is None:
        xo_ref, buf, sem = rest
    else:
        xo_ref, xo2_ref, buf, sem = rest
    tm = x_ref.shape[0]

    def issue(t, carry):
        for k in range(TOP_K):
            pltpu.make_async_copy(yb_ref.at[pl.ds(dest_ref[t * TOP_K + k], 1)], buf.at[k, pl.ds(t, 1)], sem).start()
        return carry

    lax.fori_loop(0, tm, issue, 0)
    for k in range(TOP_K):
        pltpu.make_async_copy(yb_ref.at[pl.ds(0, tm)], buf.at[k], sem).wait()
    gates = gate_ref[...]
    y = gates[:, 0:1] * buf[0]
    for k in range(1, TOP_K):
        y = y + gates[:, k:k + 1] * buf[k]
    xn = _post_sublayer(alpha, x_ref[...], y, g_ref[...], lng_ref[...], lnb_ref[...])
    if split_tiles is None:
        xo_ref[...] = xn
    else:
        @pl.when(pl.program_id(0) < split_tiles)
        def _():
            xo_ref[...] = xn

        @pl.when(pl.program_id(0) >= split_tiles)
        def _():
            xo2_ref[...] = xn


def _combine(dest, yb, gate_t, x, g2, lng, lnb, alpha, split_rows=None, tm=512):
    n, d = x.shape
    if split_rows is None:
        split_tiles = None
        out_shape = jax.ShapeDtypeStruct((n, d), F32)
        out_specs = _row_spec(tm, d)
    else:
        assert split_rows % tm == 0
        split_tiles = split_rows // tm
        out_shape = (jax.ShapeDtypeStruct((split_rows, d), F32), jax.ShapeDtypeStruct((n - split_rows, d), F32))
        out_specs = (pl.BlockSpec((tm, d), lambda i: (jnp.minimum(i, split_tiles - 1), 0)),
                     pl.BlockSpec((tm, d), lambda i: (jnp.maximum(i - split_tiles, 0), 0)))
    return pl.pallas_call(
        functools.partial(_combine_kernel, alpha=alpha, split_tiles=split_tiles),
        out_shape=out_shape,
        grid=(n // tm,),
        in_specs=[pl.BlockSpec((tm * TOP_K,), lambda i: (i,), memory_space=pltpu.SMEM),
                  pl.BlockSpec(memory_space=pl.ANY),
                  _row_spec(tm, ROUTER_PAD), _row_spec(tm, d), _blk_vec_spec(tm, d), _const_spec((1, d)),
                  _const_spec((1, d))],
        out_specs=out_specs,
        scratch_shapes=[pltpu.VMEM((TOP_K, tm, d), F32), pltpu.SemaphoreType.DMA],
        compiler_params=_cparams(("arbitrary",)),
        name="moe_combine",
    )(dest, yb, gate_t, x, g2, lng, lnb)


def _moe(x, h2, idx_t, gate_t, rank_t, cnt, g2, lng, lnb, wgu, bgu, wd, bd, layer, alpha, split_rows=None):
    n_tok, d = h2.shape
    n_exp = wgu.shape[1]
    bm = MOE_BLOCK
    n_blocks = -(-n_tok * TOP_K // bm) + n_exp
    counts = cnt[0, :n_exp].astype(jnp.int32)
    padded = (counts + bm - 1) // bm * bm
    pend = jnp.cumsum(padded)
    pstart = pend - padded
    e_iota = jnp.arange(n_exp, dtype=jnp.int32)
    dest = rank_t[:, :TOP_K] + jnp.sum(jnp.where(idx_t[:, :TOP_K, None] == e_iota, pstart, 0), axis=-1)
    dest = dest.reshape(-1).astype(jnp.int32)
    blk_row = jnp.arange(n_blocks, dtype=jnp.int32) * bm
    block_e = jnp.minimum(jnp.sum(pend[None, :] <= blk_row[:, None], axis=1), n_exp - 1).astype(jnp.int32)
    first = jnp.concatenate([jnp.ones((1,), jnp.int32), (block_e[1:] != block_e[:-1]).astype(jnp.int32)])
    row_end = jnp.sum(jnp.where(block_e[:, None] == e_iota, pstart + counts, 0), axis=1)
    valid = jnp.clip(row_end - blk_row, 0, bm).astype(jnp.int32)
    sp = jnp.concatenate([block_e, first, valid, (pend[-1:] // bm).astype(jnp.int32)])
    xs = _dispatch(pend.astype(jnp.int32), dest, h2, n_blocks * bm)
    yb = _ffn(sp, xs, wgu, bgu, wd, bd, layer)
    return _combine(dest, yb, gate_t, x, g2, lng, lnb, alpha, split_rows)


def _block_meta(n_p, l_p, n_s, l_s):
    rb = ROW_BLOCK
    assert l_p % rb == 0 and l_s % rb == 0
    rows = []
    for s in range(n_p):
        for j in range(l_p // rb):
            rows.append((int(j == 0), int(j == l_p // rb - 1), 0, 0, s, 0, s))
    for s in range(n_s):
        for j in range(l_s // rb):
            rows.append((int(j == 0), int(j == l_s // rb - 1), 1, s, n_p - 1, 1 + j, n_p + s))
    meta = np.asarray(rows, np.int32).T
    cond = np.asarray([0] * (n_p * l_p // rb) + [1 + s for s in range(n_s) for _ in range(l_s // rb)], np.int32)
    return meta, cond


def _rotary_tables(l_s, dk):
    rows = l_s // GRID_W
    row = jnp.repeat(jnp.arange(rows, dtype=F32), GRID_W)
    col = jnp.tile(jnp.arange(GRID_W, dtype=F32), rows)
    n_freq = dk // 4
    inv = ROPE_BASE ** (-jnp.arange(n_freq, dtype=F32) / n_freq)
    ang = jnp.concatenate([row[:, None] * inv, col[:, None] * inv], axis=-1)
    ident = jnp.zeros((ROW_BLOCK, dk // 2), F32)
    return (jnp.concatenate([ident + 1.0, jnp.cos(ang)], axis=0),
            jnp.concatenate([ident, jnp.sin(ang)], axis=0))


def kernel(x_prompt, x_sample, state_ret, state_gla, state_s5_re, state_s5_im, c, c_ctx, w_mod, b_mod, ln_g, ln_b, ret_w_in, ret_decay, ret_w_out, gla_w_in, gla_w_a1, gla_w_a2, gla_b_a, gla_w_out, s5_a_re, s5_a_im, s5_log_step, s5_b_re, s5_b_im, s5_c_re, s5_c_im, s5_d, s5_w_glu, moe_w_router, moe_b_router, moe_w_gu, moe_b_gu, moe_w_down, moe_b_down):
    n_p, l_p, d = x_prompt.shape
    n_s, l_s, _ = x_sample.shape
    depth = w_mod.shape[0]
    n_exp = moe_w_router.shape[-1]
    alpha = (2 * depth) ** 0.25
    rb = ROW_BLOCK
    n = n_p * l_p + n_s * l_s
    nb = n // rb
    nseq = n_p + n_s
    meta_np, cond_np = _block_meta(n_p, l_p, n_s, l_s)
    meta = jnp.asarray(meta_np.reshape(-1))

    x = jnp.concatenate([x_prompt.reshape(n_p * l_p, d), x_sample.reshape(n_s * l_s, d)], axis=0)

    ncond = 16
    conds = jnp.zeros((ncond, d), F32).at[0].set(c_ctx).at[1:1 + n_s].set(c)
    mod = _mod_all(conds, w_mod, b_mod)
    mod_b = mod[:, cond_np].reshape(depth, nb, 6, 1, d)

    cos_t, sin_t = _rotary_tables(l_s, d // RET_HEADS)
    lg_ret = jax.nn.log_sigmoid(ret_decay.astype(F32))

    wr_pad = jnp.zeros((depth, d, ROUTER_PAD), BF16).at[:, :, :n_exp].set(moe_w_router.astype(BF16))
    br_pad = jnp.full((depth, 1, ROUTER_PAD), -1e30, F32).at[:, 0, :n_exp].set(moe_b_router)

    new_ret = jnp.zeros((n_p,) + state_ret.shape[1:], F32)
    new_gla, new_s5_re, new_s5_im = [], [], []
    ret_i = gla_i = s5_i = 0
    for layer in range(depth):
        sh1, sc1, g1, sh2, sc2, g2 = [mod_b[layer, :, t] for t in range(6)]
        lng1, lnb1 = ln_g[layer, 0][None], ln_b[layer, 0][None]
        lng2, lnb2 = ln_g[layer, 1][None], ln_b[layer, 1][None]
        wr, br = wr_pad[layer], br_pad[layer]
        kind = layer % 3
        if kind == 0:
            qkvg = _proj(x, sc1, sh1, ret_w_in[ret_i].astype(BF16), tm=1024, tn=1536)
            a, new_ret = _retention(qkvg, meta, lg_ret[ret_i].reshape(-1), cos_t, sin_t, state_ret, new_ret,
                                    ret_i, nb)
            x, h2, idx_t, gate_t, rank_t, cnt = _mixer_out(a, ret_w_out[ret_i].astype(BF16), x, g1, lng1, lnb1, sc2, sh2,
                                              wr, br, alpha)
            ret_i += 1
        elif kind == 1:
            qkvr = _proj(x, sc1, sh1, gla_w_in[gla_i].astype(BF16), tm=1024, tn=1536)
            rank = gla_w_a1.shape[-1]
            key = gla_w_a2.shape[-1]
            w1 = jnp.concatenate([gla_w_a1[gla_i, 0], gla_w_a1[gla_i, 1]], axis=-1).astype(BF16)
            w2 = jnp.zeros((2 * rank, 2 * key), F32)
            w2 = w2.at[:rank, :key].set(gla_w_a2[gla_i, 0]).at[rank:, key:].set(gla_w_a2[gla_i, 1]).astype(BF16)
            la = _gla_gate(x, sc1, sh1, w1, w2, gla_b_a[gla_i].reshape(1, 2 * key))
            a, st = _gla(qkvr, la, meta, state_gla, gla_i, nb, n_p)
            new_gla.append(st)
            x, h2, idx_t, gate_t, rank_t, cnt = _mixer_out(a, gla_w_out[gla_i].astype(BF16), x, g1, lng1, lnb1, sc2, sh2,
                                              wr, br, alpha)
            gla_i += 1
        else:
            t = S5_CHUNK
            g = d // S5_GROUP
            p = s5_a_re.shape[-1]
            r = n // t
            win8, m8, c8, a1, a2 = _s5_tables(
                s5_a_re[s5_i], s5_a_im[s5_i], s5_log_step[s5_i], s5_b_re[s5_i], s5_b_im[s5_i],
                s5_c_re[s5_i], s5_c_im[s5_i])
            lg = S5_LANE_GROUPS
            nj = g // lg
            scr = jnp.repeat(sc1[:, 0], rb // t, axis=0)
            shr = jnp.repeat(sh1[:, 0], rb // t, axis=0)
            wf, wb = _s5_in(x, scr, shr, win8, tr=_row_tile(r, 640))
            s0 = jnp.concatenate([state_s5_re[:, s5_i], state_s5_im[:, s5_i]], axis=-1)
            s0 = jnp.concatenate([jnp.zeros((n_p,) + s0.shape[1:], F32), s0], axis=0)
            s0 = s0.reshape(nseq, 2, nj, lg, 2 * p)
            a1 = a1.reshape(2, nj, lg, 2 * p)
            a2 = a2.reshape(2, nj, lg, 2 * p)
            xf, stf = _s5_scan(wf.reshape(nj, r, lg, 2 * p), a1[0], a2[0], s0[:, 0], meta, nb, nseq, reverse=False)
            xb, stb = _s5_scan(wb.reshape(nj, r, lg, 2 * p), a1[1], a2[1], s0[:, 1], meta, nb, nseq, reverse=True)
            y = _s5_out(x, scr, shr, m8, xf.reshape(nj, r * lg, 2 * p), xb.reshape(nj, r * lg, 2 * p), c8,
                        tr=_row_tile(r, 256))
            st = jnp.stack([stf[:n_p], stb[:n_p]], axis=1).reshape(n_p, 2, g, 2 * p)
            new_s5_re.append(st[..., :p])
            new_s5_im.append(st[..., p:])
            wg = s5_w_glu[s5_i].astype(BF16)
            x, h2, idx_t, gate_t, rank_t, cnt = _glu_out(y, wg[:, :d], wg[:, d:], s5_d[s5_i][None], sc1, sh1, x, g1,
                                            lng1, lnb1, sc2, sh2, wr, br, alpha)
            s5_i += 1
        x = _moe(x, h2, idx_t, gate_t, rank_t, cnt, g2, lng2, lnb2, moe_w_gu, moe_b_gu[:, :, None, :],
                 moe_w_down, moe_b_down[:, :, None, :], layer, alpha,
                 split_rows=n_p * l_p if layer == depth - 1 else None)

    y_prompt = x[0].reshape(n_p, l_p, d)
    y_sample = x[1].reshape(n_s, l_s, d)
    return (y_prompt, y_sample, new_ret, jnp.stack(new_gla, axis=1),
            jnp.stack(new_s5_re, axis=1), jnp.stack(new_s5_im, axis=1))
```
